```python
import jax, jax.numpy as jnp
from jax import lax
import numpy as np

D_MODEL = 1024
BATCH = 4
SEQ = 8192
DEPTH = 1
DEC_BATCH = 32
DEC_SEQ = 8
PAST_LEN = 16384
PAGE_SIZE = 128

HEAD_DIM = 64
H_A = 6
H_B = 6
H_M = 4
D_A = H_A * HEAD_DIM
D_B = H_B * HEAD_DIM
D_M = H_M * HEAD_DIM
D_MIX = D_A + D_B + D_M
N_MEM = 256
MOBA_BLOCK = 256
MOBA_TOPK = 3
IDX_HEADS = 4
IDX_DIM = 64
IDX_TOPK = 256
ROPE_THETA = 500000.0
ROT_DIM = HEAD_DIM // 4
MOBA_Q_CHUNK = 32
DSA_Q_CHUNK = 128
EPS = 1e-6
SIZES = [D_A] * 4 + [D_B] * 4 + [IDX_HEADS * IDX_DIM, IDX_DIM, IDX_HEADS, D_M, D_M]
SPLIT_AT = [int(s) for s in np.cumsum(SIZES)[:-1]]
D_IN = int(sum(SIZES))

kernel_name = 'hymba_moba_dsa_mem_decode_step'


def _rmsnorm(x, g):
    xf = x.astype(jnp.float32)
    y = xf * lax.rsqrt(jnp.mean(xf * xf, axis=-1, keepdims=True) + EPS)
    return (y * g.astype(jnp.float32)).astype(x.dtype)


def _rope(x, pos):
    half = ROT_DIM // 2
    inv = jnp.power(ROPE_THETA, -jnp.arange(half, dtype=jnp.float32) / half)
    ang = pos.astype(jnp.float32)[:, None] * inv[None, :]
    cos = jnp.cos(ang)[:, None, :].astype(x.dtype)
    sin = jnp.sin(ang)[:, None, :].astype(x.dtype)
    x1 = x[..., :half]
    x2 = x[..., half:ROT_DIM]
    return jnp.concatenate([x1 * cos - x2 * sin, x1 * sin + x2 * cos, x[..., ROT_DIM:]], axis=-1)


def _over_query_chunks(fn, chunk, q_args, pos):
    tq = pos.shape[0]
    if tq <= chunk or tq % chunk:
        return fn(*q_args, pos)
    n = tq // chunk
    split = lambda a: jnp.moveaxis(a.reshape(a.shape[0], n, chunk, *a.shape[2:]), 1, 0)
    xs = tuple(split(a) for a in q_args) + (pos.reshape(n, chunk),)
    outs = lax.map(lambda args: fn(*args), xs)
    outs = jnp.moveaxis(outs, 0, 1)
    return outs.reshape(outs.shape[0], tq, *outs.shape[3:])


def _moba_attention(q, k, v, q_pos):
    b, l, h, d = k.shape
    nb = -(-l // MOBA_BLOCK)
    pad = nb * MOBA_BLOCK - l
    kb = jnp.pad(k, ((0, 0), (0, pad), (0, 0), (0, 0))).reshape(b, nb, MOBA_BLOCK, h, d)
    vb = jnp.pad(v, ((0, 0), (0, pad), (0, 0), (0, 0))).reshape(b, nb, MOBA_BLOCK, h, d)
    kmean = jnp.mean(kb.astype(jnp.float32), axis=2).astype(k.dtype)
    n_sel = min(MOBA_TOPK, nb)
    scale = HEAD_DIM ** -0.5
    bi = jnp.arange(b)[:, None, None, None]
    hi = jnp.arange(h)[None, :, None, None]
    jj = jnp.arange(MOBA_BLOCK)

    def attend(qc, pc):
        tc = pc.shape[0]
        own = pc // MOBA_BLOCK
        gate = jnp.einsum('bthd,bnhd->bhtn', qc, kmean).astype(jnp.float32)
        past = jnp.arange(nb)[None, :] < own[:, None]
        gate = jnp.where(past[None, None], gate, -jnp.inf)
        _, top = lax.top_k(gate, n_sel)
        top_ok = top < own[None, None, :, None]
        own_b = jnp.broadcast_to(own[None, None, :, None], (b, h, tc, 1))
        sel = jnp.concatenate([top, own_b], axis=-1)
        valid = jnp.concatenate([top_ok, jnp.ones_like(own_b, dtype=bool)], axis=-1)
        kg = kb[bi, sel, :, hi]
        vg = vb[bi, sel, :, hi]
        kpos = sel[..., None] * MOBA_BLOCK + jj
        mask = valid[..., None] & (kpos <= pc[None, None, :, None, None])
        s_all = sel.shape[-1] * MOBA_BLOCK
        kg = kg.reshape(b, h, tc, s_all, d)
        vg = vg.reshape(b, h, tc, s_all, d)
        s = jnp.einsum('bthd,bhtkd->bhtk', qc, kg).astype(jnp.float32) * scale
        s = jnp.where(mask.reshape(b, h, tc, s_all), s, -jnp.inf)
        p = jax.nn.softmax(s, axis=-1).astype(v.dtype)
        return jnp.einsum('bhtk,bhtkd->bthd', p, vg)

    return _over_query_chunks(attend, MOBA_Q_CHUNK, (q,), q_pos)


def _dsa_attention(q, k, v, iq, iw, ik, q_pos):
    b, l = k.shape[0], k.shape[1]
    n_sel = min(IDX_TOPK, l // 4)
    kpos = jnp.arange(l)
    scale = HEAD_DIM ** -0.5
    idx_scale = IDX_DIM ** -0.5
    w_scale = IDX_HEADS ** -0.5
    bi = jnp.arange(b)[:, None, None]

    def attend(qc, iqc, iwc, pc):
        rel = jax.nn.relu(jnp.einsum('btnd,bsd->btns', iqc, ik).astype(jnp.float32) * idx_scale)
        score = jnp.einsum('btn,btns->bts', iwc.astype(jnp.float32) * w_scale, rel)
        score = jnp.where((kpos[None, :] <= pc[:, None])[None], score, -jnp.inf)
        _, sel = lax.top_k(score, n_sel)
        ok = sel <= pc[None, :, None]
        kg = k[bi, sel]
        vg = v[bi, sel]
        s = jnp.einsum('bthd,btkhd->bhtk', qc, kg).astype(jnp.float32) * scale
        s = jnp.where(ok[:, None], s, -jnp.inf)
        p = jax.nn.softmax(s, axis=-1).astype(v.dtype)
        return jnp.einsum('bhtk,btkhd->bthd', p, vg)

    return _over_query_chunks(attend, DSA_Q_CHUNK, (q, iq, iw), q_pos)


def _mem_attention(q, mk, mv):
    s = jnp.einsum('bthd,bmhd->bhtm', q, mk).astype(jnp.float32) * (HEAD_DIM ** -0.5)
    p = jax.nn.softmax(s, axis=-1).astype(mv.dtype)
    return jnp.einsum('bhtm,bmhd->bthd', p, mv)


def _mem_kv(mem, g, w_mem_kv):
    b = mem.shape[0]
    mk, mv = jnp.split(_rmsnorm(mem, g) @ w_mem_kv, 2, axis=-1)
    return mk.reshape(b, N_MEM, H_M, HEAD_DIM), mv.reshape(b, N_MEM, H_M, HEAD_DIM)


def _project(x, pos, g, w_in):
    b, t, _ = x.shape
    u = _rmsnorm(x, g) @ w_in
    qa, ka, va, ga, qb, kb, vb, gb, iq, ik, iw, qm, gm = jnp.split(u, SPLIT_AT, axis=-1)
    heads = lambda a, n: a.reshape(b, t, n, HEAD_DIM)
    qa = _rope(heads(qa, H_A), pos)
    ka = _rope(heads(ka, H_A), pos)
    va = heads(va, H_A)
    qb = _rope(heads(qb, H_B), pos)
    kb = _rope(heads(kb, H_B), pos)
    vb = heads(vb, H_B)
    iq = _rope(iq.reshape(b, t, IDX_HEADS, IDX_DIM), pos)
    ik = _rope(ik[:, :, None, :], pos)[:, :, 0]
    qm = heads(qm, H_M)
    return qa, ka, va, ga, qb, kb, vb, gb, iq, ik, iw, qm, gm


def _merge(x, oa, ga, ob, gb, om, gm, w_out):
    b, t, _ = x.shape
    y = jnp.concatenate([oa.reshape(b, t, D_A) * jax.nn.silu(ga),
                         ob.reshape(b, t, D_B) * jax.nn.silu(gb),
                         om.reshape(b, t, D_M) * jax.nn.silu(gm)], axis=-1)
    return x + y @ w_out


def setup_inputs(seed: int = 0) -> dict:
    key = jax.random.key(seed)
    ks = jax.random.split(key, 20)
    n_pages = PAST_LEN // PAGE_SIZE
    n_used = DEC_BATCH * n_pages
    n_pool = n_used + (n_used + 3) // 4
    nrm = lambda k, s, sc=1.0: jax.random.normal(k, s, dtype=jnp.float32) * sc
    page_table = jax.random.permutation(ks[0], n_pool)[:n_used].reshape(DEC_BATCH, n_pages).astype(jnp.int32)
    return {
        'x_prompt': nrm(ks[1], (BATCH, SEQ, D_MODEL)),
        'x_sample': nrm(ks[2], (DEC_BATCH, DEC_SEQ, D_MODEL)),
        'mem_prompt': nrm(ks[3], (BATCH, N_MEM, D_MODEL)),
        'cache_k_a': nrm(ks[4], (DEPTH, n_pool, PAGE_SIZE, H_A, HEAD_DIM)),
        'cache_v_a': nrm(ks[5], (DEPTH, n_pool, PAGE_SIZE, H_A, HEAD_DIM)),
        'cache_k_b': nrm(ks[6], (DEPTH, n_pool, PAGE_SIZE, H_B, HEAD_DIM)),
        'cache_v_b': nrm(ks[7], (DEPTH, n_pool, PAGE_SIZE, H_B, HEAD_DIM)),
        'cache_k_idx': nrm(ks[8], (DEPTH, n_pool, PAGE_SIZE, IDX_DIM)),
        'cache_mem_k': nrm(ks[9], (DEPTH, DEC_BATCH, N_MEM, H_M, HEAD_DIM)),
        'cache_mem_v': nrm(ks[10], (DEPTH, DEC_BATCH, N_MEM, H_M, HEAD_DIM)),
        'page_table': page_table,
        'norm_g': 1.0 + nrm(ks[11], (DEPTH, D_MODEL), 0.1),
        'w_in': nrm(ks[12], (DEPTH, D_MODEL, D_IN), D_MODEL ** -0.5),
        'mem_norm_g': 1.0 + nrm(ks[13], (DEPTH, D_MODEL), 0.1),
        'w_mem_kv': nrm(ks[14], (DEPTH, D_MODEL, 2 * D_M), D_MODEL ** -0.5),
        'w_out': nrm(ks[15], (DEPTH, D_MIX, D_MODEL), D_MIX ** -0.5),
        'final_norm_g': 1.0 + nrm(ks[16], (D_MODEL,), 0.1),
    }


def reference(x_prompt, x_sample, mem_prompt, cache_k_a, cache_v_a, cache_k_b, cache_v_b,
              cache_k_idx, cache_mem_k, cache_mem_v, page_table, norm_g, w_in, mem_norm_g,
              w_mem_kv, w_out, final_norm_g):
    n_dec, n_pages = page_table.shape
    past = n_pages * cache_k_a.shape[2]
    pos_p = jnp.arange(x_prompt.shape[1], dtype=jnp.int32)
    pos_s = past + jnp.arange(x_sample.shape[1], dtype=jnp.int32)

    def paged(c, l):
        g = c[l][page_table]
        return g.reshape(n_dec, past, *c.shape[3:])

    xp, xs = x_prompt, x_sample
    kap, vap, kbp, vbp, kip, mkp, mvp = [], [], [], [], [], [], []
    kas, vas, kbs, vbs, kis = [], [], [], [], []
    for l in range(DEPTH):
        qa, ka, va, ga, qb, kb, vb, gb, iq, ik, iw, qm, gm = _project(xp, pos_p, norm_g[l], w_in[l])
        mk, mv = _mem_kv(mem_prompt, mem_norm_g[l], w_mem_kv[l])
        oa = _moba_attention(qa, ka, va, pos_p)
        ob = _dsa_attention(qb, kb, vb, iq, iw, ik, pos_p)
        om = _mem_attention(qm, mk, mv)
        xp = _merge(xp, oa, ga, ob, gb, om, gm, w_out[l])
        kap.append(ka); vap.append(va); kbp.append(kb); vbp.append(vb); kip.append(ik)
        mkp.append(mk); mvp.append(mv)

        qa, ka, va, ga, qb, kb, vb, gb, iq, ik, iw, qm, gm = _project(xs, pos_s, norm_g[l], w_in[l])
        ka_all = jnp.concatenate([paged(cache_k_a, l), ka], axis=1)
        va_all = jnp.concatenate([paged(cache_v_a, l), va], axis=1)
        kb_all = jnp.concatenate([paged(cache_k_b, l), kb], axis=1)
        vb_all = jnp.concatenate([paged(cache_v_b, l), vb], axis=1)
        ik_all = jnp.concatenate([paged(cache_k_idx, l), ik], axis=1)
        oa = _moba_attention(qa, ka_all, va_all, pos_s)
        ob = _dsa_attention(qb, kb_all, vb_all, iq, iw, ik_all, pos_s)
        om = _mem_attention(qm, cache_mem_k[l], cache_mem_v[l])
        xs = _merge(xs, oa, ga, ob, gb, om, gm, w_out[l])
        kas.append(ka); vas.append(va); kbs.append(kb); vbs.append(vb); kis.append(ik)

    y_prompt = _rmsnorm(xp, final_norm_g)
    y_sample = _rmsnorm(xs, final_norm_g)
    return (y_prompt, y_sample,
            jnp.stack(kap), jnp.stack(vap), jnp.stack(kbp), jnp.stack(vbp), jnp.stack(kip),
            jnp.stack(mkp), jnp.stack(mvp),
            jnp.stack(kas), jnp.stack(vas), jnp.stack(kbs), jnp.stack(vbs), jnp.stack(kis))
```

```python
import functools

import jax
import jax.numpy as jnp
import numpy as np
from jax import lax
from jax.experimental import pallas as pl
from jax.experimental.pallas import tpu as pltpu

HEAD_DIM = 64
H_A = 6
H_B = 6
H_M = 4
D_A = H_A * HEAD_DIM
D_B = H_B * HEAD_DIM
D_M = H_M * HEAD_DIM
N_MEM = 256
MOBA_BLOCK = 256
MOBA_TOPK = 3
IDX_HEADS = 4
IDX_DIM = 64
IDX_TOPK = 256
ROPE_THETA = 500000.0
ROT_DIM = HEAD_DIM // 4
EPS = 1e-6

LANES = 128
NEG = -1e30
INT_MIN = -(2 ** 31)
VMEM_LIMIT = 60 * 1024 * 1024

_G = D_A
OFF_QA, OFF_KA, OFF_VA, OFF_GA = 0, _G, 2 * _G, 3 * _G
OFF_QB, OFF_KB, OFF_VB, OFF_GB = 4 * _G, 5 * _G, 6 * _G, 7 * _G
OFF_IQ = 8 * _G
OFF_IK = OFF_IQ + 256
OFF_IW = OFF_IK + LANES
OFF_QM = OFF_IW + LANES
OFF_GM = OFF_QM + D_M
W_PAD = OFF_GM + D_M

F32 = jnp.float32
BF16 = jnp.bfloat16
NT = (((1,), (1,)), ((), ()))


def _cparams(*sem):
    return pltpu.CompilerParams(dimension_semantics=sem, vmem_limit_bytes=VMEM_LIMIT)


def _half_mask(hh):
    lane = lax.broadcasted_iota(jnp.int32, (1, LANES), 1)
    return (lane >= HEAD_DIM * hh) & (lane < HEAD_DIM * (hh + 1))


def _sortable(score):
    bits = lax.bitcast_convert_type(score, jnp.int32)
    key = bits ^ (lax.shift_right_arithmetic(bits, 31) & jnp.int32(0x7FFFFFFF))
    return jnp.where(score == 0.0, jnp.int32(0), key)


def _proj_kernel(x_ref, g_ref, w_ref, tc_ref, ts1_ref, ts2_ref,
                 qa_ref, ka_ref, va_ref, ga_ref, qb_ref, kb_ref, vb_ref, gb_ref,
                 iq_ref, ik_ref, iw_ref, qm_ref, gm_ref,
                 kabf_ref, vabf_ref, kbbf_ref, vbbf_ref, ikbf_ref, kmean_ref):
    x = x_ref[...]
    ms = jnp.mean(x * x, axis=-1, keepdims=True)
    h = (x * lax.rsqrt(ms + EPS) * g_ref[...]).astype(BF16)
    tc, ts1, ts2 = tc_ref[...], ts1_ref[...], ts2_ref[...]

    def mm(off, width):
        return jnp.dot(h, w_ref[:, off:off + width], preferred_element_type=F32)

    def rope(u):
        outs = []
        for c in range(u.shape[1] // LANES):
            uc = u[:, c * LANES:(c + 1) * LANES]
            outs.append(uc * tc + pltpu.roll(uc, LANES - ROT_DIM // 2, 1) * ts1
                        + pltpu.roll(uc, ROT_DIM // 2, 1) * ts2)
        return outs[0] if len(outs) == 1 else jnp.concatenate(outs, axis=1)

    qa_ref[...] = rope(mm(OFF_QA, D_A))
    ka = rope(mm(OFF_KA, D_A))
    ka_ref[...] = ka
    kabf_ref[...] = ka.astype(BF16)
    kmean_ref[0] = jnp.mean(ka, axis=0, keepdims=True)
    va = mm(OFF_VA, D_A)
    va_ref[...] = va
    vabf_ref[...] = va.astype(BF16)
    ga_ref[...] = mm(OFF_GA, D_A)
    qb_ref[...] = rope(mm(OFF_QB, D_B))
    kb = rope(mm(OFF_KB, D_B))
    kb_ref[...] = kb
    kbbf_ref[...] = kb.astype(BF16)
    vb = mm(OFF_VB, D_B)
    vb_ref[...] = vb
    vbbf_ref[...] = vb.astype(BF16)
    gb_ref[...] = mm(OFF_GB, D_B)
    iq_ref[...] = rope(mm(OFF_IQ, IDX_HEADS * IDX_DIM))
    ik2 = rope(mm(OFF_IK, LANES))
    ik_ref[...] = ik2[:, :IDX_DIM]
    ikbf_ref[...] = ik2.astype(BF16)
    iw_ref[...] = mm(OFF_IW, LANES)
    qm_ref[...] = mm(OFF_QM, D_M)
    gm_ref[...] = mm(OFF_GM, D_M)


def _project(x2d, g, w_pad, tabs, n_tab_tiles, tm):
    n, d = x2d.shape
    nt = n // tm
    tok = lambda w: pl.BlockSpec((tm, w), lambda i: (i, 0))
    tab = pl.BlockSpec((tm, LANES), lambda i: (i % n_tab_tiles, 0))
    widths = [D_A] * 4 + [D_B] * 4 + [IDX_HEADS * IDX_DIM, IDX_DIM, LANES, D_M, D_M]
    out_shape = [jax.ShapeDtypeStruct((n, w), F32) for w in widths]
    out_specs = [tok(w) for w in widths]
    for w in (D_A, D_A, D_B, D_B, LANES):
        out_shape.append(jax.ShapeDtypeStruct((n, w), BF16))
        out_specs.append(tok(w))
    out_shape.append(jax.ShapeDtypeStruct((nt, 1, D_A), F32))
    out_specs.append(pl.BlockSpec((1, 1, D_A), lambda i: (i, 0, 0)))
    return pl.pallas_call(
        _proj_kernel,
        grid=(nt,),
        in_specs=[tok(d),
                  pl.BlockSpec((1, d), lambda i: (0, 0)),
                  pl.BlockSpec((d, W_PAD), lambda i: (0, 0)),
                  tab, tab, tab],
        out_specs=out_specs,
        out_shape=out_shape,
        compiler_params=_cparams("parallel"),
        name="proj",
    )(x2d, g, w_pad, *tabs)


def _memkv_kernel(m_ref, g_ref, w_ref, mk_ref, mv_ref):
    x = m_ref[0]
    ms = jnp.mean(x * x, axis=-1, keepdims=True)
    h = (x * lax.rsqrt(ms + EPS) * g_ref[...]).astype(BF16)
    u = jnp.dot(h, w_ref[...], preferred_element_type=F32)
    mk_ref[0] = u[:, :D_M]
    mv_ref[0] = u[:, D_M:]


def _mem_kv(mem, g, w_bf):
    b, nm, d = mem.shape
    return pl.pallas_call(
        _memkv_kernel,
        grid=(b,),
        in_specs=[pl.BlockSpec((1, nm, d), lambda i: (i, 0, 0)),
                  pl.BlockSpec((1, d), lambda i: (0, 0)),
                  pl.BlockSpec((d, 2 * D_M), lambda i: (0, 0))],
        out_specs=[pl.BlockSpec((1, nm, D_M), lambda i: (i, 0, 0))] * 2,
        out_shape=[jax.ShapeDtypeStruct((b, nm, D_M), F32)] * 2,
        compiler_params=_cparams("parallel"),
        name="mem_kv",
    )(mem, g, w_bf)


def _moba_kernel(q_ref, k_ref, v_ref, km_ref, o_ref):
    blk = MOBA_BLOCK
    qi = pl.program_id(1)
    row = lax.broadcasted_iota(jnp.int32, (blk, blk), 0)
    col = lax.broadcasted_iota(jnp.int32, (blk, blk), 1)
    causal = col <= row
    blk_id = lax.broadcasted_iota(jnp.int32, (blk, LANES), 1)
    blk_f = blk_id.astype(F32)
    past = blk_id < qi
    lane = lax.broadcasted_iota(jnp.int32, (1, LANES), 1)
    outs = []
    for c in range(D_A // LANES):
        cs = slice(c * LANES, (c + 1) * LANES)
        qc = q_ref[0, :, cs]
        kmc = km_ref[0, :, cs]
        heads = []
        for hh in range(2):
            qm = jnp.where(_half_mask(hh), qc, 0.0)
            g = lax.dot_general(qm, kmc, NT, precision=lax.Precision.HIGHEST,
                                preferred_element_type=F32)
            g = jnp.where(past, g, -jnp.inf)
            sel = jnp.zeros((blk, LANES), jnp.bool_)
            for _ in range(MOBA_TOPK):
                m = jnp.max(g, axis=1, keepdims=True)
                first = jnp.min(jnp.where(g == m, blk_f, 1e9), axis=1, keepdims=True)
                pick = (blk_f == first) & past
                sel = sel | pick
                g = jnp.where(pick, -jnp.inf, g)
            selbias = jnp.where(sel, 0.0, NEG).astype(BF16)
            q_bf = (qm * (HEAD_DIM ** -0.5)).astype(BF16)
            q_aug = jnp.concatenate([q_bf, selbias], axis=1)

            own0 = pl.multiple_of(qi * blk, blk)
            ko = k_ref[0, pl.ds(own0, blk), cs]
            vo = v_ref[0, pl.ds(own0, blk), cs]
            s = lax.dot_general(q_bf, ko, NT, preferred_element_type=F32)
            s = jnp.where(causal, s, NEG)
            m0 = jnp.max(s, axis=1, keepdims=True)
            p = jnp.exp(s - m0)
            l0 = jnp.sum(p, axis=1, keepdims=True)
            acc0 = jnp.dot(p.astype(BF16), vo, preferred_element_type=F32)

            def body(j, carry, q_aug=q_aug, cs=cs):
                m, l, acc = carry
                j0 = pl.multiple_of(j * blk, blk)
                kj = k_ref[0, pl.ds(j0, blk), cs]
                vj = v_ref[0, pl.ds(j0, blk), cs]
                onehot = jnp.where(blk_id == j, 1.0, 0.0).astype(BF16)
                k_aug = jnp.concatenate([kj, onehot], axis=1)
                s = lax.dot_general(q_aug, k_aug, NT, preferred_element_type=F32)
                m_new = jnp.maximum(m, jnp.max(s, axis=1, keepdims=True))
                alpha = jnp.exp(m - m_new)
                p = jnp.exp(s - m_new)
                l = alpha * l + jnp.sum(p, axis=1, keepdims=True)
                acc = alpha * acc + jnp.dot(p.astype(BF16), vj, preferred_element_type=F32)
                return m_new, l, acc

            m, l, acc = lax.fori_loop(0, qi, body, (m0, l0, acc0))
            heads.append(acc / l)
        outs.append(jnp.where(lane < HEAD_DIM, heads[0], heads[1]))
    o_ref[0] = jnp.concatenate(outs, axis=1)


def _moba_prompt(qa, ka_bf, va_bf, kmean_pad):
    b, t, _ = qa.shape
    blk = MOBA_BLOCK
    return pl.pallas_call(
        _moba_kernel,
        grid=(b, t // blk),
        in_specs=[pl.BlockSpec((1, blk, D_A), lambda i, j: (i, j, 0)),
                  pl.BlockSpec((1, t, D_A), lambda i, j: (i, 0, 0)),
                  pl.BlockSpec((1, t, D_A), lambda i, j: (i, 0, 0)),
                  pl.BlockSpec((1, LANES, D_A), lambda i, j: (i, 0, 0))],
        out_specs=pl.BlockSpec((1, blk, D_A), lambda i, j: (i, j, 0)),
        out_shape=jax.ShapeDtypeStruct((b, t, D_A), F32),
        compiler_params=_cparams("parallel", "arbitrary"),
        name="moba_prompt",
    )(qa, ka_bf, va_bf, kmean_pad)


def _topk_bias(key_scr, cst_scr, n_tiles, k_sel, idx_bits, tile_cols, row_pos, col_pos):
    rows = key_scr.shape[1]
    kf = float(k_sel)

    def count(pred):
        def body(j, acc):
            w = jnp.where(pred(key_scr[j], j), 1.0, 0.0)
            for s in range(tile_cols // LANES):
                acc = acc + w[:, s * LANES:(s + 1) * LANES]
            return acc
        acc = lax.fori_loop(0, n_tiles, body, jnp.zeros((rows, LANES), F32))
        return jnp.sum(acc, axis=1, keepdims=True)

    zero = jnp.zeros((rows, 1), jnp.int32)
    cnt0 = count(lambda k, j: k >= zero)
    thr = jnp.where(cnt0 >= kf, zero, jnp.int32(INT_MIN))

    def bit_body(i, thr):
        cand = thr | lax.shift_left(jnp.int32(1), 30 - i)
        cnt = count(lambda k, j: k >= cand)
        return jnp.where(cnt >= kf, cand, thr)

    thr = lax.fori_loop(0, 31, bit_body, thr)
    cnt_ge = count(lambda k, j: k >= thr)
    need = kf - count(lambda k, j: k > thr)
    cst_scr[...] = jnp.full(cst_scr.shape, jnp.int32(2 ** 30))

    @pl.when(jnp.max(cnt_ge) > kf)
    def _():
        def cbit(i, cpos):
            cand = cpos | lax.shift_left(jnp.int32(1), idx_bits - 1 - i)
            cnt = count(lambda k, j: (k == thr) & (col_pos(j) < cand))
            return jnp.where(cnt < need, cand, cpos)
        cpos = lax.fori_loop(0, idx_bits, cbit, zero)
        cst_scr[...] = jnp.broadcast_to(cpos, cst_scr.shape)

    cstar = cst_scr[:, :1]

    def to_bias(j, carry):
        k = key_scr[j]
        cp = col_pos(j)
        keep = ((k > thr) | ((k == thr) & (cp <= cstar))) & (cp <= row_pos)
        key_scr[j] = lax.bitcast_convert_type(jnp.where(keep, 0.0, NEG).astype(F32), jnp.int32)
        return carry

    lax.fori_loop(0, n_tiles, to_bias, 0)


def _dsa_kernel(q_ref, iq_ref, iw_ref, ik_ref, k_ref, v_ref, o_ref, key_scr, cst_scr, *, k_sel, idx_bits):
    tq = q_ref.shape[1]
    qi = pl.program_id(1)
    n_tiles = qi + 1
    row = lax.broadcasted_iota(jnp.int32, (tq, tq), 0)
    col = lax.broadcasted_iota(jnp.int32, (tq, tq), 1)
    row_pos = row + qi * tq
    lane = lax.broadcasted_iota(jnp.int32, (1, LANES), 1)
    col_pos = lambda j: col + j * tq

    iq = iq_ref[0]
    iw = iw_ref[0] * (IDX_HEADS ** -0.5)
    iqm = [jnp.where(_half_mask(n % 2), iq[:, (n // 2) * LANES:(n // 2 + 1) * LANES], 0.0).astype(BF16)
           for n in range(IDX_HEADS)]

    def score_body(j, carry):
        j0 = pl.multiple_of(j * tq, tq)
        ikj = ik_ref[0, pl.ds(j0, tq), :]
        score = jnp.zeros((tq, tq), F32)
        for n in range(IDX_HEADS):
            rel = lax.dot_general(iqm[n], ikj, NT, preferred_element_type=F32) * (IDX_DIM ** -0.5)
            score = score + iw[:, n:n + 1] * jnp.maximum(rel, 0.0)
        key = _sortable(score)
        key_scr[j] = jnp.where(col_pos(j) > row_pos, jnp.int32(INT_MIN), key)
        return carry

    lax.fori_loop(0, n_tiles, score_body, 0)
    _topk_bias(key_scr, cst_scr, n_tiles, k_sel, idx_bits, tq, row_pos, col_pos)

    outs = []
    for c in range(D_B // LANES):
        cs = slice(c * LANES, (c + 1) * LANES)
        qc = q_ref[0, :, cs]
        heads = []
        for hh in range(2):
            q_bf = (jnp.where(_half_mask(hh), qc, 0.0) * (HEAD_DIM ** -0.5)).astype(BF16)

            def body(j, carry, q_bf=q_bf, cs=cs):
                m, l, acc = carry
                j0 = pl.multiple_of(j * tq, tq)
                kj = k_ref[0, pl.ds(j0, tq), cs]
                vj = v_ref[0, pl.ds(j0, tq), cs]
                bias = lax.bitcast_convert_type(key_scr[j], F32)
                s = lax.dot_general(q_bf, kj, NT, preferred_element_type=F32) + bias
                m_new = jnp.maximum(m, jnp.max(s, axis=1, keepdims=True))
                alpha = jnp.exp(m - m_new)
                p = jnp.exp(s - m_new)
                l = alpha * l + jnp.sum(p, axis=1, keepdims=True)
                acc = alpha * acc + jnp.dot(p.astype(BF16), vj, preferred_element_type=F32)
                return m_new, l, acc

            init = (jnp.full((tq, 1), NEG, F32), jnp.zeros((tq, 1), F32), jnp.zeros((tq, LANES), F32))
            m, l, acc = lax.fori_loop(0, n_tiles, body, init)
            heads.append(acc / l)
        outs.append(jnp.where(lane < HEAD_DIM, heads[0], heads[1]))
    o_ref[0] = jnp.concatenate(outs, axis=1)


def _dsa_prompt(qb, iq, iw, ik_bf, kb_bf, vb_bf):
    b, t, _ = qb.shape
    tq = 256
    k_sel = min(IDX_TOPK, t // 4)
    kern = functools.partial(_dsa_kernel, k_sel=k_sel, idx_bits=max(1, (t - 1).bit_length()))
    res = lambda w: pl.BlockSpec((1, t, w), lambda i, j: (i, 0, 0))
    til = lambda w: pl.BlockSpec((1, tq, w), lambda i, j: (i, j, 0))
    return pl.pallas_call(
        kern,
        grid=(b, t // tq),
        in_specs=[til(D_B), til(IDX_HEADS * IDX_DIM), til(LANES), res(LANES), res(D_B), res(D_B)],
        out_specs=til(D_B),
        out_shape=jax.ShapeDtypeStruct((b, t, D_B), F32),
        scratch_shapes=[pltpu.VMEM((t // tq, tq, tq), jnp.int32), pltpu.VMEM((tq, LANES), jnp.int32)],
        compiler_params=_cparams("parallel", "arbitrary"),
        name="dsa_prompt",
    )(qb, iq, iw, ik_bf, kb_bf, vb_bf)


def _merge_kernel(x_ref, oa_ref, ga_ref, ob_ref, gb_ref, qm_ref, gm_ref, mk_ref, mv_ref, wo_ref, fg_ref, y_ref,
                  *, final):
    lane = lax.broadcasted_iota(jnp.int32, (1, LANES), 1)
    mk = mk_ref[0].astype(BF16)
    mv = mv_ref[0].astype(BF16)
    oms = []
    for c in range(D_M // LANES):
        cs = slice(c * LANES, (c + 1) * LANES)
        qc = qm_ref[0, :, cs]
        heads = []
        for hh in range(2):
            q_bf = (jnp.where(_half_mask(hh), qc, 0.0) * (HEAD_DIM ** -0.5)).astype(BF16)
            s = lax.dot_general(q_bf, mk[:, cs], NT, preferred_element_type=F32)
            m = jnp.max(s, axis=1, keepdims=True)
            p = jnp.exp(s - m)
            l = jnp.sum(p, axis=1, keepdims=True)
            heads.append(jnp.dot(p.astype(BF16), mv[:, cs], preferred_element_type=F32) / l)
        oms.append(jnp.where(lane < HEAD_DIM, heads[0], heads[1]))
    om = jnp.concatenate(oms, axis=1)
    silu = lambda g: g / (1.0 + jnp.exp(-g))
    ymix = jnp.concatenate([oa_ref[0] * silu(ga_ref[0]), ob_ref[0] * silu(gb_ref[0]), om * silu(gm_ref[0])],
                           axis=1).astype(BF16)
    z = x_ref[0] + jnp.dot(ymix, wo_ref[...], preferred_element_type=F32)
    if final:
        ms = jnp.mean(z * z, axis=-1, keepdims=True)
        z = z * lax.rsqrt(ms + EPS) * fg_ref[...]
    y_ref[0] = z


def _merge(x, oa, ga, ob, gb, qm, gm, mk, mv, wo_bf, fg, tm, final):
    b, t, d = x.shape
    til = lambda w: pl.BlockSpec((1, tm, w), lambda i, j: (i, j, 0))
    memspec = pl.BlockSpec((1, N_MEM, D_M), lambda i, j: (i, 0, 0))
    return pl.pallas_call(
        functools.partial(_merge_kernel, final=final),
        grid=(b, t // tm),
        in_specs=[til(d), til(D_A), til(D_A), til(D_B), til(D_B), til(D_M), til(D_M), memspec, memspec,
                  pl.BlockSpec((d, d), lambda i, j: (0, 0)),
                  pl.BlockSpec((1, d), lambda i, j: (0, 0))],
        out_specs=til(d),
        out_shape=jax.ShapeDtypeStruct((b, t, d), F32),
        compiler_params=_cparams("parallel", "parallel"),
        name="merge",
    )(x, oa, ga, ob, gb, qm, gm, mk, mv, wo_bf, fg)


def _expand_heads(q, n_heads):
    t, w = q.shape
    qt = jnp.concatenate([q] * n_heads, axis=0)
    r = lax.broadcasted_iota(jnp.int32, (n_heads * t, w), 0)
    l = lax.broadcasted_iota(jnp.int32, (n_heads * t, w), 1)
    keep = (l >= (r // t) * HEAD_DIM) & (l < (r // t + 1) * HEAD_DIM)
    return jnp.where(keep, qt, 0.0)


def _collapse_heads(o, n_heads, t):
    w = o.shape[1]
    l = lax.broadcasted_iota(jnp.int32, (t, w), 1)
    out = jnp.zeros((t, w), F32)
    for h in range(n_heads):
        keep = (l >= h * HEAD_DIM) & (l < (h + 1) * HEAD_DIM)
        out = out + jnp.where(keep, o[h * t:(h + 1) * t, :], 0.0)
    return out


def _moba_dec_kernel(pt_ref, q_ref, k0_ref, k1_ref, v0_ref, v1_ref, kn_ref, vn_ref, o_ref,
                     o_scr, m_scr, l_scr, g_scr, *, n_new):
    j = pl.program_id(1)
    nb = pl.num_programs(1)
    t = q_ref.shape[1]
    rows = H_A * t
    q = q_ref[0]
    qx = _expand_heads(q, H_A)
    qx_bf = (qx * (HEAD_DIM ** -0.5)).astype(BF16)

    k2 = jnp.concatenate([k0_ref[0], k1_ref[0]], axis=0)
    v2 = jnp.concatenate([v0_ref[0], v1_ref[0]], axis=0)
    kmean = jnp.mean(k2, axis=0, keepdims=True)
    gate = jnp.sum(qx * kmean, axis=1, keepdims=True)
    s = lax.dot_general(qx_bf, k2.astype(BF16), NT, preferred_element_type=F32)
    m = jnp.max(s, axis=1, keepdims=True)
    p = jnp.exp(s - m)
    l = jnp.sum(p, axis=1, keepdims=True)
    o_scr[j] = jnp.dot(p.astype(BF16), v2.astype(BF16), preferred_element_type=F32)
    m_scr[j] = jnp.broadcast_to(m, (rows, LANES))
    l_scr[j] = jnp.broadcast_to(l, (rows, LANES))
    g_scr[j] = jnp.broadcast_to(gate, (rows, LANES))

    @pl.when(j == nb - 1)
    def _():
        npad = kn_ref.shape[1]
        r = lax.broadcasted_iota(jnp.int32, (rows, npad), 0)
        cidx = lax.broadcasted_iota(jnp.int32, (rows, npad), 1)
        ok = cidx <= (r % t)
        s = lax.dot_general(qx_bf, kn_ref[0].astype(BF16), NT, preferred_element_type=F32)
        s = jnp.where(ok, s, NEG)
        m_own = jnp.max(s, axis=1, keepdims=True)
        p = jnp.exp(s - m_own)
        l_own = jnp.sum(p, axis=1, keepdims=True)
        o_own = jnp.dot(p.astype(BF16), vn_ref[0].astype(BF16), preferred_element_type=F32)

        n_blocks = g_scr.shape[0]
        g = g_scr[...]
        bidx = lax.broadcasted_iota(jnp.int32, g.shape, 0).astype(F32)
        sel = jnp.zeros(g.shape, jnp.bool_)
        for _ in range(min(MOBA_TOPK, n_blocks)):
            mx = jnp.max(g, axis=0, keepdims=True)
            first = jnp.min(jnp.where(g == mx, bidx, 1e9), axis=0, keepdims=True)
            pick = bidx == first
            sel = sel | pick
            g = jnp.where(pick, -jnp.inf, g)
        ms = jnp.where(sel, m_scr[...], NEG)
        big = jnp.maximum(jnp.max(ms, axis=0), jnp.broadcast_to(m_own, (rows, LANES)))
        w = jnp.where(sel, jnp.exp(ms - big[None]), 0.0)
        w_own = jnp.exp(jnp.broadcast_to(m_own, (rows, LANES)) - big)
        l_tot = jnp.sum(w * l_scr[...], axis=0) + w_own * jnp.broadcast_to(l_own, (rows, LANES))
        w3 = jnp.concatenate([w] * (D_A // LANES), axis=2)
        o_tot = jnp.sum(w3 * o_scr[...], axis=0) + jnp.concatenate([w_own] * (D_A // LANES), axis=1) * o_own
        o_n = o_tot / jnp.concatenate([l_tot] * (D_A // LANES), axis=1)
        o_ref[0] = _collapse_heads(o_n, H_A, t)


def _paged_specs(width, per_step):
    return [pl.BlockSpec((1, 128, width), lambda s, j, pt, i=i: (pt[s, j * per_step + i], 0, 0))
            for i in range(per_step)]


def _moba_decode(page_table, qa, cache_k, cache_v, kn_pad, vn_pad, page_size):
    nseq, t, _ = qa.shape
    n_pages = page_table.shape[1]
    ppb = MOBA_BLOCK // page_size
    nb = n_pages // ppb
    rows = H_A * t
    seq = lambda w, r: pl.BlockSpec((1, r, w), lambda s, j, pt: (s, 0, 0))
    kspecs = _paged_specs(D_A, ppb)
    grid_spec = pltpu.PrefetchScalarGridSpec(
        num_scalar_prefetch=1,
        grid=(nseq, nb),
        in_specs=[seq(D_A, t)] + kspecs + kspecs + [seq(D_A, kn_pad.shape[1])] * 2,
        out_specs=seq(D_A, t),
        scratch_shapes=[pltpu.VMEM((nb, rows, D_A), F32)] + [pltpu.VMEM((nb, rows, LANES), F32)] * 3,
    )
    return pl.pallas_call(
        functools.partial(_moba_dec_kernel, n_new=t),
        grid_spec=grid_spec,
        out_shape=jax.ShapeDtypeStruct((nseq, t, D_A), F32),
        compiler_params=_cparams("parallel", "arbitrary"),
        name="moba_decode",
    )(page_table, qa, cache_k, cache_k, cache_v, cache_v, kn_pad, vn_pad)


def _idx_dec_kernel(pt_ref, iq_ref, iw_ref, *rest, per_step, k_sel, idx_bits, past, t):
    ik_refs = rest[:per_step]
    ikn_ref, bias_ref, key_scr, cst_scr = rest[per_step:]
    j = pl.program_id(1)
    nsteps = pl.num_programs(1)
    n_pages = key_scr.shape[0] - 1
    iq_bf = iq_ref[0].astype(BF16)
    iw = iw_ref[0]

    def scores(ik):
        rel = lax.dot_general(iq_bf, ik.astype(BF16), NT, preferred_element_type=F32) * (IDX_DIM ** -0.5)
        wrel = iw * jnp.maximum(rel, 0.0)
        sc = wrel[0:t]
        for n in range(1, IDX_HEADS):
            sc = sc + wrel[n * t:(n + 1) * t]
        return sc

    for i in range(per_step):
        key_scr[j * per_step + i] = _sortable(scores(ik_refs[i][0]))

    @pl.when(j == nsteps - 1)
    def _():
        r = lax.broadcasted_iota(jnp.int32, (t, LANES), 0)
        cidx = lax.broadcasted_iota(jnp.int32, (t, LANES), 1)
        key_new = _sortable(scores(ikn_ref[0]))
        key_scr[n_pages] = jnp.where(cidx <= r, key_new, jnp.int32(INT_MIN))
        col_pos = lambda jj: cidx + jj * LANES
        _topk_bias(key_scr, cst_scr, n_pages + 1, k_sel, idx_bits, LANES, r + past, col_pos)
        bias_ref[0] = lax.bitcast_convert_type(key_scr[...], F32)


def _idx_decode(page_table, iq_rows, iw_rows, cache_ik, ikn_pad, page_size, t):
    nseq = iq_rows.shape[0]
    n_pages = page_table.shape[1]
    per_step = 8
    while n_pages % per_step:
        per_step //= 2
    past = n_pages * page_size
    total = past + t
    k_sel = min(IDX_TOPK, total // 4)
    idx_bits = max(1, (past + LANES - 1).bit_length())
    seq = lambda r, w: pl.BlockSpec((1, r, w), lambda s, j, pt: (s, 0, 0))
    grid_spec = pltpu.PrefetchScalarGridSpec(
        num_scalar_prefetch=1,
        grid=(nseq, n_pages // per_step),
        in_specs=[seq(IDX_HEADS * t, IDX_DIM), seq(IDX_HEADS * t, LANES)] + _paged_specs(IDX_DIM, per_step)
        + [seq(LANES, IDX_DIM)],
        out_specs=pl.BlockSpec((1, n_pages + 1, t, LANES), lambda s, j, pt: (s, 0, 0, 0)),
        scratch_shapes=[pltpu.VMEM((n_pages + 1, t, LANES), jnp.int32), pltpu.VMEM((t, LANES), jnp.int32)],
    )
    kern = functools.partial(_idx_dec_kernel, per_step=per_step, k_sel=k_sel, idx_bits=idx_bits, past=past, t=t)
    return pl.pallas_call(
        kern,
        grid_spec=grid_spec,
        out_shape=jax.ShapeDtypeStruct((nseq, n_pages + 1, t, LANES), F32),
        compiler_params=_cparams("parallel", "arbitrary"),
        name="idx_decode",
    )(page_table, iq_rows, iw_rows, *([cache_ik] * per_step), ikn_pad)


def _dsa_dec_kernel(pt_ref, q_ref, b0_ref, b1_ref, bn_ref, k0_ref, k1_ref, v0_ref, v1_ref, kn_ref, vn_ref,
                    o_ref, m_scr, l_scr, acc_scr):
    j = pl.program_id(1)
    nb = pl.num_programs(1)
    t = q_ref.shape[1]
    rows = H_B * t
    qx_bf = (_expand_heads(q_ref[0], H_B) * (HEAD_DIM ** -0.5)).astype(BF16)

    @pl.when(j == 0)
    def _():
        m_scr[...] = jnp.full(m_scr.shape, NEG, F32)
        l_scr[...] = jnp.zeros(l_scr.shape, F32)
        acc_scr[...] = jnp.zeros(acc_scr.shape, F32)

    def step(k, v, bias):
        s = lax.dot_general(qx_bf, k.astype(BF16), NT, preferred_element_type=F32)
        s = s + jnp.concatenate([bias] * H_B, axis=0)
        m_old = m_scr[:, :1]
        m_new = jnp.maximum(m_old, jnp.max(s, axis=1, keepdims=True))
        alpha = jnp.exp(m_old - m_new)
        p = jnp.exp(s - m_new)
        l_scr[...] = jnp.broadcast_to(alpha * l_scr[:, :1] + jnp.sum(p, axis=1, keepdims=True), l_scr.shape)
        acc_scr[...] = alpha * acc_scr[...] + jnp.dot(p.astype(BF16), v.astype(BF16), preferred_element_type=F32)
        m_scr[...] = jnp.broadcast_to(m_new, m_scr.shape)

    k2 = jnp.concatenate([k0_ref[0], k1_ref[0]], axis=0)
    v2 = jnp.concatenate([v0_ref[0], v1_ref[0]], axis=0)
    bias2 = jnp.concatenate([b0_ref[0, 0], b1_ref[0, 0]], axis=1)
    step(k2, v2, bias2)

    @pl.when(j == nb - 1)
    def _():
        step(kn_ref[0], vn_ref[0], bn_ref[0, 0])
        o_n = acc_scr[...] / l_scr[:, :1]
        o_ref[0] = _collapse_heads(o_n, H_B, t)


def _dsa_decode(page_table, qb, bias, cache_k, cache_v, kn_pad, vn_pad):
    nseq, t, _ = qb.shape
    n_pages = page_table.shape[1]
    nb = n_pages // 2
    rows = H_B * t
    seq = lambda w, r: pl.BlockSpec((1, r, w), lambda s, j, pt: (s, 0, 0))
    bspec = lambda i: pl.BlockSpec((1, 1, t, LANES), lambda s, j, pt, i=i: (s, 2 * j + i, 0, 0))
    bnew = pl.BlockSpec((1, 1, t, LANES), lambda s, j, pt: (s, n_pages, 0, 0))
    kspecs = _paged_specs(D_B, 2)
    grid_spec = pltpu.PrefetchScalarGridSpec(
        num_scalar_prefetch=1,
        grid=(nseq, nb),
        in_specs=[seq(D_B, t), bspec(0), bspec(1), bnew] + kspecs + kspecs + [seq(D_B, kn_pad.shape[1])] * 2,
        out_specs=seq(D_B, t),
        scratch_shapes=[pltpu.VMEM((rows, LANES), F32), pltpu.VMEM((rows, LANES), F32),
                        pltpu.VMEM((rows, D_B), F32)],
    )
    return pl.pallas_call(
        _dsa_dec_kernel,
        grid_spec=grid_spec,
        out_shape=jax.ShapeDtypeStruct((nseq, t, D_B), F32),
        compiler_params=_cparams("parallel", "arbitrary"),
        name="dsa_decode",
    )(page_table, qb, bias, bias, bias, cache_k, cache_k, cache_v, cache_v, kn_pad, vn_pad)


def _rope_tables(pos):
    half = ROT_DIM // 2
    inv = jnp.power(ROPE_THETA, -jnp.arange(half, dtype=F32) / half)
    ang = pos.astype(F32)[:, None] * inv[None, :]
    cos, sin = jnp.cos(ang), jnp.sin(ang)
    n = pos.shape[0]
    ones = jnp.ones((n, HEAD_DIM - ROT_DIM), F32)
    zeros = jnp.zeros((n, HEAD_DIM - ROT_DIM), F32)
    zh = jnp.zeros((n, half), F32)
    tc = jnp.concatenate([cos, cos, ones], axis=1)
    ts1 = jnp.concatenate([-sin, zh, zeros], axis=1)
    ts2 = jnp.concatenate([zh, sin, zeros], axis=1)
    dup = lambda a: jnp.concatenate([a, a], axis=1)
    return dup(tc), dup(ts1), dup(ts2)


def _pad_w_in(w):
    d = w.shape[0]
    main = w[:, :8 * _G]
    o = 8 * _G
    iq = w[:, o:o + 256]
    ik = w[:, o + 256:o + 320]
    iw = w[:, o + 320:o + 324]
    qm = w[:, o + 324:o + 324 + D_M]
    gm = w[:, o + 324 + D_M:o + 324 + 2 * D_M]
    iw_pad = jnp.concatenate([iw, jnp.zeros((d, LANES - IDX_HEADS), w.dtype)], axis=1)
    return jnp.concatenate([main, iq, ik, ik, iw_pad, qm, gm], axis=1).astype(BF16)


def kernel(x_prompt, x_sample, mem_prompt, cache_k_a, cache_v_a, cache_k_b, cache_v_b, cache_k_idx,
           cache_mem_k, cache_mem_v, page_table, norm_g, w_in, mem_norm_g, w_mem_kv, w_out, final_norm_g):
    depth = w_in.shape[0]
    b, t, d = x_prompt.shape
    nseq, ts, _ = x_sample.shape
    n_pages = page_table.shape[1]
    page_size = cache_k_a.shape[2]
    past = n_pages * page_size
    n_pool = cache_k_a.shape[1]
    assert t % MOBA_BLOCK == 0 and t // MOBA_BLOCK <= LANES and page_size == LANES
    assert past % MOBA_BLOCK == 0 and ts <= LANES and ts % 8 == 0

    tabs_p = _rope_tables(jnp.arange(t, dtype=jnp.int32))
    pos_s = past + jnp.arange(ts, dtype=jnp.int32)
    tabs_s = tuple(jnp.tile(a, (nseq, 1)) for a in _rope_tables(pos_s))
    fg = final_norm_g.reshape(1, d)

    xp, xs = x_prompt, x_sample
    outs = [[] for _ in range(12)]
    for l in range(depth):
        w_pad = _pad_w_in(w_in[l])
        g = norm_g[l].reshape(1, d)
        wo_bf = w_out[l].astype(BF16)

        (qa, ka, va, ga, qb, kb, vb, gb, iq, ik, iw, qm, gm,
         ka_bf, va_bf, kb_bf, vb_bf, ik_bf, kmean) = _project(
            xp.reshape(b * t, d), g, w_pad, tabs_p, t // MOBA_BLOCK, MOBA_BLOCK)
        r3 = lambda a: a.reshape(b, t, a.shape[-1])
        mk, mv = _mem_kv(mem_prompt, mem_norm_g[l].reshape(1, d), w_mem_kv[l].astype(BF16))
        nbp = t // MOBA_BLOCK
        kmean_pad = jnp.pad(kmean.reshape(b, nbp, D_A), ((0, 0), (0, LANES - nbp), (0, 0)))
        oa = _moba_prompt(r3(qa), r3(ka_bf), r3(va_bf), kmean_pad)
        ob = _dsa_prompt(r3(qb), r3(iq).astype(BF16), r3(iw), r3(ik_bf), r3(kb_bf), r3(vb_bf))
        xp = _merge(xp, oa, r3(ga), ob, r3(gb), r3(qm), r3(gm), mk, mv, wo_bf, fg, MOBA_BLOCK, l == depth - 1)
        for lst, a, shp in ((outs[0], ka, (b, t, H_A, HEAD_DIM)), (outs[1], va, (b, t, H_A, HEAD_DIM)),
                            (outs[2], kb, (b, t, H_B, HEAD_DIM)), (outs[3], vb, (b, t, H_B, HEAD_DIM)),
                            (outs[4], ik, (b, t, IDX_DIM)), (outs[5], mk, (b, N_MEM, H_M, HEAD_DIM)),
                            (outs[6], mv, (b, N_MEM, H_M, HEAD_DIM))):
            lst.append(a.reshape(shp))

        n_s = nseq * ts
        tm_s = n_s if n_s <= MOBA_BLOCK else MOBA_BLOCK
        (qa, ka, va, ga, qb, kb, vb, gb, iq, ik, iw, qm, gm,
         _, _, _, _, _, _) = _project(xs.reshape(n_s, d), g, w_pad, tabs_s, n_s // tm_s, tm_s)
        s3 = lambda a: a.reshape(nseq, ts, a.shape[-1])
        padn = lambda a: jnp.pad(s3(a), ((0, 0), (0, LANES - ts), (0, 0)))
        oa = _moba_decode(page_table, s3(qa), cache_k_a[l].reshape(n_pool, page_size, D_A),
                          cache_v_a[l].reshape(n_pool, page_size, D_A), padn(ka), padn(va), page_size)
        iq_rows = s3(iq).reshape(nseq, ts, IDX_HEADS, IDX_DIM).transpose(0, 2, 1, 3).reshape(
            nseq, IDX_HEADS * ts, IDX_DIM)
        iw_rows = (s3(iw)[:, :, :IDX_HEADS] * (IDX_HEADS ** -0.5)).transpose(0, 2, 1).reshape(
            nseq, IDX_HEADS * ts, 1)
        iw_rows = jnp.broadcast_to(iw_rows, (nseq, IDX_HEADS * ts, LANES))
        bias = _idx_decode(page_table, iq_rows, iw_rows, cache_k_idx[l], padn(ik), page_size, ts)
        ob = _dsa_decode(page_table, s3(qb), bias, cache_k_b[l].reshape(n_pool, page_size, D_B),
                         cache_v_b[l].reshape(n_pool, page_size, D_B), padn(kb), padn(vb))
        mks = cache_mem_k[l].reshape(nseq, N_MEM, D_M)
        mvs = cache_mem_v[l].reshape(nseq, N_MEM, D_M)
        xs = _merge(xs, oa, s3(ga), ob, s3(gb), s3(qm), s3(gm), mks, mvs, wo_bf, fg, ts, l == depth - 1)
        for lst, a, shp in ((outs[7], ka, (nseq, ts, H_A, HEAD_DIM)), (outs[8], va, (nseq, ts, H_A, HEAD_DIM)),
                            (outs[9], kb, (nseq, ts, H_B, HEAD_DIM)), (outs[10], vb, (nseq, ts, H_B, HEAD_DIM)),
                            (outs[11], ik, (nseq, ts, IDX_DIM))):
            lst.append(a.reshape(shp))

    stk = [jnp.stack(o) for o in outs]
    return (xp, xs, stk[0], stk[1], stk[2], stk[3], stk[4], stk[5], stk[6],
            stk[7], stk[8], stk[9], stk[10], stk[11])
```

```python
import functools

import jax
import jax.numpy as jnp
from jax import lax
from jax.experimental import pallas as pl
from jax.experimental.pallas import tpu as pltpu

HEAD_DIM = 64
H_A = 6
H_B = 6
H_M = 4
D_A = H_A * HEAD_DIM
D_B = H_B * HEAD_DIM
D_M = H_M * HEAD_DIM
N_MEM = 256
MOBA_BLOCK = 256
MOBA_TOPK = 3
IDX_HEADS = 4
IDX_DIM = 64
IDX_TOPK = 256
ROPE_THETA = 500000.0
ROT_DIM = HEAD_DIM // 4
EPS = 1e-6

LANES = 128
TILE = 256
NEG = -1e30
INT_MIN = -(2 ** 31)
I16_MIN = -(2 ** 15)
VMEM_LIMIT = 60 * 1024 * 1024
IDX_PAGES_PER_STEP = 16

_G = D_A
OFF_QA, OFF_KA, OFF_VA, OFF_GA = 0, _G, 2 * _G, 3 * _G
OFF_QB, OFF_KB, OFF_VB, OFF_GB = 4 * _G, 5 * _G, 6 * _G, 7 * _G
OFF_IQ = 8 * _G
OFF_IK = OFF_IQ + 256
OFF_IW = OFF_IK + LANES
OFF_QM = OFF_IW + LANES
OFF_GM = OFF_QM + D_M
W_PAD = OFF_GM + D_M

F32 = jnp.float32
BF16 = jnp.bfloat16
NT = (((1,), (1,)), ((), ()))


def _cparams(*sem):
    return pltpu.CompilerParams(dimension_semantics=sem, vmem_limit_bytes=VMEM_LIMIT)


def _half_mask(hh):
    lane = lax.broadcasted_iota(jnp.int32, (1, LANES), 1)
    return (lane >= HEAD_DIM * hh) & (lane < HEAD_DIM * (hh + 1))


def _sortable(score):
    bits = lax.bitcast_convert_type(score, jnp.int32)
    key = bits ^ (lax.shift_right_arithmetic(bits, 31) & jnp.int32(0x7FFFFFFF))
    return jnp.where(score == 0.0, jnp.int32(0), key)


def _head_pair_queries(qc):
    scaled = qc * (HEAD_DIM ** -0.5)
    return jnp.concatenate([jnp.where(_half_mask(0), scaled, 0.0), jnp.where(_half_mask(1), scaled, 0.0)],
                           axis=0).astype(BF16)


def _pair_output(acc, l):
    t = acc.shape[1] // 2
    o = acc / l
    d = lax.broadcasted_iota(jnp.int32, (LANES, t), 0)
    return jnp.where(d < HEAD_DIM, o[:, :t], o[:, t:]).T


def _proj_kernel(x_ref, g_ref, w_ref, tc_ref, ts1_ref, ts2_ref,
                 qa_ref, ka_ref, va_ref, ga_ref, qb_ref, kb_ref, vb_ref, gb_ref,
                 iq_ref, ik_ref, iw_ref, qm_ref, gm_ref,
                 kabf_ref, vat_ref, kbbf_ref, vbt_ref, ikbf_ref, kmean_ref):
    x = x_ref[...]
    ms = jnp.mean(x * x, axis=-1, keepdims=True)
    h = (x * lax.rsqrt(ms + EPS) * g_ref[...]).astype(BF16)
    tc, ts1, ts2 = tc_ref[...], ts1_ref[...], ts2_ref[...]

    def mm(off, width):
        return jnp.dot(h, w_ref[:, off:off + width], preferred_element_type=F32)

    def rope(u):
        outs = []
        for c in range(u.shape[1] // LANES):
            uc = u[:, c * LANES:(c + 1) * LANES]
            outs.append(uc * tc + pltpu.roll(uc, LANES - ROT_DIM // 2, 1) * ts1
                        + pltpu.roll(uc, ROT_DIM // 2, 1) * ts2)
        return outs[0] if len(outs) == 1 else jnp.concatenate(outs, axis=1)

    qa_ref[...] = rope(mm(OFF_QA, D_A))
    ka = rope(mm(OFF_KA, D_A))
    ka_ref[...] = ka
    kabf_ref[...] = ka.astype(BF16)
    kmean_ref[0] = jnp.mean(ka, axis=0, keepdims=True)
    va = mm(OFF_VA, D_A)
    va_ref[...] = va
    vat_ref[0] = va.T.astype(BF16)
    ga_ref[...] = mm(OFF_GA, D_A)
    qb_ref[...] = rope(mm(OFF_QB, D_B))
    kb = rope(mm(OFF_KB, D_B))
    kb_ref[...] = kb
    kbbf_ref[...] = kb.astype(BF16)
    vb = mm(OFF_VB, D_B)
    vb_ref[...] = vb
    vbt_ref[0] = vb.T.astype(BF16)
    gb_ref[...] = mm(OFF_GB, D_B)
    iq_ref[...] = rope(mm(OFF_IQ, IDX_HEADS * IDX_DIM))
    ik2 = rope(mm(OFF_IK, LANES))
    ik_ref[...] = ik2[:, :IDX_DIM]
    ikbf_ref[...] = ik2.astype(BF16)
    iw_ref[...] = mm(OFF_IW, LANES)
    qm_ref[...] = mm(OFF_QM, D_M)
    gm_ref[...] = mm(OFF_GM, D_M)


def _project(x2d, g, w_pad, tabs, n_tab_tiles, tm):
    n, d = x2d.shape
    nt = n // tm
    tok = lambda w: pl.BlockSpec((tm, w), lambda i: (i, 0))
    tab = pl.BlockSpec((tm, LANES), lambda i: (i % n_tab_tiles, 0))
    vt = lambda w: pl.BlockSpec((1, w, tm), lambda i: (i, 0, 0))
    widths = [D_A] * 4 + [D_B] * 4 + [IDX_HEADS * IDX_DIM, IDX_DIM, LANES, D_M, D_M]
    out_shape = [jax.ShapeDtypeStruct((n, w), F32) for w in widths]
    out_specs = [tok(w) for w in widths]
    out_shape += [jax.ShapeDtypeStruct((n, D_A), BF16), jax.ShapeDtypeStruct((nt, D_A, tm), BF16),
                  jax.ShapeDtypeStruct((n, D_B), BF16), jax.ShapeDtypeStruct((nt, D_B, tm), BF16),
                  jax.ShapeDtypeStruct((n, LANES), BF16), jax.ShapeDtypeStruct((nt, 1, D_A), F32)]
    out_specs += [tok(D_A), vt(D_A), tok(D_B), vt(D_B), tok(LANES), pl.BlockSpec((1, 1, D_A), lambda i: (i, 0, 0))]
    return pl.pallas_call(
        _proj_kernel,
        grid=(nt,),
        in_specs=[tok(d),
                  pl.BlockSpec((1, d), lambda i: (0, 0)),
                  pl.BlockSpec((d, W_PAD), lambda i: (0, 0)),
                  tab, tab, tab],
        out_specs=out_specs,
        out_shape=out_shape,
        compiler_params=_cparams("parallel"),
        name="proj",
    )(x2d, g, w_pad, *tabs)


def _memkv_kernel(m_ref, g_ref, w_ref, mk_ref, mv_ref):
    x = m_ref[0]
    ms = jnp.mean(x * x, axis=-1, keepdims=True)
    h = (x * lax.rsqrt(ms + EPS) * g_ref[...]).astype(BF16)
    u = jnp.dot(h, w_ref[...], preferred_element_type=F32)
    mk_ref[0] = u[:, :D_M]
    mv_ref[0] = u[:, D_M:]


def _mem_kv(mem, g, w_bf):
    b, nm, d = mem.shape
    return pl.pallas_call(
        _memkv_kernel,
        grid=(b,),
        in_specs=[pl.BlockSpec((1, nm, d), lambda i: (i, 0, 0)),
                  pl.BlockSpec((1, d), lambda i: (0, 0)),
                  pl.BlockSpec((d, 2 * D_M), lambda i: (0, 0))],
        out_specs=[pl.BlockSpec((1, nm, D_M), lambda i: (i, 0, 0))] * 2,
        out_shape=[jax.ShapeDtypeStruct((b, nm, D_M), F32)] * 2,
        compiler_params=_cparams("parallel"),
        name="mem_kv",
    )(mem, g, w_bf)


def _flash_step(s, m, l, acc_ref, c, vt):
    m_new = jnp.maximum(m, jnp.max(s, axis=0, keepdims=True))
    alpha = jnp.exp(m - m_new)
    p = jnp.exp(s - m_new)
    l_new = alpha * l + jnp.sum(p, axis=0, keepdims=True)
    acc_ref[c] = alpha * acc_ref[c] + jnp.dot(vt, p.astype(BF16), preferred_element_type=F32)
    return m_new, l_new


def _moba_kernel(q_ref, k_ref, vt_ref, km_ref, o_ref, qaug_scr, acc_scr):
    qi = pl.program_id(1)
    n_pairs = D_A // LANES
    blk_id = lax.broadcasted_iota(jnp.int32, (TILE, LANES), 1)
    blk_f = blk_id.astype(F32)
    past = blk_id < qi
    for c in range(n_pairs):
        cs = slice(c * LANES, (c + 1) * LANES)
        qc = q_ref[0, :, cs]
        kmc = km_ref[0, :, cs]
        biases = []
        for hh in range(2):
            qm = jnp.where(_half_mask(hh), qc, 0.0)
            g = lax.dot_general(qm, kmc, NT, precision=lax.Precision.HIGHEST, preferred_element_type=F32)
            g = jnp.where(past, g, -jnp.inf)
            sel = jnp.zeros((TILE, LANES), jnp.bool_)
            for _ in range(MOBA_TOPK):
                m = jnp.max(g, axis=1, keepdims=True)
                first = jnp.min(jnp.where(g == m, blk_f, 1e9), axis=1, keepdims=True)
                pick = (blk_f == first) & past
                sel = sel | pick
                g = jnp.where(pick, -jnp.inf, g)
            biases.append(jnp.where(sel, 0.0, NEG).astype(BF16))
        qaug_scr[c] = jnp.concatenate([_head_pair_queries(qc), jnp.concatenate(biases, axis=0)], axis=1)

    own0 = pl.multiple_of(qi * TILE, TILE)
    ko = k_ref[0, pl.ds(own0, TILE), :]
    vto = vt_ref[0, qi]
    krow = lax.broadcasted_iota(jnp.int32, (TILE, 2 * TILE), 0)
    qcol = lax.broadcasted_iota(jnp.int32, (TILE, 2 * TILE), 1) % TILE
    ms, ls = [], []
    for c in range(n_pairs):
        cs = slice(c * LANES, (c + 1) * LANES)
        s = lax.dot_general(ko[:, cs], qaug_scr[c, :, :LANES], NT, preferred_element_type=F32)
        s = jnp.where(krow <= qcol, s, NEG)
        m = jnp.max(s, axis=0, keepdims=True)
        p = jnp.exp(s - m)
        ls.append(jnp.sum(p, axis=0, keepdims=True))
        ms.append(m)
        acc_scr[c] = jnp.dot(vto[cs, :], p.astype(BF16), preferred_element_type=F32)

    def body(j, carry):
        ms, ls = carry
        j0 = pl.multiple_of(j * TILE, TILE)
        kj = k_ref[0, pl.ds(j0, TILE), :]
        vtj = vt_ref[0, j]
        onehot = jnp.where(blk_id == j, 1.0, 0.0).astype(BF16)
        new_m, new_l = [], []
        for c in range(n_pairs):
            cs = slice(c * LANES, (c + 1) * LANES)
            k_aug = jnp.concatenate([kj[:, cs], onehot], axis=1)
            s = lax.dot_general(k_aug, qaug_scr[c], NT, preferred_element_type=F32)
            m, l = _flash_step(s, ms[c], ls[c], acc_scr, c, vtj[cs, :])
            new_m.append(m)
            new_l.append(l)
        return tuple(new_m), tuple(new_l)

    ms, ls = lax.fori_loop(0, qi, body, (tuple(ms), tuple(ls)))
    o_ref[0] = jnp.concatenate([_pair_output(acc_scr[c], ls[c]) for c in range(n_pairs)], axis=1)


def _moba_prompt(qa, ka_bf, va_t, kmean_pad):
    b, t, _ = qa.shape
    nt = t // TILE
    return pl.pallas_call(
        _moba_kernel,
        grid=(b, nt),
        in_specs=[pl.BlockSpec((1, TILE, D_A), lambda i, j: (i, j, 0)),
                  pl.BlockSpec((1, t, D_A), lambda i, j: (i, 0, 0)),
                  pl.BlockSpec((1, nt, D_A, TILE), lambda i, j: (i, 0, 0, 0)),
                  pl.BlockSpec((1, LANES, D_A), lambda i, j: (i, 0, 0))],
        out_specs=pl.BlockSpec((1, TILE, D_A), lambda i, j: (i, j, 0)),
        out_shape=jax.ShapeDtypeStruct((b, t, D_A), F32),
        scratch_shapes=[pltpu.VMEM((D_A // LANES, 2 * TILE, 2 * LANES), BF16),
                        pltpu.VMEM((D_A // LANES, LANES, 2 * TILE), F32)],
        compiler_params=_cparams("parallel", "arbitrary"),
        name="moba_prompt",
    )(qa, ka_bf, va_t, kmean_pad)


def _dsa_kernel(q_ref, iq_ref, iw_ref, ik_ref, k_ref, vt_ref, o_ref,
                key_scr, hi_scr, lo_scr, cst_scr, acc_scr, *, k_sel, idx_bits):
    qi = pl.program_id(1)
    n_tiles = qi + 1
    n_pairs = D_B // LANES
    kf = float(k_sel)
    krow = lax.broadcasted_iota(jnp.int32, (TILE, TILE), 0)
    q_pos = lax.broadcasted_iota(jnp.int32, (TILE, TILE), 1) + qi * TILE
    key_pos = lambda j: krow + j * TILE

    iq = iq_ref[0]
    iw_t = (iw_ref[0] * (IDX_HEADS ** -0.5)).T
    iqm = [jnp.where(_half_mask(n % 2), iq[:, (n // 2) * LANES:(n // 2 + 1) * LANES], 0.0).astype(BF16)
           for n in range(IDX_HEADS)]

    def score_body(j, carry):
        j0 = pl.multiple_of(j * TILE, TILE)
        ikj = ik_ref[0, pl.ds(j0, TILE), :]
        score = jnp.zeros((TILE, TILE), F32)
        for n in range(IDX_HEADS):
            rel = lax.dot_general(ikj, iqm[n], NT, preferred_element_type=F32) * (IDX_DIM ** -0.5)
            score = score + iw_t[n:n + 1, :] * jnp.maximum(rel, 0.0)
        key = jnp.where(key_pos(j) > q_pos, jnp.int32(INT_MIN), _sortable(score))
        key_scr[j] = key
        hi_scr[j] = lax.shift_right_arithmetic(key, 16).astype(jnp.int16)
        lo_scr[j] = ((key & 0xFFFF) + I16_MIN).astype(jnp.int16)
        return carry

    lax.fori_loop(0, n_tiles, score_body, 0)

    def count16(plane, pred):
        def body(j, acc):
            w = jnp.where(pred(plane[j]), jnp.int16(1), jnp.int16(0))
            for r in range(TILE // 64):
                acc = acc + w[r * 64:(r + 1) * 64]
            return acc
        acc = lax.fori_loop(0, n_tiles, body, jnp.zeros((64, TILE), jnp.int16))
        return jnp.sum(acc.astype(jnp.int32).astype(F32), axis=0, keepdims=True)

    def count32(pred):
        def body(j, acc):
            w = jnp.where(pred(key_scr[j], j), 1.0, 0.0)
            for r in range(TILE // 64):
                acc = acc + w[r * 64:(r + 1) * 64]
            return acc
        acc = lax.fori_loop(0, n_tiles, body, jnp.zeros((64, TILE), F32))
        return jnp.sum(acc, axis=0, keepdims=True)

    def search16(plane, target):
        zero = jnp.zeros((1, TILE), jnp.int32)
        c0 = count16(plane, lambda x: x >= zero.astype(jnp.int16))
        thr = jnp.where(c0 >= target, zero, jnp.int32(I16_MIN))

        def bit(i, thr):
            cand = thr | lax.shift_left(jnp.int32(1), 14 - i)
            c = count16(plane, lambda x: x >= cand.astype(jnp.int16))
            return jnp.where(c >= target, cand, thr)

        return lax.fori_loop(0, 15, bit, thr)

    thr_hi = search16(hi_scr, kf)
    thr_hi16 = thr_hi.astype(jnp.int16)
    above = count16(hi_scr, lambda x: x > thr_hi16)

    def mask_lo(j, carry):
        lo_scr[j] = jnp.where(hi_scr[j] == thr_hi16, lo_scr[j], jnp.int16(I16_MIN))
        return carry

    lax.fori_loop(0, n_tiles, mask_lo, 0)
    thr_lo = search16(lo_scr, kf - above)
    thr = thr_hi * 65536 + (thr_lo - I16_MIN)

    cnt_ge = count32(lambda k, j: k >= thr)
    need = kf - count32(lambda k, j: k > thr)
    cst_scr[...] = jnp.full(cst_scr.shape, jnp.int32(2 ** 30))

    @pl.when(jnp.max(cnt_ge) > kf)
    def _():
        def cbit(i, cpos):
            cand = cpos | lax.shift_left(jnp.int32(1), idx_bits - 1 - i)
            cnt = count32(lambda k, j: (k == thr) & (key_pos(j) < cand))
            return jnp.where(cnt < need, cand, cpos)
        cpos = lax.fori_loop(0, idx_bits, cbit, jnp.zeros((1, TILE), jnp.int32))
        cst_scr[...] = jnp.broadcast_to(cpos, cst_scr.shape)

    cstar = cst_scr[:1, :]

    def to_bias(j, carry):
        k = key_scr[j]
        kp = key_pos(j)
        keep = ((k > thr) | ((k == thr) & (kp <= cstar))) & (kp <= q_pos)
        key_scr[j] = lax.bitcast_convert_type(jnp.where(keep, 0.0, NEG).astype(F32), jnp.int32)
        return carry

    lax.fori_loop(0, n_tiles, to_bias, 0)

    q_pairs = [_head_pair_queries(q_ref[0, :, c * LANES:(c + 1) * LANES]) for c in range(n_pairs)]
    for c in range(n_pairs):
        acc_scr[c] = jnp.zeros((LANES, 2 * TILE), F32)

    def body(j, carry):
        ms, ls = carry
        j0 = pl.multiple_of(j * TILE, TILE)
        kj = k_ref[0, pl.ds(j0, TILE), :]
        vtj = vt_ref[0, j]
        bias = lax.bitcast_convert_type(key_scr[j], F32)
        bias2 = jnp.concatenate([bias, bias], axis=1)
        new_m, new_l = [], []
        logits = [lax.dot_general(kj[:, c * LANES:(c + 1) * LANES], q_pairs[c], NT, preferred_element_type=F32)
                  for c in range(n_pairs)]
        for c in range(n_pairs):
            m, l = _flash_step(logits[c] + bias2, ms[c], ls[c], acc_scr, c, vtj[c * LANES:(c + 1) * LANES, :])
            new_m.append(m)
            new_l.append(l)
        return tuple(new_m), tuple(new_l)

    init = (tuple(jnp.full((1, 2 * TILE), NEG, F32) for _ in range(n_pairs)),
            tuple(jnp.zeros((1, 2 * TILE), F32) for _ in range(n_pairs)))
    ms, ls = lax.fori_loop(0, n_tiles, body, init)
    o_ref[0] = jnp.concatenate([_pair_output(acc_scr[c], ls[c]) for c in range(n_pairs)], axis=1)


def _dsa_prompt(qb, iq, iw, ik_bf, kb_bf, vb_t):
    b, t, _ = qb.shape
    nt = t // TILE
    k_sel = min(IDX_TOPK, t // 4)
    kern = functools.partial(_dsa_kernel, k_sel=k_sel, idx_bits=max(1, (t - 1).bit_length()))
    res = lambda w: pl.BlockSpec((1, t, w), lambda i, j: (i, 0, 0))
    til = lambda w: pl.BlockSpec((1, TILE, w), lambda i, j: (i, j, 0))
    return pl.pallas_call(
        kern,
        grid=(b, nt),
        in_specs=[til(D_B), til(IDX_HEADS * IDX_DIM), til(LANES), res(LANES), res(D_B),
                  pl.BlockSpec((1, nt, D_B, TILE), lambda i, j: (i, 0, 0, 0))],
        out_specs=til(D_B),
        out_shape=jax.ShapeDtypeStruct((b, t, D_B), F32),
        scratch_shapes=[pltpu.VMEM((nt, TILE, TILE), jnp.int32),
                        pltpu.VMEM((nt, TILE, TILE), jnp.int16),
                        pltpu.VMEM((nt, TILE, TILE), jnp.int16),
                        pltpu.VMEM((8, TILE), jnp.int32),
                        pltpu.VMEM((D_B // LANES, LANES, 2 * TILE), F32)],
        compiler_params=_cparams("parallel", "arbitrary"),
        name="dsa_prompt",
    )(qb, iq, iw, ik_bf, kb_bf, vb_t)


def _merge_kernel(x_ref, oa_ref, ga_ref, ob_ref, gb_ref, qm_ref, gm_ref, mk_ref, mv_ref, wo_ref, fg_ref, y_ref,
                  *, final):
    lane = lax.broadcasted_iota(jnp.int32, (1, LANES), 1)
    mk = mk_ref[0].astype(BF16)
    mv = mv_ref[0].astype(BF16)
    oms = []
    for c in range(D_M // LANES):
        cs = slice(c * LANES, (c + 1) * LANES)
        qc = qm_ref[0, :, cs]
        heads = []
        for hh in range(2):
            q_bf = (jnp.where(_half_mask(hh), qc, 0.0) * (HEAD_DIM ** -0.5)).astype(BF16)
            s = lax.dot_general(q_bf, mk[:, cs], NT, preferred_element_type=F32)
            m = jnp.max(s, axis=1, keepdims=True)
            p = jnp.exp(s - m)
            l = jnp.sum(p, axis=1, keepdims=True)
            heads.append(jnp.dot(p.astype(BF16), mv[:, cs], preferred_element_type=F32) / l)
        oms.append(jnp.where(lane < HEAD_DIM, heads[0], heads[1]))
    om = jnp.concatenate(oms, axis=1)
    silu = lambda g: g / (1.0 + jnp.exp(-g))
    ymix = jnp.concatenate([oa_ref[0] * silu(ga_ref[0]), ob_ref[0] * silu(gb_ref[0]), om * silu(gm_ref[0])],
                           axis=1).astype(BF16)
    z = x_ref[0] + jnp.dot(ymix, wo_ref[...], preferred_element_type=F32)
    if final:
        ms = jnp.mean(z * z, axis=-1, keepdims=True)
        z = z * lax.rsqrt(ms + EPS) * fg_ref[...]
    y_ref[0] = z


def _merge(x, oa, ga, ob, gb, qm, gm, mk, mv, wo_bf, fg, tm, final):
    b, t, d = x.shape
    til = lambda w: pl.BlockSpec((1, tm, w), lambda i, j: (i, j, 0))
    memspec = pl.BlockSpec((1, N_MEM, D_M), lambda i, j: (i, 0, 0))
    return pl.pallas_call(
        functools.partial(_merge_kernel, final=final),
        grid=(b, t // tm),
        in_specs=[til(d), til(D_A), til(D_A), til(D_B), til(D_B), til(D_M), til(D_M), memspec, memspec,
                  pl.BlockSpec((d, d), lambda i, j: (0, 0)),
                  pl.BlockSpec((1, d), lambda i, j: (0, 0))],
        out_specs=til(d),
        out_shape=jax.ShapeDtypeStruct((b, t, d), F32),
        compiler_params=_cparams("parallel", "parallel"),
        name="merge",
    )(x, oa, ga, ob, gb, qm, gm, mk, mv, wo_bf, fg)


def _expand_heads(q, n_heads):
    t, w = q.shape
    qt = jnp.concatenate([q] * n_heads, axis=0)
    r = lax.broadcasted_iota(jnp.int32, (n_heads * t, w), 0)
    l = lax.broadcasted_iota(jnp.int32, (n_heads * t, w), 1)
    keep = (l >= (r // t) * HEAD_DIM) & (l < (r // t + 1) * HEAD_DIM)
    return jnp.where(keep, qt, 0.0)


def _collapse_heads(o, n_heads, t):
    w = o.shape[1]
    l = lax.broadcasted_iota(jnp.int32, (t, w), 1)
    out = jnp.zeros((t, w), F32)
    for h in range(n_heads):
        keep = (l >= h * HEAD_DIM) & (l < (h + 1) * HEAD_DIM)
        out = out + jnp.where(keep, o[h * t:(h + 1) * t, :], 0.0)
    return out


def _page_rows(refs):
    pages = [r[0, 0].reshape(r.shape[2], r.shape[3] * r.shape[4]) for r in refs]
    return pages[0] if len(pages) == 1 else jnp.concatenate(pages, axis=0)


def _paged_specs(layer, block_tail, per_step):
    zeros = (0,) * len(block_tail)
    return [pl.BlockSpec((1, 1) + block_tail, lambda s, j, pt, i=i: (layer, pt[s, j * per_step + i]) + zeros)
            for i in range(per_step)]


def _moba_dec_kernel(pt_ref, q_ref, k0_ref, k1_ref, v0_ref, v1_ref, kn_ref, vn_ref, o_ref,
                     o_scr, m_scr, l_scr, g_scr):
    j = pl.program_id(1)
    nb = pl.num_programs(1)
    t = q_ref.shape[1]
    rows = H_A * t
    qx = _expand_heads(q_ref[0], H_A)
    qx_bf = (qx * (HEAD_DIM ** -0.5)).astype(BF16)

    k2 = _page_rows([k0_ref, k1_ref])
    v2 = _page_rows([v0_ref, v1_ref])
    kmean = jnp.mean(k2, axis=0, keepdims=True)
    gate = jnp.sum(qx * kmean, axis=1, keepdims=True)
    s = lax.dot_general(qx_bf, k2.astype(BF16), NT, preferred_element_type=F32)
    m = jnp.max(s, axis=1, keepdims=True)
    p = jnp.exp(s - m)
    l = jnp.sum(p, axis=1, keepdims=True)
    o_scr[j] = jnp.dot(p.astype(BF16), v2.astype(BF16), preferred_element_type=F32)
    m_scr[j] = jnp.broadcast_to(m, (rows, LANES))
    l_scr[j] = jnp.broadcast_to(l, (rows, LANES))
    g_scr[j] = jnp.broadcast_to(gate, (rows, LANES))

    @pl.when(j == nb - 1)
    def _():
        npad = kn_ref.shape[1]
        r = lax.broadcasted_iota(jnp.int32, (rows, npad), 0)
        cidx = lax.broadcasted_iota(jnp.int32, (rows, npad), 1)
        ok = cidx <= (r % t)
        s = lax.dot_general(qx_bf, kn_ref[0].astype(BF16), NT, preferred_element_type=F32)
        s = jnp.where(ok, s, NEG)
        m_own = jnp.max(s, axis=1, keepdims=True)
        p = jnp.exp(s - m_own)
        l_own = jnp.sum(p, axis=1, keepdims=True)
        o_own = jnp.dot(p.astype(BF16), vn_ref[0].astype(BF16), preferred_element_type=F32)

        n_blocks = g_scr.shape[0]
        g = g_scr[...]
        bidx = lax.broadcasted_iota(jnp.int32, g.shape, 0).astype(F32)
        sel = jnp.zeros(g.shape, jnp.bool_)
        for _ in range(min(MOBA_TOPK, n_blocks)):
            mx = jnp.max(g, axis=0, keepdims=True)
            first = jnp.min(jnp.where(g == mx, bidx, 1e9), axis=0, keepdims=True)
            pick = bidx == first
            sel = sel | pick
            g = jnp.where(pick, -jnp.inf, g)
        ms = jnp.where(sel, m_scr[...], NEG)
        big = jnp.maximum(jnp.max(ms, axis=0), jnp.broadcast_to(m_own, (rows, LANES)))
        w = jnp.where(sel, jnp.exp(ms - big[None]), 0.0)
        w_own = jnp.exp(jnp.broadcast_to(m_own, (rows, LANES)) - big)
        l_tot = jnp.sum(w * l_scr[...], axis=0) + w_own * jnp.broadcast_to(l_own, (rows, LANES))
        w3 = jnp.concatenate([w] * (D_A // LANES), axis=2)
        o_tot = jnp.sum(w3 * o_scr[...], axis=0) + jnp.concatenate([w_own] * (D_A // LANES), axis=1) * o_own
        o_n = o_tot / jnp.concatenate([l_tot] * (D_A // LANES), axis=1)
        o_ref[0] = _collapse_heads(o_n, H_A, t)


def _moba_decode(layer, page_table, qa, cache_k, cache_v, kn_pad, vn_pad):
    nseq, t, _ = qa.shape
    n_pages = page_table.shape[1]
    page_size = cache_k.shape[2]
    ppb = MOBA_BLOCK // page_size
    nb = n_pages // ppb
    rows = H_A * t
    seq = lambda w, r: pl.BlockSpec((1, r, w), lambda s, j, pt: (s, 0, 0))
    kspecs = _paged_specs(layer, cache_k.shape[2:], ppb)
    grid_spec = pltpu.PrefetchScalarGridSpec(
        num_scalar_prefetch=1,
        grid=(nseq, nb),
        in_specs=[seq(D_A, t)] + kspecs + kspecs + [seq(D_A, kn_pad.shape[1])] * 2,
        out_specs=seq(D_A, t),
        scratch_shapes=[pltpu.VMEM((nb, rows, D_A), F32)] + [pltpu.VMEM((nb, rows, LANES), F32)] * 3,
    )
    return pl.pallas_call(
        _moba_dec_kernel,
        grid_spec=grid_spec,
        out_shape=jax.ShapeDtypeStruct((nseq, t, D_A), F32),
        compiler_params=_cparams("parallel", "arbitrary"),
        name="moba_decode",
    )(page_table, qa, cache_k, cache_k, cache_v, cache_v, kn_pad, vn_pad)


def _idx_dec_kernel(pt_ref, iq_ref, iw_ref, *rest, per_step, k_sel, idx_bits, past, t):
    ik_refs = rest[:per_step]
    ikn_ref, bias_ref, key_scr = rest[per_step:]
    j = pl.program_id(1)
    nsteps = pl.num_programs(1)
    width = key_scr.shape[2]
    iq_bf = iq_ref[0].astype(BF16)
    iw = iw_ref[0][:, :1]

    def scores(ik):
        rel = lax.dot_general(iq_bf, ik.astype(BF16), NT, preferred_element_type=F32) * (IDX_DIM ** -0.5)
        wrel = iw * jnp.maximum(rel, 0.0)
        sc = wrel[0:t]
        for n in range(1, IDX_HEADS):
            sc = sc + wrel[n * t:(n + 1) * t]
        return sc

    key_scr[j] = _sortable(scores(jnp.concatenate([r[0, 0] for r in ik_refs], axis=0)))

    @pl.when(j == nsteps - 1)
    def _():
        kf = float(k_sel)
        npad = ikn_ref.shape[1]
        r_new = lax.broadcasted_iota(jnp.int32, (t, npad), 0)
        c_new = lax.broadcasted_iota(jnp.int32, (t, npad), 1)
        key_new = jnp.where(c_new <= r_new, _sortable(scores(ikn_ref[0])), jnp.int32(INT_MIN))
        tail = jnp.full((t, width - npad), jnp.int32(INT_MIN))
        key_scr[nsteps] = jnp.concatenate([key_new, tail], axis=1)

        keys = key_scr[...]
        pos = (lax.broadcasted_iota(jnp.int32, keys.shape, 0) * width
               + lax.broadcasted_iota(jnp.int32, keys.shape, 2))

        def count(pred):
            w = jnp.sum(jnp.where(pred, 1.0, 0.0), axis=0)
            acc = w[:, :LANES]
            for s in range(1, width // LANES):
                acc = acc + w[:, s * LANES:(s + 1) * LANES]
            return jnp.sum(acc, axis=1, keepdims=True)[None]

        zero = jnp.zeros((1, t, 1), jnp.int32)
        thr = jnp.where(count(keys >= zero) >= kf, zero, jnp.int32(INT_MIN))

        def bit(i, thr):
            cand = thr | lax.shift_left(jnp.int32(1), 30 - i)
            return jnp.where(count(keys >= cand) >= kf, cand, thr)

        thr = lax.fori_loop(0, 31, bit, thr)
        need = kf - count(keys > thr)

        def cbit(i, cpos):
            cand = cpos | lax.shift_left(jnp.int32(1), idx_bits - 1 - i)
            return jnp.where(count((keys == thr) & (pos < cand)) < need, cand, cpos)

        cstar = lax.fori_loop(0, idx_bits, cbit, zero)
        keep = (keys > thr) | ((keys == thr) & (pos <= cstar))
        bias_ref[0] = jnp.where(keep, 0.0, NEG).astype(F32)


def _idx_decode(layer, page_table, iq_rows, iw_rows, cache_ik, ikn_pad, t):
    nseq = iq_rows.shape[0]
    n_pages = page_table.shape[1]
    page_size = cache_ik.shape[2]
    per_step = IDX_PAGES_PER_STEP
    while n_pages % per_step:
        per_step //= 2
    nsteps = n_pages // per_step
    width = per_step * page_size
    past = n_pages * page_size
    k_sel = min(IDX_TOPK, (past + t) // 4)
    idx_bits = max(1, ((nsteps + 1) * width - 1).bit_length())
    seq = lambda r, w: pl.BlockSpec((1, r, w), lambda s, j, pt: (s, 0, 0))
    grid_spec = pltpu.PrefetchScalarGridSpec(
        num_scalar_prefetch=1,
        grid=(nseq, nsteps),
        in_specs=[seq(IDX_HEADS * t, IDX_DIM), seq(IDX_HEADS * t, LANES)]
        + _paged_specs(layer, cache_ik.shape[2:], per_step) + [seq(ikn_pad.shape[1], IDX_DIM)],
        out_specs=pl.BlockSpec((1, nsteps + 1, t, width), lambda s, j, pt: (s, 0, 0, 0)),
        scratch_shapes=[pltpu.VMEM((nsteps + 1, t, width), jnp.int32)],
    )
    kern = functools.partial(_idx_dec_kernel, per_step=per_step, k_sel=k_sel, idx_bits=idx_bits, past=past, t=t)
    return pl.pallas_call(
        kern,
        grid_spec=grid_spec,
        out_shape=jax.ShapeDtypeStruct((nseq, nsteps + 1, t, width), F32),
        compiler_params=_cparams("parallel", "arbitrary"),
        name="idx_decode",
    )(page_table, iq_rows, iw_rows, *([cache_ik] * per_step), ikn_pad)


def _dsa_dec_kernel(pt_ref, q_ref, b_ref, bn_ref, k0_ref, k1_ref, v0_ref, v1_ref, kn_ref, vn_ref,
                    o_ref, m_scr, l_scr, acc_scr):
    j = pl.program_id(1)
    nb = pl.num_programs(1)
    t = q_ref.shape[1]
    qx_bf = (_expand_heads(q_ref[0], H_B) * (HEAD_DIM ** -0.5)).astype(BF16)

    @pl.when(j == 0)
    def _():
        m_scr[...] = jnp.full(m_scr.shape, NEG, F32)
        l_scr[...] = jnp.zeros(l_scr.shape, F32)
        acc_scr[...] = jnp.zeros(acc_scr.shape, F32)

    def step(k, v, bias):
        s = lax.dot_general(qx_bf, k.astype(BF16), NT, preferred_element_type=F32)
        s = s + jnp.concatenate([bias] * H_B, axis=0)
        m_old = m_scr[:, :1]
        m_new = jnp.maximum(m_old, jnp.max(s, axis=1, keepdims=True))
        alpha = jnp.exp(m_old - m_new)
        p = jnp.exp(s - m_new)
        l_scr[...] = jnp.broadcast_to(alpha * l_scr[:, :1] + jnp.sum(p, axis=1, keepdims=True), l_scr.shape)
        acc_scr[...] = alpha * acc_scr[...] + jnp.dot(p.astype(BF16), v.astype(BF16), preferred_element_type=F32)
        m_scr[...] = jnp.broadcast_to(m_new, m_scr.shape)

    step(_page_rows([k0_ref, k1_ref]), _page_rows([v0_ref, v1_ref]), b_ref[0, 0])

    @pl.when(j == nb - 1)
    def _():
        step(kn_ref[0], vn_ref[0], bn_ref[0, 0])
        o_n = acc_scr[...] / l_scr[:, :1]
        o_ref[0] = _collapse_heads(o_n, H_B, t)


def _dsa_decode(layer, page_table, qb, bias, cache_k, cache_v, kn_pad, vn_pad):
    nseq, t, _ = qb.shape
    n_pages = page_table.shape[1]
    page_size = cache_k.shape[2]
    ppb = 2
    nb = n_pages // ppb
    rows = H_B * t
    width = bias.shape[3]
    chunk = ppb * page_size
    per_w = width // chunk
    npad = kn_pad.shape[1]
    seq = lambda w, r: pl.BlockSpec((1, r, w), lambda s, j, pt: (s, 0, 0))
    bspec = pl.BlockSpec((1, 1, t, chunk), lambda s, j, pt: (s, j // per_w, 0, j % per_w))
    bnew = pl.BlockSpec((1, 1, t, npad), lambda s, j, pt: (s, bias.shape[1] - 1, 0, 0))
    kspecs = _paged_specs(layer, cache_k.shape[2:], ppb)
    grid_spec = pltpu.PrefetchScalarGridSpec(
        num_scalar_prefetch=1,
        grid=(nseq, nb),
        in_specs=[seq(D_B, t), bspec, bnew] + kspecs + kspecs + [seq(D_B, npad)] * 2,
        out_specs=seq(D_B, t),
        scratch_shapes=[pltpu.VMEM((rows, LANES), F32), pltpu.VMEM((rows, LANES), F32),
                        pltpu.VMEM((rows, D_B), F32)],
    )
    return pl.pallas_call(
        _dsa_dec_kernel,
        grid_spec=grid_spec,
        out_shape=jax.ShapeDtypeStruct((nseq, t, D_B), F32),
        compiler_params=_cparams("parallel", "arbitrary"),
        name="dsa_decode",
    )(page_table, qb, bias, bias, cache_k, cache_k, cache_v, cache_v, kn_pad, vn_pad)


def _rope_tables(pos):
    half = ROT_DIM // 2
    inv = jnp.power(ROPE_THETA, -jnp.arange(half, dtype=F32) / half)
    ang = pos.astype(F32)[:, None] * inv[None, :]
    cos, sin = jnp.cos(ang), jnp.sin(ang)
    n = pos.shape[0]
    ones = jnp.ones((n, HEAD_DIM - ROT_DIM), F32)
    zeros = jnp.zeros((n, HEAD_DIM - ROT_DIM), F32)
    zh = jnp.zeros((n, half), F32)
    tc = jnp.concatenate([cos, cos, ones], axis=1)
    ts1 = jnp.concatenate([-sin, zh, zeros], axis=1)
    ts2 = jnp.concatenate([zh, sin, zeros], axis=1)
    dup = lambda a: jnp.concatenate([a, a], axis=1)
    return dup(tc), dup(ts1), dup(ts2)


def _pad_w_in(w):
    d = w.shape[0]
    main = w[:, :8 * _G]
    o = 8 * _G
    iq = w[:, o:o + 256]
    ik = w[:, o + 256:o + 320]
    iw = w[:, o + 320:o + 324]
    qm = w[:, o + 324:o + 324 + D_M]
    gm = w[:, o + 324 + D_M:o + 324 + 2 * D_M]
    iw_pad = jnp.concatenate([iw, jnp.zeros((d, LANES - IDX_HEADS), w.dtype)], axis=1)
    return jnp.concatenate([main, iq, ik, ik, iw_pad, qm, gm], axis=1).astype(BF16)


def kernel(x_prompt, x_sample, mem_prompt, cache_k_a, cache_v_a, cache_k_b, cache_v_b, cache_k_idx,
           cache_mem_k, cache_mem_v, page_table, norm_g, w_in, mem_norm_g, w_mem_kv, w_out, final_norm_g):
    depth = w_in.shape[0]
    b, t, d = x_prompt.shape
    nseq, ts, _ = x_sample.shape
    n_pages = page_table.shape[1]
    page_size = cache_k_a.shape[2]
    past = n_pages * page_size
    assert t % TILE == 0 and t // TILE <= LANES and page_size == LANES
    assert past % MOBA_BLOCK == 0 and ts <= LANES and ts % 8 == 0

    tabs_p = _rope_tables(jnp.arange(t, dtype=jnp.int32))
    pos_s = past + jnp.arange(ts, dtype=jnp.int32)
    tabs_s = tuple(jnp.tile(a, (nseq, 1)) for a in _rope_tables(pos_s))
    fg = final_norm_g.reshape(1, d)
    nbp = t // TILE

    xp, xs = x_prompt, x_sample
    outs = [[] for _ in range(12)]
    for l in range(depth):
        w_pad = _pad_w_in(w_in[l])
        g = norm_g[l].reshape(1, d)
        wo_bf = w_out[l].astype(BF16)
        last = l == depth - 1

        (qa, ka, va, ga, qb, kb, vb, gb, iq, ik, iw, qm, gm,
         ka_bf, va_t, kb_bf, vb_t, ik_bf, kmean) = _project(xp.reshape(b * t, d), g, w_pad, tabs_p, nbp, TILE)
        r3 = lambda a: a.reshape(b, t, a.shape[-1])
        r4 = lambda a: a.reshape(b, nbp, a.shape[-2], TILE)
        mk, mv = _mem_kv(mem_prompt, mem_norm_g[l].reshape(1, d), w_mem_kv[l].astype(BF16))
        kmean_pad = jnp.pad(kmean.reshape(b, nbp, D_A), ((0, 0), (0, LANES - nbp), (0, 0)))
        oa = _moba_prompt(r3(qa), r3(ka_bf), r4(va_t), kmean_pad)
        ob = _dsa_prompt(r3(qb), r3(iq).astype(BF16), r3(iw), r3(ik_bf), r3(kb_bf), r4(vb_t))
        xp = _merge(xp, oa, r3(ga), ob, r3(gb), r3(qm), r3(gm), mk, mv, wo_bf, fg, TILE, last)
        for lst, a, shp in ((outs[0], ka, (b, t, H_A, HEAD_DIM)), (outs[1], va, (b, t, H_A, HEAD_DIM)),
                            (outs[2], kb, (b, t, H_B, HEAD_DIM)), (outs[3], vb, (b, t, H_B, HEAD_DIM)),
                            (outs[4], ik, (b, t, IDX_DIM)), (outs[5], mk, (b, N_MEM, H_M, HEAD_DIM)),
                            (outs[6], mv, (b, N_MEM, H_M, HEAD_DIM))):
            lst.append(a.reshape(shp))

        n_s = nseq * ts
        tm_s = n_s if n_s <= TILE else TILE
        (qa, ka, va, ga, qb, kb, vb, gb, iq, ik, iw, qm, gm,
         _, _, _, _, _, _) = _project(xs.reshape(n_s, d), g, w_pad, tabs_s, n_s // tm_s, tm_s)
        s3 = lambda a: a.reshape(nseq, ts, a.shape[-1])
        padn = lambda a: jnp.pad(s3(a), ((0, 0), (0, LANES - ts), (0, 0)))
        oa = _moba_decode(l, page_table, s3(qa), cache_k_a, cache_v_a, padn(ka), padn(va))
        iq_rows = s3(iq).reshape(nseq, ts, IDX_HEADS, IDX_DIM).transpose(0, 2, 1, 3).reshape(
            nseq, IDX_HEADS * ts, IDX_DIM)
        iw_rows = (s3(iw)[:, :, :IDX_HEADS] * (IDX_HEADS ** -0.5)).transpose(0, 2, 1).reshape(
            nseq, IDX_HEADS * ts, 1)
        iw_rows = jnp.broadcast_to(iw_rows, (nseq, IDX_HEADS * ts, LANES))
        bias = _idx_decode(l, page_table, iq_rows, iw_rows, cache_k_idx, padn(ik), ts)
        ob = _dsa_decode(l, page_table, s3(qb), bias, cache_k_b, cache_v_b, padn(kb), padn(vb))
        mks = cache_mem_k[l].reshape(nseq, N_MEM, D_M)
        mvs = cache_mem_v[l].reshape(nseq, N_MEM, D_M)
        xs = _merge(xs, oa, s3(ga), ob, s3(gb), s3(qm), s3(gm), mks, mvs, wo_bf, fg, ts, last)
        for lst, a, shp in ((outs[7], ka, (nseq, ts, H_A, HEAD_DIM)), (outs[8], va, (nseq, ts, H_A, HEAD_DIM)),
                            (outs[9], kb, (nseq, ts, H_B, HEAD_DIM)), (outs[10], vb, (nseq, ts, H_B, HEAD_DIM)),
                            (outs[11], ik, (nseq, ts, IDX_DIM))):
            lst.append(a.reshape(shp))

    stk = [jnp.stack(o) for o in outs]
    return (xp, xs, stk[0], stk[1], stk[2], stk[3], stk[4], stk[5], stk[6],
            stk[7], stk[8], stk[9], stk[10], stk[11])
```

```python
import functools

import jax
import jax.numpy as jnp
from jax import lax
from jax.experimental import pallas as pl
from jax.experimental.pallas import tpu as pltpu

HEAD_DIM = 64
H_A = 6
H_B = 6
H_M = 4
D_A = H_A * HEAD_DIM
D_B = H_B * HEAD_DIM
D_M = H_M * HEAD_DIM
N_MEM = 256
MOBA_BLOCK = 256
MOBA_TOPK = 3
IDX_HEADS = 4
IDX_DIM = 64
IDX_TOPK = 256
ROPE_THETA = 500000.0
ROT_DIM = HEAD_DIM // 4
EPS = 1e-6

LANES = 128
TILE = 256
NEG = -1e30
INT_MIN = -(2 ** 31)
I16_MIN = -(2 ** 15)
VMEM_LIMIT = 60 * 1024 * 1024
IDX_PAGES_PER_STEP = 16

_G = D_A
OFF_QA, OFF_KA, OFF_VA, OFF_GA = 0, _G, 2 * _G, 3 * _G
OFF_QB, OFF_KB, OFF_VB, OFF_GB = 4 * _G, 5 * _G, 6 * _G, 7 * _G
OFF_IQ = 8 * _G
OFF_IK = OFF_IQ + 256
OFF_IW = OFF_IK + LANES
OFF_QM = OFF_IW + LANES
OFF_GM = OFF_QM + D_M
W_PAD = OFF_GM + D_M

F32 = jnp.float32
BF16 = jnp.bfloat16
NT = (((1,), (1,)), ((), ()))


def _cparams(*sem):
    return pltpu.CompilerParams(dimension_semantics=sem, vmem_limit_bytes=VMEM_LIMIT)


def _half_mask(hh):
    lane = lax.broadcasted_iota(jnp.int32, (1, LANES), 1)
    return (lane >= HEAD_DIM * hh) & (lane < HEAD_DIM * (hh + 1))


def _sortable(score):
    bits = lax.bitcast_convert_type(score, jnp.int32)
    key = bits ^ (lax.shift_right_arithmetic(bits, 31) & jnp.int32(0x7FFFFFFF))
    return jnp.where(score == 0.0, jnp.int32(0), key)


def _head_pair_queries(qc):
    scaled = qc * (HEAD_DIM ** -0.5)
    return jnp.concatenate([jnp.where(_half_mask(0), scaled, 0.0), jnp.where(_half_mask(1), scaled, 0.0)],
                           axis=0).astype(BF16)


def _pair_output(acc, l):
    t = acc.shape[1] // 2
    o = acc / l
    d = lax.broadcasted_iota(jnp.int32, (LANES, t), 0)
    return jnp.where(d < HEAD_DIM, o[:, :t], o[:, t:]).T


def _proj_kernel(x_ref, g_ref, w_ref, tc_ref, ts1_ref, ts2_ref,
                 qa_ref, ka_ref, va_ref, ga_ref, qb_ref, kb_ref, vb_ref, gb_ref,
                 iq_ref, ik_ref, iw_ref, qm_ref, gm_ref,
                 kabf_ref, vat_ref, kbbf_ref, vbt_ref, ikbf_ref, kmean_ref):
    x = x_ref[...]
    ms = jnp.mean(x * x, axis=-1, keepdims=True)
    h = (x * lax.rsqrt(ms + EPS) * g_ref[...]).astype(BF16)
    tc, ts1, ts2 = tc_ref[...], ts1_ref[...], ts2_ref[...]

    def mm(off, width):
        return jnp.dot(h, w_ref[:, off:off + width], preferred_element_type=F32)

    def rope(u):
        outs = []
        for c in range(u.shape[1] // LANES):
            uc = u[:, c * LANES:(c + 1) * LANES]
            outs.append(uc * tc + pltpu.roll(uc, LANES - ROT_DIM // 2, 1) * ts1
                        + pltpu.roll(uc, ROT_DIM // 2, 1) * ts2)
        return outs[0] if len(outs) == 1 else jnp.concatenate(outs, axis=1)

    qa_ref[...] = rope(mm(OFF_QA, D_A))
    ka = rope(mm(OFF_KA, D_A))
    ka_ref[...] = ka
    kabf_ref[...] = ka.astype(BF16)
    kmean_ref[0] = jnp.mean(ka, axis=0, keepdims=True)
    va = mm(OFF_VA, D_A)
    va_ref[...] = va
    vat_ref[0] = va.T.astype(BF16)
    ga_ref[...] = mm(OFF_GA, D_A)
    qb_ref[...] = rope(mm(OFF_QB, D_B))
    kb = rope(mm(OFF_KB, D_B))
    kb_ref[...] = kb
    kbbf_ref[...] = kb.astype(BF16)
    vb = mm(OFF_VB, D_B)
    vb_ref[...] = vb
    vbt_ref[0] = vb.T.astype(BF16)
    gb_ref[...] = mm(OFF_GB, D_B)
    iq_ref[...] = rope(mm(OFF_IQ, IDX_HEADS * IDX_DIM))
    ik2 = rope(mm(OFF_IK, LANES))
    ik_ref[...] = ik2[:, :IDX_DIM]
    ikbf_ref[...] = ik2.astype(BF16)
    iw_ref[...] = mm(OFF_IW, LANES)
    qm_ref[...] = mm(OFF_QM, D_M)
    gm_ref[...] = mm(OFF_GM, D_M)


def _project(x2d, g, w_pad, tabs, n_tab_tiles, tm):
    n, d = x2d.shape
    nt = n // tm
    tok = lambda w: pl.BlockSpec((tm, w), lambda i: (i, 0))
    tab = pl.BlockSpec((tm, LANES), lambda i: (i % n_tab_tiles, 0))
    vt = lambda w: pl.BlockSpec((1, w, tm), lambda i: (i, 0, 0))
    widths = [D_A] * 4 + [D_B] * 4 + [IDX_HEADS * IDX_DIM, IDX_DIM, LANES, D_M, D_M]
    out_shape = [jax.ShapeDtypeStruct((n, w), F32) for w in widths]
    out_specs = [tok(w) for w in widths]
    out_shape += [jax.ShapeDtypeStruct((n, D_A), BF16), jax.ShapeDtypeStruct((nt, D_A, tm), BF16),
                  jax.ShapeDtypeStruct((n, D_B), BF16), jax.ShapeDtypeStruct((nt, D_B, tm), BF16),
                  jax.ShapeDtypeStruct((n, LANES), BF16), jax.ShapeDtypeStruct((nt, 1, D_A), F32)]
    out_specs += [tok(D_A), vt(D_A), tok(D_B), vt(D_B), tok(LANES), pl.BlockSpec((1, 1, D_A), lambda i: (i, 0, 0))]
    return pl.pallas_call(
        _proj_kernel,
        grid=(nt,),
        in_specs=[tok(d),
                  pl.BlockSpec((1, d), lambda i: (0, 0)),
                  pl.BlockSpec((d, W_PAD), lambda i: (0, 0)),
                  tab, tab, tab],
        out_specs=out_specs,
        out_shape=out_shape,
        compiler_params=_cparams("parallel"),
        name="proj",
    )(x2d, g, w_pad, *tabs)


def _memkv_kernel(m_ref, g_ref, w_ref, mk_ref, mv_ref):
    x = m_ref[0]
    ms = jnp.mean(x * x, axis=-1, keepdims=True)
    h = (x * lax.rsqrt(ms + EPS) * g_ref[...]).astype(BF16)
    u = jnp.dot(h, w_ref[...], preferred_element_type=F32)
    mk_ref[0] = u[:, :D_M]
    mv_ref[0] = u[:, D_M:]


def _mem_kv(mem, g, w_bf):
    b, nm, d = mem.shape
    return pl.pallas_call(
        _memkv_kernel,
        grid=(b,),
        in_specs=[pl.BlockSpec((1, nm, d), lambda i: (i, 0, 0)),
                  pl.BlockSpec((1, d), lambda i: (0, 0)),
                  pl.BlockSpec((d, 2 * D_M), lambda i: (0, 0))],
        out_specs=[pl.BlockSpec((1, nm, D_M), lambda i: (i, 0, 0))] * 2,
        out_shape=[jax.ShapeDtypeStruct((b, nm, D_M), F32)] * 2,
        compiler_params=_cparams("parallel"),
        name="mem_kv",
    )(mem, g, w_bf)


def _flash_step(s, m, l, acc_ref, c, vt):
    m_new = jnp.maximum(m, jnp.max(s, axis=0, keepdims=True))
    alpha = jnp.exp(m - m_new)
    p = jnp.exp(s - m_new)
    l_new = alpha * l + jnp.sum(p, axis=0, keepdims=True)
    acc_ref[c] = alpha * acc_ref[c] + jnp.dot(vt, p.astype(BF16), preferred_element_type=F32)
    return m_new, l_new


def _moba_kernel(q_ref, k_ref, vt_ref, km_ref, o_ref, qaug_scr, acc_scr, s_scr):
    qi = pl.program_id(1)
    n_pairs = D_A // LANES
    blk_id = lax.broadcasted_iota(jnp.int32, (TILE, LANES), 1)
    blk_f = blk_id.astype(F32)
    past = blk_id < qi
    for c in range(n_pairs):
        cs = slice(c * LANES, (c + 1) * LANES)
        qc = q_ref[0, :, cs]
        kmc = km_ref[0, :, cs]
        biases = []
        for hh in range(2):
            qm = jnp.where(_half_mask(hh), qc, 0.0)
            g = lax.dot_general(qm, kmc, NT, precision=lax.Precision.HIGHEST, preferred_element_type=F32)
            g = jnp.where(past, g, -jnp.inf)
            sel = jnp.zeros((TILE, LANES), jnp.bool_)
            for _ in range(MOBA_TOPK):
                m = jnp.max(g, axis=1, keepdims=True)
                first = jnp.min(jnp.where(g == m, blk_f, 1e9), axis=1, keepdims=True)
                pick = (blk_f == first) & past
                sel = sel | pick
                g = jnp.where(pick, -jnp.inf, g)
            biases.append(jnp.where(sel, 0.0, NEG).astype(BF16))
        qaug_scr[c] = jnp.concatenate([_head_pair_queries(qc), jnp.concatenate(biases, axis=0)], axis=1)

    own0 = pl.multiple_of(qi * TILE, TILE)
    ko = k_ref[0, pl.ds(own0, TILE), :]
    vto = vt_ref[0, qi]
    krow = lax.broadcasted_iota(jnp.int32, (TILE, 2 * TILE), 0)
    qcol = lax.broadcasted_iota(jnp.int32, (TILE, 2 * TILE), 1) % TILE
    ms, ls = [], []
    for c in range(n_pairs):
        cs = slice(c * LANES, (c + 1) * LANES)
        s = lax.dot_general(ko[:, cs], qaug_scr[c, :, :LANES], NT, preferred_element_type=F32)
        s = jnp.where(krow <= qcol, s, NEG)
        m = jnp.max(s, axis=0, keepdims=True)
        p = jnp.exp(s - m)
        ls.append(jnp.sum(p, axis=0, keepdims=True))
        ms.append(m)
        acc_scr[c] = jnp.dot(vto[cs, :], p.astype(BF16), preferred_element_type=F32)

    def logits(j, c):
        j0 = pl.multiple_of(j * TILE, TILE)
        onehot = jnp.where(blk_id == j, 1.0, 0.0).astype(BF16)
        k_aug = jnp.concatenate([k_ref[0, pl.ds(j0, TILE), c * LANES:(c + 1) * LANES], onehot], axis=1)
        return lax.dot_general(k_aug, qaug_scr[c], NT, preferred_element_type=F32)

    s_scr[0] = logits(0, 0)

    def body(j, carry):
        ms, ls = carry
        vtj = vt_ref[0, j]
        new_m, new_l = [], []
        for c in range(n_pairs):
            if c + 1 < n_pairs:
                s_scr[c + 1] = logits(j, c + 1)
            else:
                s_scr[0] = logits(jnp.minimum(j + 1, qi - 1), 0)
            m, l = _flash_step(s_scr[c], ms[c], ls[c], acc_scr, c, vtj[c * LANES:(c + 1) * LANES, :])
            new_m.append(m)
            new_l.append(l)
        return tuple(new_m), tuple(new_l)

    ms, ls = lax.fori_loop(0, qi, body, (tuple(ms), tuple(ls)))
    o_ref[0] = jnp.concatenate([_pair_output(acc_scr[c], ls[c]) for c in range(n_pairs)], axis=1)


def _moba_prompt(qa, ka_bf, va_t, kmean_pad):
    b, t, _ = qa.shape
    nt = t // TILE
    return pl.pallas_call(
        _moba_kernel,
        grid=(b, nt),
        in_specs=[pl.BlockSpec((1, TILE, D_A), lambda i, j: (i, j, 0)),
                  pl.BlockSpec((1, t, D_A), lambda i, j: (i, 0, 0)),
                  pl.BlockSpec((1, nt, D_A, TILE), lambda i, j: (i, 0, 0, 0)),
                  pl.BlockSpec((1, LANES, D_A), lambda i, j: (i, 0, 0))],
        out_specs=pl.BlockSpec((1, TILE, D_A), lambda i, j: (i, j, 0)),
        out_shape=jax.ShapeDtypeStruct((b, t, D_A), F32),
        scratch_shapes=[pltpu.VMEM((D_A // LANES, 2 * TILE, 2 * LANES), BF16),
                        pltpu.VMEM((D_A // LANES, LANES, 2 * TILE), F32),
                        pltpu.VMEM((D_A // LANES, TILE, 2 * TILE), F32)],
        compiler_params=_cparams("parallel", "arbitrary"),
        name="moba_prompt",
    )(qa, ka_bf, va_t, kmean_pad)


def _dsa_kernel(q_ref, iq_ref, iw_ref, ik_ref, k_ref, vt_ref, o_ref,
                key_scr, hi_scr, lo_scr, cst_scr, acc_scr, s_scr, *, k_sel, idx_bits):
    qi = pl.program_id(1)
    n_tiles = qi + 1
    n_pairs = D_B // LANES
    kf = float(k_sel)
    krow = lax.broadcasted_iota(jnp.int32, (TILE, TILE), 0)
    q_pos = lax.broadcasted_iota(jnp.int32, (TILE, TILE), 1) + qi * TILE
    key_pos = lambda j: krow + j * TILE

    iq = iq_ref[0]
    iw_t = (iw_ref[0] * (IDX_HEADS ** -0.5)).T
    iqm = [jnp.where(_half_mask(n % 2), iq[:, (n // 2) * LANES:(n // 2 + 1) * LANES], 0.0).astype(BF16)
           for n in range(IDX_HEADS)]

    def score_body(j, carry):
        j0 = pl.multiple_of(j * TILE, TILE)
        ikj = ik_ref[0, pl.ds(j0, TILE), :]
        score = jnp.zeros((TILE, TILE), F32)
        for n in range(IDX_HEADS):
            rel = lax.dot_general(ikj, iqm[n], NT, preferred_element_type=F32) * (IDX_DIM ** -0.5)
            score = score + iw_t[n:n + 1, :] * jnp.maximum(rel, 0.0)
        key = jnp.where(key_pos(j) > q_pos, jnp.int32(INT_MIN), _sortable(score))
        key_scr[j] = key
        hi_scr[j] = lax.shift_right_arithmetic(key, 16).astype(jnp.int16)
        lo_scr[j] = ((key & 0xFFFF) + I16_MIN).astype(jnp.int16)
        return carry

    lax.fori_loop(0, n_tiles, score_body, 0)

    def count16(plane, pred):
        def body(j, acc):
            w = jnp.where(pred(plane[j]), jnp.int16(1), jnp.int16(0))
            for r in range(TILE // 64):
                acc = acc + w[r * 64:(r + 1) * 64]
            return acc
        acc = lax.fori_loop(0, n_tiles, body, jnp.zeros((64, TILE), jnp.int16))
        return jnp.sum(acc.astype(jnp.int32).astype(F32), axis=0, keepdims=True)

    def count32(pred):
        def body(j, acc):
            w = jnp.where(pred(key_scr[j], j), 1.0, 0.0)
            for r in range(TILE // 64):
                acc = acc + w[r * 64:(r + 1) * 64]
            return acc
        acc = lax.fori_loop(0, n_tiles, body, jnp.zeros((64, TILE), F32))
        return jnp.sum(acc, axis=0, keepdims=True)

    def search16(plane, target):
        zero = jnp.zeros((1, TILE), jnp.int32)
        c0 = count16(plane, lambda x: x >= zero.astype(jnp.int16))
        thr = jnp.where(c0 >= target, zero, jnp.int32(I16_MIN))

        def bit(i, thr):
            cand = thr | lax.shift_left(jnp.int32(1), 14 - i)
            c = count16(plane, lambda x: x >= cand.astype(jnp.int16))
            return jnp.where(c >= target, cand, thr)

        return lax.fori_loop(0, 15, bit, thr)

    thr_hi = search16(hi_scr, kf)
    thr_hi16 = thr_hi.astype(jnp.int16)
    above = count16(hi_scr, lambda x: x > thr_hi16)

    def mask_lo(j, carry):
        lo_scr[j] = jnp.where(hi_scr[j] == thr_hi16, lo_scr[j], jnp.int16(I16_MIN))
        return carry

    lax.fori_loop(0, n_tiles, mask_lo, 0)
    thr_lo = search16(lo_scr, kf - above)
    thr = thr_hi * 65536 + (thr_lo - I16_MIN)

    cnt_ge = count32(lambda k, j: k >= thr)
    need = kf - count32(lambda k, j: k > thr)
    cst_scr[...] = jnp.full(cst_scr.shape, jnp.int32(2 ** 30))

    @pl.when(jnp.max(cnt_ge) > kf)
    def _():
        def cbit(i, cpos):
            cand = cpos | lax.shift_left(jnp.int32(1), idx_bits - 1 - i)
            cnt = count32(lambda k, j: (k == thr) & (key_pos(j) < cand))
            return jnp.where(cnt < need, cand, cpos)
        cpos = lax.fori_loop(0, idx_bits, cbit, jnp.zeros((1, TILE), jnp.int32))
        cst_scr[...] = jnp.broadcast_to(cpos, cst_scr.shape)

    cstar = cst_scr[:1, :]

    def to_bias(j, carry):
        k = key_scr[j]
        kp = key_pos(j)
        keep = ((k > thr) | ((k == thr) & (kp <= cstar))) & (kp <= q_pos)
        key_scr[j] = lax.bitcast_convert_type(jnp.where(keep, 0.0, NEG).astype(F32), jnp.int32)
        return carry

    lax.fori_loop(0, n_tiles, to_bias, 0)

    q_pairs = [_head_pair_queries(q_ref[0, :, c * LANES:(c + 1) * LANES]) for c in range(n_pairs)]
    for c in range(n_pairs):
        acc_scr[c] = jnp.zeros((LANES, 2 * TILE), F32)

    def logits(j, c):
        j0 = pl.multiple_of(j * TILE, TILE)
        return lax.dot_general(k_ref[0, pl.ds(j0, TILE), c * LANES:(c + 1) * LANES], q_pairs[c], NT,
                               preferred_element_type=F32)

    s_scr[0] = logits(0, 0)

    def body(j, carry):
        ms, ls = carry
        vtj = vt_ref[0, j]
        bias = lax.bitcast_convert_type(key_scr[j], F32)
        bias2 = jnp.concatenate([bias, bias], axis=1)
        new_m, new_l = [], []
        for c in range(n_pairs):
            if c + 1 < n_pairs:
                s_scr[c + 1] = logits(j, c + 1)
            else:
                s_scr[0] = logits(jnp.minimum(j + 1, n_tiles - 1), 0)
            m, l = _flash_step(s_scr[c] + bias2, ms[c], ls[c], acc_scr, c, vtj[c * LANES:(c + 1) * LANES, :])
            new_m.append(m)
            new_l.append(l)
        return tuple(new_m), tuple(new_l)

    init = (tuple(jnp.full((1, 2 * TILE), NEG, F32) for _ in range(n_pairs)),
            tuple(jnp.zeros((1, 2 * TILE), F32) for _ in range(n_pairs)))
    ms, ls = lax.fori_loop(0, n_tiles, body, init)
    o_ref[0] = jnp.concatenate([_pair_output(acc_scr[c], ls[c]) for c in range(n_pairs)], axis=1)


def _dsa_prompt(qb, iq, iw, ik_bf, kb_bf, vb_t):
    b, t, _ = qb.shape
    nt = t // TILE
    k_sel = min(IDX_TOPK, t // 4)
    kern = functools.partial(_dsa_kernel, k_sel=k_sel, idx_bits=max(1, (t - 1).bit_length()))
    res = lambda w: pl.BlockSpec((1, t, w), lambda i, j: (i, 0, 0))
    til = lambda w: pl.BlockSpec((1, TILE, w), lambda i, j: (i, j, 0))
    return pl.pallas_call(
        kern,
        grid=(b, nt),
        in_specs=[til(D_B), til(IDX_HEADS * IDX_DIM), til(LANES), res(LANES), res(D_B),
                  pl.BlockSpec((1, nt, D_B, TILE), lambda i, j: (i, 0, 0, 0))],
        out_specs=til(D_B),
        out_shape=jax.ShapeDtypeStruct((b, t, D_B), F32),
        scratch_shapes=[pltpu.VMEM((nt, TILE, TILE), jnp.int32),
                        pltpu.VMEM((nt, TILE, TILE), jnp.int16),
                        pltpu.VMEM((nt, TILE, TILE), jnp.int16),
                        pltpu.VMEM((8, TILE), jnp.int32),
                        pltpu.VMEM((D_B // LANES, LANES, 2 * TILE), F32),
                        pltpu.VMEM((D_B // LANES, TILE, 2 * TILE), F32)],
        compiler_params=_cparams("parallel", "arbitrary"),
        name="dsa_prompt",
    )(qb, iq, iw, ik_bf, kb_bf, vb_t)


def _merge_kernel(x_ref, oa_ref, ga_ref, ob_ref, gb_ref, qm_ref, gm_ref, mk_ref, mv_ref, wo_ref, fg_ref, y_ref,
                  *, final):
    lane = lax.broadcasted_iota(jnp.int32, (1, LANES), 1)
    mk = mk_ref[0].astype(BF16)
    mv = mv_ref[0].astype(BF16)
    oms = []
    for c in range(D_M // LANES):
        cs = slice(c * LANES, (c + 1) * LANES)
        qc = qm_ref[0, :, cs]
        heads = []
        for hh in range(2):
            q_bf = (jnp.where(_half_mask(hh), qc, 0.0) * (HEAD_DIM ** -0.5)).astype(BF16)
            s = lax.dot_general(q_bf, mk[:, cs], NT, preferred_element_type=F32)
            m = jnp.max(s, axis=1, keepdims=True)
            p = jnp.exp(s - m)
            l = jnp.sum(p, axis=1, keepdims=True)
            heads.append(jnp.dot(p.astype(BF16), mv[:, cs], preferred_element_type=F32) / l)
        oms.append(jnp.where(lane < HEAD_DIM, heads[0], heads[1]))
    om = jnp.concatenate(oms, axis=1)
    silu = lambda g: g / (1.0 + jnp.exp(-g))
    ymix = jnp.concatenate([oa_ref[0] * silu(ga_ref[0]), ob_ref[0] * silu(gb_ref[0]), om * silu(gm_ref[0])],
                           axis=1).astype(BF16)
    z = x_ref[0] + jnp.dot(ymix, wo_ref[...], preferred_element_type=F32)
    if final:
        ms = jnp.mean(z * z, axis=-1, keepdims=True)
        z = z * lax.rsqrt(ms + EPS) * fg_ref[...]
    y_ref[0] = z


def _merge(x, oa, ga, ob, gb, qm, gm, mk, mv, wo_bf, fg, tm, final):
    b, t, d = x.shape
    til = lambda w: pl.BlockSpec((1, tm, w), lambda i, j: (i, j, 0))
    memspec = pl.BlockSpec((1, N_MEM, D_M), lambda i, j: (i, 0, 0))
    return pl.pallas_call(
        functools.partial(_merge_kernel, final=final),
        grid=(b, t // tm),
        in_specs=[til(d), til(D_A), til(D_A), til(D_B), til(D_B), til(D_M), til(D_M), memspec, memspec,
                  pl.BlockSpec((d, d), lambda i, j: (0, 0)),
                  pl.BlockSpec((1, d), lambda i, j: (0, 0))],
        out_specs=til(d),
        out_shape=jax.ShapeDtypeStruct((b, t, d), F32),
        compiler_params=_cparams("parallel", "parallel"),
        name="merge",
    )(x, oa, ga, ob, gb, qm, gm, mk, mv, wo_bf, fg)


def _expand_heads(q, n_heads):
    t, w = q.shape
    qt = jnp.concatenate([q] * n_heads, axis=0)
    r = lax.broadcasted_iota(jnp.int32, (n_heads * t, w), 0)
    l = lax.broadcasted_iota(jnp.int32, (n_heads * t, w), 1)
    keep = (l >= (r // t) * HEAD_DIM) & (l < (r // t + 1) * HEAD_DIM)
    return jnp.where(keep, qt, 0.0)


def _collapse_heads(o, n_heads, t):
    w = o.shape[1]
    l = lax.broadcasted_iota(jnp.int32, (t, w), 1)
    out = jnp.zeros((t, w), F32)
    for h in range(n_heads):
        keep = (l >= h * HEAD_DIM) & (l < (h + 1) * HEAD_DIM)
        out = out + jnp.where(keep, o[h * t:(h + 1) * t, :], 0.0)
    return out


def _pages_t(refs):
    pages = [r[0, 0].reshape(r.shape[2] * r.shape[3], r.shape[4]) for r in refs]
    return pages[0] if len(pages) == 1 else jnp.concatenate(pages, axis=1)


def _cache_view(cache):
    return jnp.transpose(cache, (0, 1, 3, 4, 2))


def _paged_specs(layer, block_tail, per_step):
    zeros = (0,) * len(block_tail)
    return [pl.BlockSpec((1, 1) + block_tail, lambda s, j, pt, i=i: (layer, pt[s, j * per_step + i]) + zeros)
            for i in range(per_step)]


def _moba_dec_kernel(pt_ref, q_ref, *rest, ppb, bps):
    n_pg = ppb * bps
    k_refs, v_refs = rest[:n_pg], rest[n_pg:2 * n_pg]
    kn_ref, vn_ref, o_ref, o_scr, m_scr, l_scr, g_scr = rest[2 * n_pg:]
    j = pl.program_id(1)
    nsteps = pl.num_programs(1)
    t = q_ref.shape[1]
    rows = H_A * t
    qx = _expand_heads(q_ref[0], H_A)
    qx_bf = (qx * (HEAD_DIM ** -0.5)).astype(BF16)

    for b in range(bps):
        kt = _pages_t(k_refs[b * ppb:(b + 1) * ppb])
        vt = _pages_t(v_refs[b * ppb:(b + 1) * ppb])
        kmean = jnp.broadcast_to(jnp.mean(kt, axis=1, keepdims=True), (D_A, LANES))
        gate = jnp.dot(qx, kmean, precision=lax.Precision.HIGHEST, preferred_element_type=F32)
        s = jnp.dot(qx_bf, kt.astype(BF16), preferred_element_type=F32)
        m = jnp.max(s, axis=1, keepdims=True)
        p = jnp.exp(s - m)
        l = jnp.sum(p, axis=1, keepdims=True)
        blk = j * bps + b
        o_scr[blk] = lax.dot_general(p.astype(BF16), vt.astype(BF16), NT, preferred_element_type=F32)
        m_scr[blk] = jnp.broadcast_to(m, (rows, LANES))
        l_scr[blk] = jnp.broadcast_to(l, (rows, LANES))
        g_scr[blk] = gate

    @pl.when(j == nsteps - 1)
    def _():
        npad = kn_ref.shape[1]
        r = lax.broadcasted_iota(jnp.int32, (rows, npad), 0)
        cidx = lax.broadcasted_iota(jnp.int32, (rows, npad), 1)
        ok = cidx <= (r % t)
        s = lax.dot_general(qx_bf, kn_ref[0].astype(BF16), NT, preferred_element_type=F32)
        s = jnp.where(ok, s, NEG)
        m_own = jnp.max(s, axis=1, keepdims=True)
        p = jnp.exp(s - m_own)
        l_own = jnp.sum(p, axis=1, keepdims=True)
        o_own = jnp.dot(p.astype(BF16), vn_ref[0].astype(BF16), preferred_element_type=F32)

        n_blocks = g_scr.shape[0]
        g = g_scr[...]
        bidx = lax.broadcasted_iota(jnp.int32, g.shape, 0).astype(F32)
        sel = jnp.zeros(g.shape, jnp.bool_)
        for _ in range(min(MOBA_TOPK, n_blocks)):
            mx = jnp.max(g, axis=0, keepdims=True)
            first = jnp.min(jnp.where(g == mx, bidx, 1e9), axis=0, keepdims=True)
            pick = bidx == first
            sel = sel | pick
            g = jnp.where(pick, -jnp.inf, g)
        ms = jnp.where(sel, m_scr[...], NEG)
        big = jnp.maximum(jnp.max(ms, axis=0), jnp.broadcast_to(m_own, (rows, LANES)))
        w = jnp.where(sel, jnp.exp(ms - big[None]), 0.0)
        w_own = jnp.exp(jnp.broadcast_to(m_own, (rows, LANES)) - big)
        l_tot = jnp.sum(w * l_scr[...], axis=0) + w_own * jnp.broadcast_to(l_own, (rows, LANES))
        w3 = jnp.concatenate([w] * (D_A // LANES), axis=2)
        o_tot = jnp.sum(w3 * o_scr[...], axis=0) + jnp.concatenate([w_own] * (D_A // LANES), axis=1) * o_own
        o_n = o_tot / jnp.concatenate([l_tot] * (D_A // LANES), axis=1)
        o_ref[0] = _collapse_heads(o_n, H_A, t)


def _moba_decode(layer, page_table, qa, cache_kt, cache_vt, kn_pad, vn_pad):
    nseq, t, _ = qa.shape
    n_pages = page_table.shape[1]
    page_size = cache_kt.shape[4]
    ppb = MOBA_BLOCK // page_size
    nb = n_pages // ppb
    bps = 2 if nb % 2 == 0 else 1
    rows = H_A * t
    seq = lambda w, r: pl.BlockSpec((1, r, w), lambda s, j, pt: (s, 0, 0))
    kspecs = _paged_specs(layer, cache_kt.shape[2:], ppb * bps)
    grid_spec = pltpu.PrefetchScalarGridSpec(
        num_scalar_prefetch=1,
        grid=(nseq, nb // bps),
        in_specs=[seq(D_A, t)] + kspecs + kspecs + [seq(D_A, kn_pad.shape[1])] * 2,
        out_specs=seq(D_A, t),
        scratch_shapes=[pltpu.VMEM((nb, rows, D_A), F32)] + [pltpu.VMEM((nb, rows, LANES), F32)] * 3,
    )
    return pl.pallas_call(
        functools.partial(_moba_dec_kernel, ppb=ppb, bps=bps),
        grid_spec=grid_spec,
        out_shape=jax.ShapeDtypeStruct((nseq, t, D_A), F32),
        compiler_params=_cparams("parallel", "arbitrary"),
        name="moba_decode",
    )(page_table, qa, *([cache_kt] * (ppb * bps)), *([cache_vt] * (ppb * bps)), kn_pad, vn_pad)


def _idx_dec_kernel(pt_ref, iq_ref, iw_ref, *rest, per_step, k_sel, idx_bits, past, t):
    ik_refs = rest[:per_step]
    ikn_ref, bias_ref, key_scr = rest[per_step:]
    j = pl.program_id(1)
    nsteps = pl.num_programs(1)
    width = key_scr.shape[2]
    iq_bf = iq_ref[0].astype(BF16)
    iw = iw_ref[0][:, :1]

    def scores(rel):
        wrel = iw * jnp.maximum(rel * (IDX_DIM ** -0.5), 0.0)
        sc = wrel[0:t]
        for n in range(1, IDX_HEADS):
            sc = sc + wrel[n * t:(n + 1) * t]
        return sc

    ik_t = jnp.concatenate([r[0, 0] for r in ik_refs], axis=1)
    key_scr[j] = _sortable(scores(jnp.dot(iq_bf, ik_t.astype(BF16), preferred_element_type=F32)))

    @pl.when(j == nsteps - 1)
    def _():
        kf = float(k_sel)
        npad = ikn_ref.shape[1]
        r_new = lax.broadcasted_iota(jnp.int32, (t, npad), 0)
        c_new = lax.broadcasted_iota(jnp.int32, (t, npad), 1)
        rel_new = lax.dot_general(iq_bf, ikn_ref[0].astype(BF16), NT, preferred_element_type=F32)
        key_new = jnp.where(c_new <= r_new, _sortable(scores(rel_new)), jnp.int32(INT_MIN))
        tail = jnp.full((t, width - npad), jnp.int32(INT_MIN))
        key_scr[nsteps] = jnp.concatenate([key_new, tail], axis=1)

        keys = key_scr[...]
        pos = (lax.broadcasted_iota(jnp.int32, keys.shape, 0) * width
               + lax.broadcasted_iota(jnp.int32, keys.shape, 2))

        def count(pred):
            w = jnp.sum(jnp.where(pred, 1.0, 0.0), axis=0)
            acc = w[:, :LANES]
            for s in range(1, width // LANES):
                acc = acc + w[:, s * LANES:(s + 1) * LANES]
            return jnp.sum(acc, axis=1, keepdims=True)[None]

        zero = jnp.zeros((1, t, 1), jnp.int32)
        thr = jnp.where(count(keys >= zero) >= kf, zero, jnp.int32(INT_MIN))

        def bit(i, thr):
            cand = thr | lax.shift_left(jnp.int32(1), 30 - i)
            return jnp.where(count(keys >= cand) >= kf, cand, thr)

        thr = lax.fori_loop(0, 31, bit, thr)
        need = kf - count(keys > thr)

        def cbit(i, cpos):
            cand = cpos | lax.shift_left(jnp.int32(1), idx_bits - 1 - i)
            return jnp.where(count((keys == thr) & (pos < cand)) < need, cand, cpos)

        cstar = lax.fori_loop(0, idx_bits, cbit, zero)
        keep = (keys > thr) | ((keys == thr) & (pos <= cstar))
        bias_ref[0] = jnp.where(keep, 0.0, NEG).astype(F32)


def _idx_decode(layer, page_table, iq_rows, iw_rows, cache_ik, ikn_pad, t):
    nseq = iq_rows.shape[0]
    n_pages = page_table.shape[1]
    page_size = cache_ik.shape[3]
    per_step = IDX_PAGES_PER_STEP
    while n_pages % per_step:
        per_step //= 2
    nsteps = n_pages // per_step
    width = per_step * page_size
    past = n_pages * page_size
    k_sel = min(IDX_TOPK, (past + t) // 4)
    idx_bits = max(1, ((nsteps + 1) * width - 1).bit_length())
    seq = lambda r, w: pl.BlockSpec((1, r, w), lambda s, j, pt: (s, 0, 0))
    grid_spec = pltpu.PrefetchScalarGridSpec(
        num_scalar_prefetch=1,
        grid=(nseq, nsteps),
        in_specs=[seq(IDX_HEADS * t, IDX_DIM), seq(IDX_HEADS * t, LANES)]
        + _paged_specs(layer, cache_ik.shape[2:], per_step) + [seq(ikn_pad.shape[1], IDX_DIM)],
        out_specs=pl.BlockSpec((1, nsteps + 1, t, width), lambda s, j, pt: (s, 0, 0, 0)),
        scratch_shapes=[pltpu.VMEM((nsteps + 1, t, width), jnp.int32)],
    )
    kern = functools.partial(_idx_dec_kernel, per_step=per_step, k_sel=k_sel, idx_bits=idx_bits, past=past, t=t)
    return pl.pallas_call(
        kern,
        grid_spec=grid_spec,
        out_shape=jax.ShapeDtypeStruct((nseq, nsteps + 1, t, width), F32),
        compiler_params=_cparams("parallel", "arbitrary"),
        name="idx_decode",
    )(page_table, iq_rows, iw_rows, *([cache_ik] * per_step), ikn_pad)


def _dsa_dec_kernel(pt_ref, q_ref, b_ref, bn_ref, *rest, pps):
    k_refs, v_refs = rest[:pps], rest[pps:2 * pps]
    kn_ref, vn_ref, o_ref, m_scr, l_scr, acc_scr = rest[2 * pps:]
    j = pl.program_id(1)
    nsteps = pl.num_programs(1)
    t = q_ref.shape[1]
    qx_bf = (_expand_heads(q_ref[0], H_B) * (HEAD_DIM ** -0.5)).astype(BF16)

    @pl.when(j == 0)
    def _():
        m_scr[...] = jnp.full(m_scr.shape, NEG, F32)
        l_scr[...] = jnp.zeros(l_scr.shape, F32)
        acc_scr[...] = jnp.zeros(acc_scr.shape, F32)

    def step(s, pv, bias):
        s = s + jnp.concatenate([bias] * H_B, axis=0)
        m_old = m_scr[:, :1]
        m_new = jnp.maximum(m_old, jnp.max(s, axis=1, keepdims=True))
        alpha = jnp.exp(m_old - m_new)
        p = jnp.exp(s - m_new)
        l_scr[...] = jnp.broadcast_to(alpha * l_scr[:, :1] + jnp.sum(p, axis=1, keepdims=True), l_scr.shape)
        acc_scr[...] = alpha * acc_scr[...] + pv(p.astype(BF16))
        m_scr[...] = jnp.broadcast_to(m_new, m_scr.shape)

    kt = _pages_t(k_refs).astype(BF16)
    vt = _pages_t(v_refs).astype(BF16)
    step(jnp.dot(qx_bf, kt, preferred_element_type=F32),
         lambda p: lax.dot_general(p, vt, NT, preferred_element_type=F32), b_ref[0, 0])

    @pl.when(j == nsteps - 1)
    def _():
        kn = kn_ref[0].astype(BF16)
        vn = vn_ref[0].astype(BF16)
        step(lax.dot_general(qx_bf, kn, NT, preferred_element_type=F32),
             lambda p: jnp.dot(p, vn, preferred_element_type=F32), bn_ref[0, 0])
        o_n = acc_scr[...] / l_scr[:, :1]
        o_ref[0] = _collapse_heads(o_n, H_B, t)


def _dsa_decode(layer, page_table, qb, bias, cache_kt, cache_vt, kn_pad, vn_pad):
    nseq, t, _ = qb.shape
    n_pages = page_table.shape[1]
    page_size = cache_kt.shape[4]
    pps = 4 if n_pages % 4 == 0 else 2
    width = bias.shape[3]
    chunk = pps * page_size
    assert n_pages % pps == 0 and width % chunk == 0
    per_w = width // chunk
    rows = H_B * t
    npad = kn_pad.shape[1]
    seq = lambda w, r: pl.BlockSpec((1, r, w), lambda s, j, pt: (s, 0, 0))
    bspec = pl.BlockSpec((1, 1, t, chunk), lambda s, j, pt: (s, j // per_w, 0, j % per_w))
    bnew = pl.BlockSpec((1, 1, t, npad), lambda s, j, pt: (s, bias.shape[1] - 1, 0, 0))
    kspecs = _paged_specs(layer, cache_kt.shape[2:], pps)
    grid_spec = pltpu.PrefetchScalarGridSpec(
        num_scalar_prefetch=1,
        grid=(nseq, n_pages // pps),
        in_specs=[seq(D_B, t), bspec, bnew] + kspecs + kspecs + [seq(D_B, npad)] * 2,
        out_specs=seq(D_B, t),
        scratch_shapes=[pltpu.VMEM((rows, LANES), F32), pltpu.VMEM((rows, LANES), F32),
                        pltpu.VMEM((rows, D_B), F32)],
    )
    return pl.pallas_call(
        functools.partial(_dsa_dec_kernel, pps=pps),
        grid_spec=grid_spec,
        out_shape=jax.ShapeDtypeStruct((nseq, t, D_B), F32),
        compiler_params=_cparams("parallel", "arbitrary"),
        name="dsa_decode",
    )(page_table, qb, bias, bias, *([cache_kt] * pps), *([cache_vt] * pps), kn_pad, vn_pad)


def _rope_tables(pos):
    half = ROT_DIM // 2
    inv = jnp.power(ROPE_THETA, -jnp.arange(half, dtype=F32) / half)
    ang = pos.astype(F32)[:, None] * inv[None, :]
    cos, sin = jnp.cos(ang), jnp.sin(ang)
    n = pos.shape[0]
    ones = jnp.ones((n, HEAD_DIM - ROT_DIM), F32)
    zeros = jnp.zeros((n, HEAD_DIM - ROT_DIM), F32)
    zh = jnp.zeros((n, half), F32)
    tc = jnp.concatenate([cos, cos, ones], axis=1)
    ts1 = jnp.concatenate([-sin, zh, zeros], axis=1)
    ts2 = jnp.concatenate([zh, sin, zeros], axis=1)
    dup = lambda a: jnp.concatenate([a, a], axis=1)
    return dup(tc), dup(ts1), dup(ts2)


def _pad_w_in(w):
    d = w.shape[0]
    main = w[:, :8 * _G]
    o = 8 * _G
    iq = w[:, o:o + 256]
    ik = w[:, o + 256:o + 320]
    iw = w[:, o + 320:o + 324]
    qm = w[:, o + 324:o + 324 + D_M]
    gm = w[:, o + 324 + D_M:o + 324 + 2 * D_M]
    iw_pad = jnp.concatenate([iw, jnp.zeros((d, LANES - IDX_HEADS), w.dtype)], axis=1)
    return jnp.concatenate([main, iq, ik, ik, iw_pad, qm, gm], axis=1).astype(BF16)


def kernel(x_prompt, x_sample, mem_prompt, cache_k_a, cache_v_a, cache_k_b, cache_v_b, cache_k_idx,
           cache_mem_k, cache_mem_v, page_table, norm_g, w_in, mem_norm_g, w_mem_kv, w_out, final_norm_g):
    depth = w_in.shape[0]
    b, t, d = x_prompt.shape
    nseq, ts, _ = x_sample.shape
    n_pages = page_table.shape[1]
    page_size = cache_k_a.shape[2]
    past = n_pages * page_size
    assert t % TILE == 0 and t // TILE <= LANES and page_size == LANES
    assert past % MOBA_BLOCK == 0 and ts <= LANES and ts % 8 == 0

    tabs_p = _rope_tables(jnp.arange(t, dtype=jnp.int32))
    pos_s = past + jnp.arange(ts, dtype=jnp.int32)
    tabs_s = tuple(jnp.tile(a, (nseq, 1)) for a in _rope_tables(pos_s))
    fg = final_norm_g.reshape(1, d)
    nbp = t // TILE

    xp, xs = x_prompt, x_sample
    outs = [[] for _ in range(12)]
    for l in range(depth):
        w_pad = _pad_w_in(w_in[l])
        g = norm_g[l].reshape(1, d)
        wo_bf = w_out[l].astype(BF16)
        last = l == depth - 1

        (qa, ka, va, ga, qb, kb, vb, gb, iq, ik, iw, qm, gm,
         ka_bf, va_t, kb_bf, vb_t, ik_bf, kmean) = _project(xp.reshape(b * t, d), g, w_pad, tabs_p, nbp, TILE)
        r3 = lambda a: a.reshape(b, t, a.shape[-1])
        r4 = lambda a: a.reshape(b, nbp, a.shape[-2], TILE)
        mk, mv = _mem_kv(mem_prompt, mem_norm_g[l].reshape(1, d), w_mem_kv[l].astype(BF16))
        kmean_pad = jnp.pad(kmean.reshape(b, nbp, D_A), ((0, 0), (0, LANES - nbp), (0, 0)))
        oa = _moba_prompt(r3(qa), r3(ka_bf), r4(va_t), kmean_pad)
        ob = _dsa_prompt(r3(qb), r3(iq).astype(BF16), r3(iw), r3(ik_bf), r3(kb_bf), r4(vb_t))
        xp = _merge(xp, oa, r3(ga), ob, r3(gb), r3(qm), r3(gm), mk, mv, wo_bf, fg, TILE, last)
        for lst, a, shp in ((outs[0], ka, (b, t, H_A, HEAD_DIM)), (outs[1], va, (b, t, H_A, HEAD_DIM)),
                            (outs[2], kb, (b, t, H_B, HEAD_DIM)), (outs[3], vb, (b, t, H_B, HEAD_DIM)),
                            (outs[4], ik, (b, t, IDX_DIM)), (outs[5], mk, (b, N_MEM, H_M, HEAD_DIM)),
                            (outs[6], mv, (b, N_MEM, H_M, HEAD_DIM))):
            lst.append(a.reshape(shp))

        n_s = nseq * ts
        tm_s = n_s if n_s <= TILE else TILE
        (qa, ka, va, ga, qb, kb, vb, gb, iq, ik, iw, qm, gm,
         _, _, _, _, _, _) = _project(xs.reshape(n_s, d), g, w_pad, tabs_s, n_s // tm_s, tm_s)
        s3 = lambda a: a.reshape(nseq, ts, a.shape[-1])
        padn = lambda a: jnp.pad(s3(a), ((0, 0), (0, LANES - ts), (0, 0)))
        oa = _moba_decode(l, page_table, s3(qa), _cache_view(cache_k_a), _cache_view(cache_v_a), padn(ka), padn(va))
        iq_rows = s3(iq).reshape(nseq, ts, IDX_HEADS, IDX_DIM).transpose(0, 2, 1, 3).reshape(
            nseq, IDX_HEADS * ts, IDX_DIM)
        iw_rows = (s3(iw)[:, :, :IDX_HEADS] * (IDX_HEADS ** -0.5)).transpose(0, 2, 1).reshape(
            nseq, IDX_HEADS * ts, 1)
        iw_rows = jnp.broadcast_to(iw_rows, (nseq, IDX_HEADS * ts, LANES))
        bias = _idx_decode(l, page_table, iq_rows, iw_rows, jnp.transpose(cache_k_idx, (0, 1, 3, 2)), padn(ik), ts)
        ob = _dsa_decode(l, page_table, s3(qb), bias, _cache_view(cache_k_b), _cache_view(cache_v_b),
                         padn(kb), padn(vb))
        mks = cache_mem_k[l].reshape(nseq, N_MEM, D_M)
        mvs = cache_mem_v[l].reshape(nseq, N_MEM, D_M)
        xs = _merge(xs, oa, s3(ga), ob, s3(gb), s3(qm), s3(gm), mks, mvs, wo_bf, fg, ts, last)
        for lst, a, shp in ((outs[7], ka, (nseq, ts, H_A, HEAD_DIM)), (outs[8], va, (nseq, ts, H_A, HEAD_DIM)),
                            (outs[9], kb, (nseq, ts, H_B, HEAD_DIM)), (outs[10], vb, (nseq, ts, H_B, HEAD_DIM)),
                            (outs[11], ik, (nseq, ts, IDX_DIM))):
            lst.append(a.reshape(shp))

    stk = [jnp.stack(o) for o in outs]
    return (xp, xs, stk[0], stk[1], stk[2], stk[3], stk[4], stk[5], stk[6],
            stk[7], stk[8], stk[9], stk[10], stk[11])
```

```python
import functools

import jax
import jax.numpy as jnp
from jax import lax
from jax.experimental import pallas as pl
from jax.experimental.pallas import tpu as pltpu

HEAD_DIM = 64
H_A = 6
H_B = 6
H_M = 4
D_A = H_A * HEAD_DIM
D_B = H_B * HEAD_DIM
D_M = H_M * HEAD_DIM
N_MEM = 256
MOBA_BLOCK = 256
MOBA_TOPK = 3
IDX_HEADS = 4
IDX_DIM = 64
IDX_TOPK = 256
ROPE_THETA = 500000.0
ROT_DIM = HEAD_DIM // 4
EPS = 1e-6

LANES = 128
TILE = 256
NEG = -1e30
LOG2E = 1.4426950408889634
SUM_ROWS = 16
VT_ROWS = LANES + SUM_ROWS
INT_MIN = -(2 ** 31)
I16_MIN = -(2 ** 15)
VMEM_LIMIT = 60 * 1024 * 1024
IDX_PAGES_PER_STEP = 16

_G = D_A
OFF_QA, OFF_KA, OFF_VA, OFF_GA = 0, _G, 2 * _G, 3 * _G
OFF_QB, OFF_KB, OFF_VB, OFF_GB = 4 * _G, 5 * _G, 6 * _G, 7 * _G
OFF_IQ = 8 * _G
OFF_IK = OFF_IQ + 256
OFF_IW = OFF_IK + LANES
OFF_QM = OFF_IW + LANES
OFF_GM = OFF_QM + D_M
W_PAD = OFF_GM + D_M

F32 = jnp.float32
BF16 = jnp.bfloat16
NT = (((1,), (1,)), ((), ()))


def _cparams(*sem):
    return pltpu.CompilerParams(dimension_semantics=sem, vmem_limit_bytes=VMEM_LIMIT)


def _half_mask(hh):
    lane = lax.broadcasted_iota(jnp.int32, (1, LANES), 1)
    return (lane >= HEAD_DIM * hh) & (lane < HEAD_DIM * (hh + 1))


def _sortable(score):
    bits = lax.bitcast_convert_type(score, jnp.int32)
    key = bits ^ (lax.shift_right_arithmetic(bits, 31) & jnp.int32(0x7FFFFFFF))
    return jnp.where(score == 0.0, jnp.int32(0), key)


def _head_pair_queries(qc):
    scaled = qc * (HEAD_DIM ** -0.5 * LOG2E)
    return jnp.concatenate([jnp.where(_half_mask(0), scaled, 0.0), jnp.where(_half_mask(1), scaled, 0.0)],
                           axis=0).astype(BF16)


def _pair_output(acc):
    t = acc.shape[1] // 2
    o = acc[:LANES] / acc[LANES:LANES + 1]
    d = lax.broadcasted_iota(jnp.int32, (LANES, t), 0)
    return jnp.where(d < HEAD_DIM, o[:, :t], o[:, t:]).T


def _proj_kernel(x_ref, g_ref, w_ref, tc_ref, ts1_ref, ts2_ref,
                 qa_ref, ka_ref, va_ref, ga_ref, qb_ref, kb_ref, vb_ref, gb_ref,
                 iq_ref, ik_ref, iw_ref, qm_ref, gm_ref,
                 kabf_ref, vat_ref, kbbf_ref, vbt_ref, ikbf_ref, kmean_ref, *, kv_transposed):
    x = x_ref[...]
    ms = jnp.mean(x * x, axis=-1, keepdims=True)
    h = (x * lax.rsqrt(ms + EPS) * g_ref[...]).astype(BF16)
    tc, ts1, ts2 = tc_ref[...], ts1_ref[...], ts2_ref[...]

    def mm(off, width):
        return jnp.dot(h, w_ref[:, off:off + width], preferred_element_type=F32)

    def rope(u):
        outs = []
        for c in range(u.shape[1] // LANES):
            uc = u[:, c * LANES:(c + 1) * LANES]
            outs.append(uc * tc + pltpu.roll(uc, LANES - ROT_DIM // 2, 1) * ts1
                        + pltpu.roll(uc, ROT_DIM // 2, 1) * ts2)
        return outs[0] if len(outs) == 1 else jnp.concatenate(outs, axis=1)

    def put(ref, val, val_t=None):
        if kv_transposed:
            ref[0] = val.T if val_t is None else val_t
        else:
            ref[...] = val

    qa_ref[...] = rope(mm(OFF_QA, D_A))
    ka = rope(mm(OFF_KA, D_A))
    put(ka_ref, ka)
    kabf_ref[...] = ka.astype(BF16)
    kmean_ref[0] = jnp.mean(ka, axis=0, keepdims=True)
    va = mm(OFF_VA, D_A)
    va_t = va.T
    put(va_ref, va, va_t)
    vat_ref[0] = _vt_with_ones(va_t)
    ga_ref[...] = mm(OFF_GA, D_A)
    qb_ref[...] = rope(mm(OFF_QB, D_B))
    kb = rope(mm(OFF_KB, D_B))
    put(kb_ref, kb)
    kbbf_ref[...] = kb.astype(BF16)
    vb = mm(OFF_VB, D_B)
    vb_t = vb.T
    put(vb_ref, vb, vb_t)
    vbt_ref[0] = _vt_with_ones(vb_t)
    gb_ref[...] = mm(OFF_GB, D_B)
    iq_ref[...] = rope(mm(OFF_IQ, IDX_HEADS * IDX_DIM))
    ik2 = rope(mm(OFF_IK, LANES))
    put(ik_ref, ik2[:, :IDX_DIM], ik2.T[:IDX_DIM] if kv_transposed else None)
    ikbf_ref[...] = ik2.astype(BF16)
    iw_ref[...] = mm(OFF_IW, LANES)
    qm_ref[...] = mm(OFF_QM, D_M)
    gm_ref[...] = mm(OFF_GM, D_M)


def _project(x2d, g, w_pad, tabs, n_tab_tiles, tm, kv_transposed):
    n, d = x2d.shape
    nt = n // tm
    tok = lambda w: pl.BlockSpec((tm, w), lambda i: (i, 0))
    tab = pl.BlockSpec((tm, LANES), lambda i: (i % n_tab_tiles, 0))
    vt = lambda w: pl.BlockSpec((1, w // LANES * VT_ROWS, tm), lambda i: (i, 0, 0))
    vt_shape = lambda w: jax.ShapeDtypeStruct((nt, w // LANES * VT_ROWS, tm), BF16)
    widths = [D_A] * 4 + [D_B] * 4 + [IDX_HEADS * IDX_DIM, IDX_DIM, LANES, D_M, D_M]
    out_shape = [jax.ShapeDtypeStruct((n, w), F32) for w in widths]
    out_specs = [tok(w) for w in widths]
    if kv_transposed:
        for idx in (1, 2, 5, 6, 9):
            w = widths[idx]
            out_shape[idx] = jax.ShapeDtypeStruct((nt // n_tab_tiles, w, n_tab_tiles * tm), F32)
            out_specs[idx] = pl.BlockSpec((1, w, tm), lambda i: (i // n_tab_tiles, 0, i % n_tab_tiles))
    out_shape += [jax.ShapeDtypeStruct((n, D_A), BF16), vt_shape(D_A),
                  jax.ShapeDtypeStruct((n, D_B), BF16), vt_shape(D_B),
                  jax.ShapeDtypeStruct((n, LANES), BF16), jax.ShapeDtypeStruct((nt, 1, D_A), F32)]
    out_specs += [tok(D_A), vt(D_A), tok(D_B), vt(D_B), tok(LANES), pl.BlockSpec((1, 1, D_A), lambda i: (i, 0, 0))]
    return pl.pallas_call(
        functools.partial(_proj_kernel, kv_transposed=kv_transposed),
        grid=(nt,),
        in_specs=[tok(d),
                  pl.BlockSpec((1, d), lambda i: (0, 0)),
                  pl.BlockSpec((d, W_PAD), lambda i: (0, 0)),
                  tab, tab, tab],
        out_specs=out_specs,
        out_shape=out_shape,
        compiler_params=_cparams("parallel"),
        name="proj",
    )(x2d, g, w_pad, *tabs)


def _memkv_kernel(m_ref, g_ref, w_ref, mk_ref, mv_ref):
    x = m_ref[0]
    ms = jnp.mean(x * x, axis=-1, keepdims=True)
    h = (x * lax.rsqrt(ms + EPS) * g_ref[...]).astype(BF16)
    u = jnp.dot(h, w_ref[...], preferred_element_type=F32)
    mk_ref[0] = u[:, :D_M]
    mv_ref[0] = u[:, D_M:]


def _mem_kv(mem, g, w_bf):
    b, nm, d = mem.shape
    return pl.pallas_call(
        _memkv_kernel,
        grid=(b,),
        in_specs=[pl.BlockSpec((1, nm, d), lambda i: (i, 0, 0)),
                  pl.BlockSpec((1, d), lambda i: (0, 0)),
                  pl.BlockSpec((d, 2 * D_M), lambda i: (0, 0))],
        out_specs=[pl.BlockSpec((1, nm, D_M), lambda i: (i, 0, 0))] * 2,
        out_shape=[jax.ShapeDtypeStruct((b, nm, D_M), F32)] * 2,
        compiler_params=_cparams("parallel"),
        name="mem_kv",
    )(mem, g, w_bf)


def _flash_step(s, m, acc_ref, c, vt):
    m_new = jnp.maximum(m, jnp.max(s, axis=0, keepdims=True))
    alpha = jnp.exp2(m - m_new)
    p = jnp.exp2(s - m_new).astype(BF16)
    acc_ref[c] = alpha * acc_ref[c] + jnp.dot(vt, p, preferred_element_type=F32)
    return m_new


def _vt_with_ones(v_t):
    vt = v_t.astype(BF16)
    ones = jnp.ones((SUM_ROWS, v_t.shape[1]), BF16)
    parts = []
    for c in range(v_t.shape[0] // LANES):
        parts += [vt[c * LANES:(c + 1) * LANES], ones]
    return jnp.concatenate(parts, axis=0)


def _moba_kernel(q_ref, k_ref, vt_ref, km_ref, o_ref, qaug_scr, acc_scr, s_scr):
    qi = pl.program_id(1)
    n_pairs = D_A // LANES
    blk_id = lax.broadcasted_iota(jnp.int32, (TILE, LANES), 1)
    blk_f = blk_id.astype(F32)
    past = blk_id < qi
    for c in range(n_pairs):
        cs = slice(c * LANES, (c + 1) * LANES)
        qc = q_ref[0, :, cs]
        kmc = km_ref[0, :, cs]
        biases = []
        for hh in range(2):
            qm = jnp.where(_half_mask(hh), qc, 0.0)
            g = lax.dot_general(qm, kmc, NT, precision=lax.Precision.HIGHEST, preferred_element_type=F32)
            g = jnp.where(past, g, -jnp.inf)
            sel = jnp.zeros((TILE, LANES), jnp.bool_)
            for _ in range(MOBA_TOPK):
                m = jnp.max(g, axis=1, keepdims=True)
                first = jnp.min(jnp.where(g == m, blk_f, 1e9), axis=1, keepdims=True)
                pick = (blk_f == first) & past
                sel = sel | pick
                g = jnp.where(pick, -jnp.inf, g)
            biases.append(jnp.where(sel, 0.0, NEG).astype(BF16))
        qaug_scr[c] = jnp.concatenate([_head_pair_queries(qc), jnp.concatenate(biases, axis=0)], axis=1)

    own0 = pl.multiple_of(qi * TILE, TILE)
    ko = k_ref[0, pl.ds(own0, TILE), :]
    vto = vt_ref[0, qi]
    krow = lax.broadcasted_iota(jnp.int32, (TILE, 2 * TILE), 0)
    qcol = lax.broadcasted_iota(jnp.int32, (TILE, 2 * TILE), 1) % TILE
    ms = []
    for c in range(n_pairs):
        cs = slice(c * LANES, (c + 1) * LANES)
        s = lax.dot_general(ko[:, cs], qaug_scr[c, :, :LANES], NT, preferred_element_type=F32)
        s = jnp.where(krow <= qcol, s, NEG)
        m = jnp.max(s, axis=0, keepdims=True)
        p = jnp.exp2(s - m).astype(BF16)
        ms.append(m)
        acc_scr[c] = jnp.dot(vto[c * VT_ROWS:(c + 1) * VT_ROWS, :], p, preferred_element_type=F32)

    def logits(j, c):
        j0 = pl.multiple_of(j * TILE, TILE)
        onehot = jnp.where(blk_id == j, 1.0, 0.0).astype(BF16)
        k_aug = jnp.concatenate([k_ref[0, pl.ds(j0, TILE), c * LANES:(c + 1) * LANES], onehot], axis=1)
        return lax.dot_general(k_aug, qaug_scr[c], NT, preferred_element_type=F32)

    s_scr[0] = logits(0, 0)

    def body(j, ms):
        vtj = vt_ref[0, j]
        new_m = []
        for c in range(n_pairs):
            if c + 1 < n_pairs:
                s_scr[c + 1] = logits(j, c + 1)
            else:
                s_scr[0] = logits(jnp.minimum(j + 1, qi - 1), 0)
            new_m.append(_flash_step(s_scr[c], ms[c], acc_scr, c, vtj[c * VT_ROWS:(c + 1) * VT_ROWS, :]))
        return tuple(new_m)

    lax.fori_loop(0, qi, body, tuple(ms))
    o_ref[0] = jnp.concatenate([_pair_output(acc_scr[c]) for c in range(n_pairs)], axis=1)


def _moba_prompt(qa, ka_bf, va_t, kmean_pad):
    b, t, _ = qa.shape
    nt = t // TILE
    return pl.pallas_call(
        _moba_kernel,
        grid=(b, nt),
        in_specs=[pl.BlockSpec((1, TILE, D_A), lambda i, j: (i, j, 0)),
                  pl.BlockSpec((1, t, D_A), lambda i, j: (i, 0, 0)),
                  pl.BlockSpec((1, nt, va_t.shape[2], TILE), lambda i, j: (i, 0, 0, 0)),
                  pl.BlockSpec((1, LANES, D_A), lambda i, j: (i, 0, 0))],
        out_specs=pl.BlockSpec((1, TILE, D_A), lambda i, j: (i, j, 0)),
        out_shape=jax.ShapeDtypeStruct((b, t, D_A), F32),
        scratch_shapes=[pltpu.VMEM((D_A // LANES, 2 * TILE, 2 * LANES), BF16),
                        pltpu.VMEM((D_A // LANES, VT_ROWS, 2 * TILE), F32),
                        pltpu.VMEM((D_A // LANES, TILE, 2 * TILE), F32)],
        compiler_params=_cparams("parallel", "arbitrary"),
        name="moba_prompt",
    )(qa, ka_bf, va_t, kmean_pad)


def _dsa_kernel(q_ref, iq_ref, iw_ref, ik_ref, k_ref, vt_ref, o_ref,
                key_scr, hi_scr, lo_scr, cst_scr, acc_scr, s_scr, *, k_sel, idx_bits):
    qi = pl.program_id(1)
    n_tiles = qi + 1
    n_pairs = D_B // LANES
    kf = float(k_sel)
    krow = lax.broadcasted_iota(jnp.int32, (TILE, TILE), 0)
    q_pos = lax.broadcasted_iota(jnp.int32, (TILE, TILE), 1) + qi * TILE
    key_pos = lambda j: krow + j * TILE

    iq = iq_ref[0]
    iw_t = (iw_ref[0] * (IDX_HEADS ** -0.5)).T
    iqm = [(jnp.where(_half_mask(n % 2), iq[:, (n // 2) * LANES:(n // 2 + 1) * LANES], 0.0)
            * (IDX_DIM ** -0.5)).astype(BF16) for n in range(IDX_HEADS)]

    def score_body(j, carry):
        j0 = pl.multiple_of(j * TILE, TILE)
        ikj = ik_ref[0, pl.ds(j0, TILE), :]
        score = jnp.zeros((TILE, TILE), F32)
        for n in range(IDX_HEADS):
            rel = lax.dot_general(ikj, iqm[n], NT, preferred_element_type=F32)
            score = score + iw_t[n:n + 1, :] * jnp.maximum(rel, 0.0)
        key = jnp.where(key_pos(j) > q_pos, jnp.int32(INT_MIN), _sortable(score))
        key_scr[j] = key
        hi_scr[j] = lax.shift_right_arithmetic(key, 16).astype(jnp.int16)
        lo_scr[j] = ((key & 0xFFFF) + I16_MIN).astype(jnp.int16)
        return carry

    lax.fori_loop(0, n_tiles, score_body, 0)

    def count16(plane, pred):
        def body(j, acc):
            w = jnp.where(pred(plane[j]), jnp.int16(1), jnp.int16(0))
            for r in range(TILE // 64):
                acc = acc + w[r * 64:(r + 1) * 64]
            return acc
        acc = lax.fori_loop(0, n_tiles, body, jnp.zeros((64, TILE), jnp.int16))
        return jnp.sum(acc.astype(jnp.int32).astype(F32), axis=0, keepdims=True)

    def count32(pred):
        def body(j, acc):
            w = jnp.where(pred(key_scr[j], j), 1.0, 0.0)
            for r in range(TILE // 64):
                acc = acc + w[r * 64:(r + 1) * 64]
            return acc
        acc = lax.fori_loop(0, n_tiles, body, jnp.zeros((64, TILE), F32))
        return jnp.sum(acc, axis=0, keepdims=True)

    def search16(plane, target):
        zero = jnp.zeros((1, TILE), jnp.int32)
        c0 = count16(plane, lambda x: x >= zero.astype(jnp.int16))
        thr = jnp.where(c0 >= target, zero, jnp.int32(I16_MIN))

        def bit(i, thr):
            cand = thr | lax.shift_left(jnp.int32(1), 14 - i)
            c = count16(plane, lambda x: x >= cand.astype(jnp.int16))
            return jnp.where(c >= target, cand, thr)

        return lax.fori_loop(0, 15, bit, thr)

    thr_hi = search16(hi_scr, kf)
    thr_hi16 = thr_hi.astype(jnp.int16)
    above = count16(hi_scr, lambda x: x > thr_hi16)

    def mask_lo(j, carry):
        lo_scr[j] = jnp.where(hi_scr[j] == thr_hi16, lo_scr[j], jnp.int16(I16_MIN))
        return carry

    lax.fori_loop(0, n_tiles, mask_lo, 0)
    thr_lo = search16(lo_scr, kf - above)
    thr = thr_hi * 65536 + (thr_lo - I16_MIN)

    cnt_ge = count32(lambda k, j: k >= thr)
    need = kf - count32(lambda k, j: k > thr)
    cst_scr[...] = jnp.full(cst_scr.shape, jnp.int32(2 ** 30))

    @pl.when(jnp.max(cnt_ge) > kf)
    def _():
        def cbit(i, cpos):
            cand = cpos | lax.shift_left(jnp.int32(1), idx_bits - 1 - i)
            cnt = count32(lambda k, j: (k == thr) & (key_pos(j) < cand))
            return jnp.where(cnt < need, cand, cpos)
        cpos = lax.fori_loop(0, idx_bits, cbit, jnp.zeros((1, TILE), jnp.int32))
        cst_scr[...] = jnp.broadcast_to(cpos, cst_scr.shape)

    cstar = cst_scr[:1, :]

    def to_bias(j, carry):
        k = key_scr[j]
        kp = key_pos(j)
        keep = ((k > thr) | ((k == thr) & (kp <= cstar))) & (kp <= q_pos)
        key_scr[j] = lax.bitcast_convert_type(jnp.where(keep, 0.0, NEG).astype(F32), jnp.int32)
        return carry

    lax.fori_loop(0, n_tiles, to_bias, 0)

    q_pairs = [_head_pair_queries(q_ref[0, :, c * LANES:(c + 1) * LANES]) for c in range(n_pairs)]
    for c in range(n_pairs):
        acc_scr[c] = jnp.zeros((VT_ROWS, 2 * TILE), F32)

    def logits(j, c):
        j0 = pl.multiple_of(j * TILE, TILE)
        return lax.dot_general(k_ref[0, pl.ds(j0, TILE), c * LANES:(c + 1) * LANES], q_pairs[c], NT,
                               preferred_element_type=F32)

    s_scr[0] = logits(0, 0)

    def body(j, ms):
        vtj = vt_ref[0, j]
        bias = lax.bitcast_convert_type(key_scr[j], F32)
        bias2 = jnp.concatenate([bias, bias], axis=1)
        new_m = []
        for c in range(n_pairs):
            if c + 1 < n_pairs:
                s_scr[c + 1] = logits(j, c + 1)
            else:
                s_scr[0] = logits(jnp.minimum(j + 1, n_tiles - 1), 0)
            new_m.append(_flash_step(s_scr[c] + bias2, ms[c], acc_scr, c, vtj[c * VT_ROWS:(c + 1) * VT_ROWS, :]))
        return tuple(new_m)

    lax.fori_loop(0, n_tiles, body, tuple(jnp.full((1, 2 * TILE), NEG, F32) for _ in range(n_pairs)))
    o_ref[0] = jnp.concatenate([_pair_output(acc_scr[c]) for c in range(n_pairs)], axis=1)


def _dsa_prompt(qb, iq, iw, ik_bf, kb_bf, vb_t):
    b, t, _ = qb.shape
    nt = t // TILE
    k_sel = min(IDX_TOPK, t // 4)
    kern = functools.partial(_dsa_kernel, k_sel=k_sel, idx_bits=max(1, (t - 1).bit_length()))
    res = lambda w: pl.BlockSpec((1, t, w), lambda i, j: (i, 0, 0))
    til = lambda w: pl.BlockSpec((1, TILE, w), lambda i, j: (i, j, 0))
    return pl.pallas_call(
        kern,
        grid=(b, nt),
        in_specs=[til(D_B), til(IDX_HEADS * IDX_DIM), til(LANES), res(LANES), res(D_B),
                  pl.BlockSpec((1, nt, vb_t.shape[2], TILE), lambda i, j: (i, 0, 0, 0))],
        out_specs=til(D_B),
        out_shape=jax.ShapeDtypeStruct((b, t, D_B), F32),
        scratch_shapes=[pltpu.VMEM((nt, TILE, TILE), jnp.int32),
                        pltpu.VMEM((nt, TILE, TILE), jnp.int16),
                        pltpu.VMEM((nt, TILE, TILE), jnp.int16),
                        pltpu.VMEM((8, TILE), jnp.int32),
                        pltpu.VMEM((D_B // LANES, VT_ROWS, 2 * TILE), F32),
                        pltpu.VMEM((D_B // LANES, TILE, 2 * TILE), F32)],
        compiler_params=_cparams("parallel", "arbitrary"),
        name="dsa_prompt",
    )(qb, iq, iw, ik_bf, kb_bf, vb_t)


def _merge_kernel(x_ref, oa_ref, ga_ref, ob_ref, gb_ref, qm_ref, gm_ref, mk_ref, mv_ref, wo_ref, fg_ref, y_ref,
                  *, final):
    lane = lax.broadcasted_iota(jnp.int32, (1, LANES), 1)
    mk = mk_ref[0].astype(BF16)
    mv = mv_ref[0].astype(BF16)
    oms = []
    for c in range(D_M // LANES):
        cs = slice(c * LANES, (c + 1) * LANES)
        qc = qm_ref[0, :, cs]
        heads = []
        for hh in range(2):
            q_bf = (jnp.where(_half_mask(hh), qc, 0.0) * (HEAD_DIM ** -0.5)).astype(BF16)
            s = lax.dot_general(q_bf, mk[:, cs], NT, preferred_element_type=F32)
            m = jnp.max(s, axis=1, keepdims=True)
            p = jnp.exp(s - m)
            l = jnp.sum(p, axis=1, keepdims=True)
            heads.append(jnp.dot(p.astype(BF16), mv[:, cs], preferred_element_type=F32) / l)
        oms.append(jnp.where(lane < HEAD_DIM, heads[0], heads[1]))
    om = jnp.concatenate(oms, axis=1)
    silu = lambda g: g / (1.0 + jnp.exp(-g))
    ymix = jnp.concatenate([oa_ref[0] * silu(ga_ref[0]), ob_ref[0] * silu(gb_ref[0]), om * silu(gm_ref[0])],
                           axis=1).astype(BF16)
    z = x_ref[0] + jnp.dot(ymix, wo_ref[...], preferred_element_type=F32)
    if final:
        ms = jnp.mean(z * z, axis=-1, keepdims=True)
        z = z * lax.rsqrt(ms + EPS) * fg_ref[...]
    y_ref[0] = z


def _merge(x, oa, ga, ob, gb, qm, gm, mk, mv, wo_bf, fg, tm, final):
    b, t, d = x.shape
    til = lambda w: pl.BlockSpec((1, tm, w), lambda i, j: (i, j, 0))
    memspec = pl.BlockSpec((1, N_MEM, D_M), lambda i, j: (i, 0, 0))
    return pl.pallas_call(
        functools.partial(_merge_kernel, final=final),
        grid=(b, t // tm),
        in_specs=[til(d), til(D_A), til(D_A), til(D_B), til(D_B), til(D_M), til(D_M), memspec, memspec,
                  pl.BlockSpec((d, d), lambda i, j: (0, 0)),
                  pl.BlockSpec((1, d), lambda i, j: (0, 0))],
        out_specs=til(d),
        out_shape=jax.ShapeDtypeStruct((b, t, d), F32),
        compiler_params=_cparams("parallel", "parallel"),
        name="merge",
    )(x, oa, ga, ob, gb, qm, gm, mk, mv, wo_bf, fg)


def _expand_heads(q, n_heads):
    t, w = q.shape
    qt = jnp.concatenate([q] * n_heads, axis=0)
    r = lax.broadcasted_iota(jnp.int32, (n_heads * t, w), 0)
    l = lax.broadcasted_iota(jnp.int32, (n_heads * t, w), 1)
    keep = (l >= (r // t) * HEAD_DIM) & (l < (r // t + 1) * HEAD_DIM)
    return jnp.where(keep, qt, 0.0)


def _collapse_heads(o, n_heads, t):
    w = o.shape[1]
    l = lax.broadcasted_iota(jnp.int32, (t, w), 1)
    out = jnp.zeros((t, w), F32)
    for h in range(n_heads):
        keep = (l >= h * HEAD_DIM) & (l < (h + 1) * HEAD_DIM)
        out = out + jnp.where(keep, o[h * t:(h + 1) * t, :], 0.0)
    return out


def _pages_t(refs):
    pages = [r[0, 0].reshape(r.shape[2] * r.shape[3], r.shape[4]) for r in refs]
    return pages[0] if len(pages) == 1 else jnp.concatenate(pages, axis=1)


def _cache_view(cache):
    return jnp.transpose(cache, (0, 1, 3, 4, 2))


def _values_with_ones(vt):
    return jnp.concatenate([vt.astype(BF16), jnp.ones((SUM_ROWS, vt.shape[1]), BF16)], axis=0)


def _softmax_pv(s_scr, vt_scr, s_new, v_new):
    nsteps = s_scr.shape[0]
    d = v_new.shape[1]
    m = jnp.max(jnp.max(s_scr[...], axis=0), axis=1, keepdims=True)
    m = jnp.maximum(m, jnp.max(s_new, axis=1, keepdims=True))
    p_new = jnp.exp(s_new - m)
    l_new = jnp.sum(p_new, axis=1, keepdims=True)
    acc_new = jnp.dot(p_new.astype(BF16), v_new, preferred_element_type=F32)

    def body(i, acc):
        p = jnp.exp(s_scr[i] - m).astype(BF16)
        return acc + lax.dot_general(p, vt_scr[i], NT, preferred_element_type=F32)

    acc = lax.fori_loop(0, nsteps, body, jnp.zeros((s_new.shape[0], d + SUM_ROWS), F32),
                        unroll=nsteps if nsteps <= 8 else 8)
    return (acc[:, :d] + acc_new) / (acc[:, d:d + 1] + l_new)


def _paged_specs(layer, block_tail, per_step):
    zeros = (0,) * len(block_tail)
    return [pl.BlockSpec((1, 1) + block_tail, lambda s, j, pt, i=i: (layer, pt[s, j * per_step + i]) + zeros)
            for i in range(per_step)]


def _moba_dec_kernel(pt_ref, q_ref, *rest, ppb, bps):
    n_pg = ppb * bps
    k_refs, v_refs = rest[:n_pg], rest[n_pg:2 * n_pg]
    kn_ref, vn_ref, o_ref, s_scr, vt_scr, g_scr, qxt_scr = rest[2 * n_pg:]
    j = pl.program_id(1)
    nsteps = pl.num_programs(1)
    t = q_ref.shape[1]
    rows = H_A * t
    blk_w = s_scr.shape[2] // bps
    qx = _expand_heads(q_ref[0], H_A)
    qx_bf = (qx * (HEAD_DIM ** -0.5)).astype(BF16)

    @pl.when(j == 0)
    def _():
        qxt_scr[...] = jnp.concatenate([qx, jnp.zeros((LANES - rows, D_A), F32)], axis=0).T

    kt = _pages_t(k_refs)
    s_scr[j] = jnp.dot(qx_bf, kt.astype(BF16), preferred_element_type=F32)
    vt_scr[j] = _values_with_ones(_pages_t(v_refs))
    for b in range(bps):
        kmean = jnp.mean(kt[:, b * blk_w:(b + 1) * blk_w], axis=1, keepdims=True)
        g_scr[pl.ds(j * bps + b, 1), :] = jnp.sum(qxt_scr[...] * kmean, axis=0, keepdims=True)

    @pl.when(j == nsteps - 1)
    def _():
        npad = kn_ref.shape[1]
        r = lax.broadcasted_iota(jnp.int32, (rows, npad), 0)
        cidx = lax.broadcasted_iota(jnp.int32, (rows, npad), 1)
        s_new = lax.dot_general(qx_bf, kn_ref[0].astype(BF16), NT, preferred_element_type=F32)
        s_new = jnp.where(cidx <= (r % t), s_new, NEG)

        n_blocks = g_scr.shape[0]
        g = g_scr[...]
        bidx = lax.broadcasted_iota(jnp.int32, g.shape, 0).astype(F32)
        sel = jnp.zeros(g.shape, jnp.bool_)
        for _ in range(min(MOBA_TOPK, n_blocks)):
            mx = jnp.max(g, axis=0, keepdims=True)
            first = jnp.min(jnp.where(g == mx, bidx, 1e9), axis=0, keepdims=True)
            pick = bidx == first
            sel = sel | pick
            g = jnp.where(pick, -jnp.inf, g)
        bias_t = jnp.where(sel, 0.0, NEG).T
        for i in range(s_scr.shape[0]):
            cols = [jnp.broadcast_to(bias_t[:rows, i * bps + b:i * bps + b + 1], (rows, blk_w)) for b in range(bps)]
            s_scr[i] = s_scr[i] + (cols[0] if bps == 1 else jnp.concatenate(cols, axis=1))
        o_ref[0] = _collapse_heads(_softmax_pv(s_scr, vt_scr, s_new, vn_ref[0].astype(BF16)), H_A, t)


def _moba_decode(layer, page_table, qa, cache_kt, cache_vt, kn_pad, vn_pad):
    nseq, t, _ = qa.shape
    n_pages = page_table.shape[1]
    page_size = cache_kt.shape[4]
    ppb = MOBA_BLOCK // page_size
    nb = n_pages // ppb
    bps = 2 if nb % 2 == 0 else 1
    rows = H_A * t
    seq = lambda w, r: pl.BlockSpec((1, r, w), lambda s, j, pt: (s, 0, 0))
    kspecs = _paged_specs(layer, cache_kt.shape[2:], ppb * bps)
    grid_spec = pltpu.PrefetchScalarGridSpec(
        num_scalar_prefetch=1,
        grid=(nseq, nb // bps),
        in_specs=[seq(D_A, t)] + kspecs + kspecs + [seq(D_A, kn_pad.shape[1])] * 2,
        out_specs=seq(D_A, t),
        scratch_shapes=[pltpu.VMEM((nb // bps, rows, bps * MOBA_BLOCK), F32),
                        pltpu.VMEM((nb // bps, D_A + SUM_ROWS, bps * MOBA_BLOCK), BF16),
                        pltpu.VMEM((nb, LANES), F32),
                        pltpu.VMEM((D_A, LANES), F32)],
    )
    assert rows <= LANES
    return pl.pallas_call(
        functools.partial(_moba_dec_kernel, ppb=ppb, bps=bps),
        grid_spec=grid_spec,
        out_shape=jax.ShapeDtypeStruct((nseq, t, D_A), F32),
        compiler_params=_cparams("parallel", "arbitrary"),
        name="moba_decode",
    )(page_table, qa, *([cache_kt] * (ppb * bps)), *([cache_vt] * (ppb * bps)), kn_pad, vn_pad)


def _idx_dec_kernel(pt_ref, iq_ref, iw_ref, *rest, per_step, k_sel, idx_bits, past, t):
    ik_refs = rest[:per_step]
    ikn_ref, bias_ref, key_scr = rest[per_step:]
    j = pl.program_id(1)
    nsteps = pl.num_programs(1)
    width = key_scr.shape[2]
    iq_bf = iq_ref[0].astype(BF16)
    iw = iw_ref[0][:, :1]

    def scores(rel):
        wrel = iw * jnp.maximum(rel * (IDX_DIM ** -0.5), 0.0)
        sc = wrel[0:t]
        for n in range(1, IDX_HEADS):
            sc = sc + wrel[n * t:(n + 1) * t]
        return sc

    ik_t = jnp.concatenate([r[0, 0] for r in ik_refs], axis=1)
    key_scr[j] = _sortable(scores(jnp.dot(iq_bf, ik_t.astype(BF16), preferred_element_type=F32)))

    @pl.when(j == nsteps - 1)
    def _():
        kf = float(k_sel)
        npad = ikn_ref.shape[1]
        r_new = lax.broadcasted_iota(jnp.int32, (t, npad), 0)
        c_new = lax.broadcasted_iota(jnp.int32, (t, npad), 1)
        rel_new = lax.dot_general(iq_bf, ikn_ref[0].astype(BF16), NT, preferred_element_type=F32)
        key_new = jnp.where(c_new <= r_new, _sortable(scores(rel_new)), jnp.int32(INT_MIN))
        tail = jnp.full((t, width - npad), jnp.int32(INT_MIN))
        key_scr[nsteps] = jnp.concatenate([key_new, tail], axis=1)

        keys = key_scr[...]
        pos = (lax.broadcasted_iota(jnp.int32, keys.shape, 0) * width
               + lax.broadcasted_iota(jnp.int32, keys.shape, 2))

        def count(pred):
            w = jnp.sum(jnp.where(pred, 1.0, 0.0), axis=0)
            acc = w[:, :LANES]
            for s in range(1, width // LANES):
                acc = acc + w[:, s * LANES:(s + 1) * LANES]
            return jnp.sum(acc, axis=1, keepdims=True)[None]

        zero = jnp.zeros((1, t, 1), jnp.int32)
        thr = jnp.where(count(keys >= zero) >= kf, zero, jnp.int32(INT_MIN))

        def bit(i, thr):
            cand = thr | lax.shift_left(jnp.int32(1), 30 - i)
            return jnp.where(count(keys >= cand) >= kf, cand, thr)

        thr = lax.fori_loop(0, 31, bit, thr)
        need = kf - count(keys > thr)

        def cbit(i, cpos):
            cand = cpos | lax.shift_left(jnp.int32(1), idx_bits - 1 - i)
            return jnp.where(count((keys == thr) & (pos < cand)) < need, cand, cpos)

        cstar = lax.fori_loop(0, idx_bits, cbit, zero)
        keep = (keys > thr) | ((keys == thr) & (pos <= cstar))
        bias_ref[0] = jnp.where(keep, 0.0, NEG).astype(F32)


def _idx_decode(layer, page_table, iq_rows, iw_rows, cache_ik, ikn_pad, t):
    nseq = iq_rows.shape[0]
    n_pages = page_table.shape[1]
    page_size = cache_ik.shape[3]
    per_step = IDX_PAGES_PER_STEP
    while n_pages % per_step:
        per_step //= 2
    nsteps = n_pages // per_step
    width = per_step * page_size
    past = n_pages * page_size
    k_sel = min(IDX_TOPK, (past + t) // 4)
    idx_bits = max(1, ((nsteps + 1) * width - 1).bit_length())
    seq = lambda r, w: pl.BlockSpec((1, r, w), lambda s, j, pt: (s, 0, 0))
    grid_spec = pltpu.PrefetchScalarGridSpec(
        num_scalar_prefetch=1,
        grid=(nseq, nsteps),
        in_specs=[seq(IDX_HEADS * t, IDX_DIM), seq(IDX_HEADS * t, LANES)]
        + _paged_specs(layer, cache_ik.shape[2:], per_step) + [seq(ikn_pad.shape[1], IDX_DIM)],
        out_specs=pl.BlockSpec((1, nsteps + 1, t, width), lambda s, j, pt: (s, 0, 0, 0)),
        scratch_shapes=[pltpu.VMEM((nsteps + 1, t, width), jnp.int32)],
    )
    kern = functools.partial(_idx_dec_kernel, per_step=per_step, k_sel=k_sel, idx_bits=idx_bits, past=past, t=t)
    return pl.pallas_call(
        kern,
        grid_spec=grid_spec,
        out_shape=jax.ShapeDtypeStruct((nseq, nsteps + 1, t, width), F32),
        compiler_params=_cparams("parallel", "arbitrary"),
        name="idx_decode",
    )(page_table, iq_rows, iw_rows, *([cache_ik] * per_step), ikn_pad)


def _dsa_dec_kernel(pt_ref, q_ref, b_ref, bn_ref, *rest, pps):
    k_refs, v_refs = rest[:pps], rest[pps:2 * pps]
    kn_ref, vn_ref, o_ref, s_scr, vt_scr = rest[2 * pps:]
    j = pl.program_id(1)
    nsteps = pl.num_programs(1)
    t = q_ref.shape[1]
    qx_bf = (_expand_heads(q_ref[0], H_B) * (HEAD_DIM ** -0.5)).astype(BF16)

    kt = _pages_t(k_refs).astype(BF16)
    s_scr[j] = jnp.dot(qx_bf, kt, preferred_element_type=F32) + jnp.concatenate([b_ref[0, 0]] * H_B, axis=0)
    vt_scr[j] = _values_with_ones(_pages_t(v_refs))

    @pl.when(j == nsteps - 1)
    def _():
        s_new = lax.dot_general(qx_bf, kn_ref[0].astype(BF16), NT, preferred_element_type=F32)
        s_new = s_new + jnp.concatenate([bn_ref[0, 0]] * H_B, axis=0)
        o_ref[0] = _collapse_heads(_softmax_pv(s_scr, vt_scr, s_new, vn_ref[0].astype(BF16)), H_B, t)


def _dsa_decode(layer, page_table, qb, bias, cache_kt, cache_vt, kn_pad, vn_pad):
    nseq, t, _ = qb.shape
    n_pages = page_table.shape[1]
    page_size = cache_kt.shape[4]
    pps = 4 if n_pages % 4 == 0 else 2
    width = bias.shape[3]
    chunk = pps * page_size
    assert n_pages % pps == 0 and width % chunk == 0
    per_w = width // chunk
    rows = H_B * t
    npad = kn_pad.shape[1]
    seq = lambda w, r: pl.BlockSpec((1, r, w), lambda s, j, pt: (s, 0, 0))
    bspec = pl.BlockSpec((1, 1, t, chunk), lambda s, j, pt: (s, j // per_w, 0, j % per_w))
    bnew = pl.BlockSpec((1, 1, t, npad), lambda s, j, pt: (s, bias.shape[1] - 1, 0, 0))
    kspecs = _paged_specs(layer, cache_kt.shape[2:], pps)
    grid_spec = pltpu.PrefetchScalarGridSpec(
        num_scalar_prefetch=1,
        grid=(nseq, n_pages // pps),
        in_specs=[seq(D_B, t), bspec, bnew] + kspecs + kspecs + [seq(D_B, npad)] * 2,
        out_specs=seq(D_B, t),
        scratch_shapes=[pltpu.VMEM((n_pages // pps, rows, chunk), F32),
                        pltpu.VMEM((n_pages // pps, D_B + SUM_ROWS, chunk), BF16)],
    )
    return pl.pallas_call(
        functools.partial(_dsa_dec_kernel, pps=pps),
        grid_spec=grid_spec,
        out_shape=jax.ShapeDtypeStruct((nseq, t, D_B), F32),
        compiler_params=_cparams("parallel", "arbitrary"),
        name="dsa_decode",
    )(page_table, qb, bias, bias, *([cache_kt] * pps), *([cache_vt] * pps), kn_pad, vn_pad)


def _rope_tables(pos):
    half = ROT_DIM // 2
    inv = jnp.power(ROPE_THETA, -jnp.arange(half, dtype=F32) / half)
    ang = pos.astype(F32)[:, None] * inv[None, :]
    cos, sin = jnp.cos(ang), jnp.sin(ang)
    n = pos.shape[0]
    ones = jnp.ones((n, HEAD_DIM - ROT_DIM), F32)
    zeros = jnp.zeros((n, HEAD_DIM - ROT_DIM), F32)
    zh = jnp.zeros((n, half), F32)
    tc = jnp.concatenate([cos, cos, ones], axis=1)
    ts1 = jnp.concatenate([-sin, zh, zeros], axis=1)
    ts2 = jnp.concatenate([zh, sin, zeros], axis=1)
    dup = lambda a: jnp.concatenate([a, a], axis=1)
    return dup(tc), dup(ts1), dup(ts2)


def _pad_w_in(w):
    d = w.shape[0]
    main = w[:, :8 * _G]
    o = 8 * _G
    iq = w[:, o:o + 256]
    ik = w[:, o + 256:o + 320]
    iw = w[:, o + 320:o + 324]
    qm = w[:, o + 324:o + 324 + D_M]
    gm = w[:, o + 324 + D_M:o + 324 + 2 * D_M]
    iw_pad = jnp.concatenate([iw, jnp.zeros((d, LANES - IDX_HEADS), w.dtype)], axis=1)
    return jnp.concatenate([main, iq, ik, ik, iw_pad, qm, gm], axis=1).astype(BF16)


def kernel(x_prompt, x_sample, mem_prompt, cache_k_a, cache_v_a, cache_k_b, cache_v_b, cache_k_idx,
           cache_mem_k, cache_mem_v, page_table, norm_g, w_in, mem_norm_g, w_mem_kv, w_out, final_norm_g):
    depth = w_in.shape[0]
    b, t, d = x_prompt.shape
    nseq, ts, _ = x_sample.shape
    n_pages = page_table.shape[1]
    page_size = cache_k_a.shape[2]
    past = n_pages * page_size
    assert t % TILE == 0 and t // TILE <= LANES and page_size == LANES
    assert past % MOBA_BLOCK == 0 and ts <= LANES and ts % 8 == 0

    tabs_p = _rope_tables(jnp.arange(t, dtype=jnp.int32))
    pos_s = past + jnp.arange(ts, dtype=jnp.int32)
    tabs_s = tuple(jnp.tile(a, (nseq, 1)) for a in _rope_tables(pos_s))
    fg = final_norm_g.reshape(1, d)
    nbp = t // TILE

    xp, xs = x_prompt, x_sample
    outs = [[] for _ in range(12)]
    for l in range(depth):
        w_pad = _pad_w_in(w_in[l])
        g = norm_g[l].reshape(1, d)
        wo_bf = w_out[l].astype(BF16)
        last = l == depth - 1

        (qa, ka, va, ga, qb, kb, vb, gb, iq, ik, iw, qm, gm,
         ka_bf, va_t, kb_bf, vb_t, ik_bf, kmean) = _project(xp.reshape(b * t, d), g, w_pad, tabs_p, nbp, TILE, True)
        r3 = lambda a: a.reshape(b, t, a.shape[-1])
        r4 = lambda a: a.reshape(b, nbp, a.shape[-2], TILE)
        mk, mv = _mem_kv(mem_prompt, mem_norm_g[l].reshape(1, d), w_mem_kv[l].astype(BF16))
        kmean_pad = jnp.pad(kmean.reshape(b, nbp, D_A), ((0, 0), (0, LANES - nbp), (0, 0)))
        oa = _moba_prompt(r3(qa), r3(ka_bf), r4(va_t), kmean_pad)
        ob = _dsa_prompt(r3(qb), r3(iq).astype(BF16), r3(iw), r3(ik_bf), r3(kb_bf), r4(vb_t))
        xp = _merge(xp, oa, r3(ga), ob, r3(gb), r3(qm), r3(gm), mk, mv, wo_bf, fg, TILE, last)
        for lst, a, h in ((outs[0], ka, H_A), (outs[1], va, H_A), (outs[2], kb, H_B), (outs[3], vb, H_B)):
            lst.append(jnp.transpose(a.reshape(b, h, HEAD_DIM, t), (0, 3, 1, 2)))
        outs[4].append(jnp.transpose(ik, (0, 2, 1)))
        outs[5].append(mk.reshape(b, N_MEM, H_M, HEAD_DIM))
        outs[6].append(mv.reshape(b, N_MEM, H_M, HEAD_DIM))

        n_s = nseq * ts
        tm_s = n_s if n_s <= TILE else TILE
        (qa, ka, va, ga, qb, kb, vb, gb, iq, ik, iw, qm, gm,
         _, _, _, _, _, _) = _project(xs.reshape(n_s, d), g, w_pad, tabs_s, n_s // tm_s, tm_s, False)
        s3 = lambda a: a.reshape(nseq, ts, a.shape[-1])
        padn = lambda a: jnp.pad(s3(a), ((0, 0), (0, LANES - ts), (0, 0)))
        oa = _moba_decode(l, page_table, s3(qa), _cache_view(cache_k_a), _cache_view(cache_v_a), padn(ka), padn(va))
        iq_rows = s3(iq).reshape(nseq, ts, IDX_HEADS, IDX_DIM).transpose(0, 2, 1, 3).reshape(
            nseq, IDX_HEADS * ts, IDX_DIM)
        iw_rows = (s3(iw)[:, :, :IDX_HEADS] * (IDX_HEADS ** -0.5)).transpose(0, 2, 1).reshape(
            nseq, IDX_HEADS * ts, 1)
        iw_rows = jnp.broadcast_to(iw_rows, (nseq, IDX_HEADS * ts, LANES))
        bias = _idx_decode(l, page_table, iq_rows, iw_rows, jnp.transpose(cache_k_idx, (0, 1, 3, 2)), padn(ik), ts)
        ob = _dsa_decode(l, page_table, s3(qb), bias, _cache_view(cache_k_b), _cache_view(cache_v_b),
                         padn(kb), padn(vb))
        mks = cache_mem_k[l].reshape(nseq, N_MEM, D_M)
        mvs = cache_mem_v[l].reshape(nseq, N_MEM, D_M)
        xs = _merge(xs, oa, s3(ga), ob, s3(gb), s3(qm), s3(gm), mks, mvs, wo_bf, fg, ts, last)
        for lst, a, shp in ((outs[7], ka, (nseq, ts, H_A, HEAD_DIM)), (outs[8], va, (nseq, ts, H_A, HEAD_DIM)),
                            (outs[9], kb, (nseq, ts, H_B, HEAD_DIM)), (outs[10], vb, (nseq, ts, H_B, HEAD_DIM)),
                            (outs[11], ik, (nseq, ts, IDX_DIM))):
            lst.append(a.reshape(shp))

    stk = [jnp.stack(o) for o in outs]
    return (xp, xs, stk[0], stk[1], stk[2], stk[3], stk[4], stk[5], stk[6],
            stk[7], stk[8], stk[9], stk[10], stk[11])
```

```python
import functools

import jax
import jax.numpy as jnp
from jax import lax
from jax.experimental import pallas as pl
from jax.experimental.pallas import tpu as pltpu

HEAD_DIM = 64
H_A = 6
H_B = 6
H_M = 4
D_A = H_A * HEAD_DIM
D_B = H_B * HEAD_DIM
D_M = H_M * HEAD_DIM
N_MEM = 256
MOBA_BLOCK = 256
MOBA_TOPK = 3
IDX_HEADS = 4
IDX_DIM = 64
IDX_TOPK = 256
ROPE_THETA = 500000.0
ROT_DIM = HEAD_DIM // 4
EPS = 1e-6

LANES = 128
TILE = 256
NEG = -1e30
LOG2E = 1.4426950408889634
SUM_ROWS = 16
VT_ROWS = LANES + SUM_ROWS
INT_MIN = -(2 ** 31)
I16_MIN = -(2 ** 15)
VMEM_LIMIT = 60 * 1024 * 1024
IDX_PAGES_PER_STEP = 32
KV_PAGES_PER_STEP = 16

_G = D_A
OFF_QA, OFF_KA, OFF_VA, OFF_GA = 0, _G, 2 * _G, 3 * _G
OFF_QB, OFF_KB, OFF_VB, OFF_GB = 4 * _G, 5 * _G, 6 * _G, 7 * _G
OFF_IQ = 8 * _G
OFF_IK = OFF_IQ + 256
OFF_IW = OFF_IK + LANES
OFF_QM = OFF_IW + LANES
OFF_GM = OFF_QM + D_M
W_PAD = OFF_GM + D_M

F32 = jnp.float32
BF16 = jnp.bfloat16
NT = (((1,), (1,)), ((), ()))


def _cparams(*sem):
    return pltpu.CompilerParams(dimension_semantics=sem, vmem_limit_bytes=VMEM_LIMIT)


def _half_mask(hh):
    lane = lax.broadcasted_iota(jnp.int32, (1, LANES), 1)
    return (lane >= HEAD_DIM * hh) & (lane < HEAD_DIM * (hh + 1))


def _sortable(score):
    bits = lax.bitcast_convert_type(score, jnp.int32)
    key = bits ^ (lax.shift_right_arithmetic(bits, 31) & jnp.int32(0x7FFFFFFF))
    return jnp.where(score == 0.0, jnp.int32(0), key)


def _head_pair_queries(qc):
    scaled = qc * (HEAD_DIM ** -0.5 * LOG2E)
    return jnp.concatenate([jnp.where(_half_mask(0), scaled, 0.0), jnp.where(_half_mask(1), scaled, 0.0)],
                           axis=0).astype(BF16)


def _pair_output(acc):
    t = acc.shape[1] // 2
    o = acc[:LANES] / acc[LANES:LANES + 1]
    d = lax.broadcasted_iota(jnp.int32, (LANES, t), 0)
    return jnp.where(d < HEAD_DIM, o[:, :t], o[:, t:]).T


def _proj_kernel(x_ref, g_ref, w_ref, tc_ref, ts1_ref, ts2_ref,
                 qa_ref, ka_ref, va_ref, ga_ref, qb_ref, kb_ref, vb_ref, gb_ref,
                 iq_ref, ik_ref, iw_ref, qm_ref, gm_ref,
                 kabf_ref, vat_ref, kbbf_ref, vbt_ref, ikbf_ref, kmean_ref, *, kv_transposed):
    x = x_ref[...]
    ms = jnp.mean(x * x, axis=-1, keepdims=True)
    h = (x * lax.rsqrt(ms + EPS) * g_ref[...]).astype(BF16)
    tc, ts1, ts2 = tc_ref[...], ts1_ref[...], ts2_ref[...]

    def mm(off, width):
        return jnp.dot(h, w_ref[:, off:off + width], preferred_element_type=F32)

    def rope(u):
        outs = []
        for c in range(u.shape[1] // LANES):
            uc = u[:, c * LANES:(c + 1) * LANES]
            outs.append(uc * tc + pltpu.roll(uc, LANES - ROT_DIM // 2, 1) * ts1
                        + pltpu.roll(uc, ROT_DIM // 2, 1) * ts2)
        return outs[0] if len(outs) == 1 else jnp.concatenate(outs, axis=1)

    def put(ref, val, val_t=None):
        if kv_transposed:
            ref[0] = val.T if val_t is None else val_t
        else:
            ref[...] = val

    qa_ref[...] = rope(mm(OFF_QA, D_A))
    ka = rope(mm(OFF_KA, D_A))
    put(ka_ref, ka)
    kabf_ref[...] = ka.astype(BF16)
    kmean_ref[0] = jnp.mean(ka, axis=0, keepdims=True)
    va = mm(OFF_VA, D_A)
    va_t = va.T
    put(va_ref, va, va_t)
    vat_ref[0] = _vt_with_ones(va_t)
    ga_ref[...] = mm(OFF_GA, D_A)
    qb_ref[...] = rope(mm(OFF_QB, D_B))
    kb = rope(mm(OFF_KB, D_B))
    put(kb_ref, kb)
    kbbf_ref[...] = kb.astype(BF16)
    vb = mm(OFF_VB, D_B)
    vb_t = vb.T
    put(vb_ref, vb, vb_t)
    vbt_ref[0] = _vt_with_ones(vb_t)
    gb_ref[...] = mm(OFF_GB, D_B)
    iq_ref[...] = rope(mm(OFF_IQ, IDX_HEADS * IDX_DIM))
    ik2 = rope(mm(OFF_IK, LANES))
    put(ik_ref, ik2[:, :IDX_DIM], ik2.T[:IDX_DIM] if kv_transposed else None)
    ikbf_ref[...] = ik2.astype(BF16)
    iw_ref[...] = mm(OFF_IW, LANES)
    qm_ref[...] = mm(OFF_QM, D_M)
    gm_ref[...] = mm(OFF_GM, D_M)


def _project(x2d, g, w_pad, tabs, n_tab_tiles, tm, kv_transposed):
    n, d = x2d.shape
    nt = n // tm
    tok = lambda w: pl.BlockSpec((tm, w), lambda i: (i, 0))
    tab = pl.BlockSpec((tm, LANES), lambda i: (i % n_tab_tiles, 0))
    vt = lambda w: pl.BlockSpec((1, w // LANES * VT_ROWS, tm), lambda i: (i, 0, 0))
    vt_shape = lambda w: jax.ShapeDtypeStruct((nt, w // LANES * VT_ROWS, tm), BF16)
    widths = [D_A] * 4 + [D_B] * 4 + [IDX_HEADS * IDX_DIM, IDX_DIM, LANES, D_M, D_M]
    out_shape = [jax.ShapeDtypeStruct((n, w), F32) for w in widths]
    out_specs = [tok(w) for w in widths]
    if kv_transposed:
        for idx in (1, 2, 5, 6, 9):
            w = widths[idx]
            out_shape[idx] = jax.ShapeDtypeStruct((nt // n_tab_tiles, w, n_tab_tiles * tm), F32)
            out_specs[idx] = pl.BlockSpec((1, w, tm), lambda i: (i // n_tab_tiles, 0, i % n_tab_tiles))
    out_shape += [jax.ShapeDtypeStruct((n, D_A), BF16), vt_shape(D_A),
                  jax.ShapeDtypeStruct((n, D_B), BF16), vt_shape(D_B),
                  jax.ShapeDtypeStruct((n, LANES), BF16), jax.ShapeDtypeStruct((nt, 1, D_A), F32)]
    out_specs += [tok(D_A), vt(D_A), tok(D_B), vt(D_B), tok(LANES), pl.BlockSpec((1, 1, D_A), lambda i: (i, 0, 0))]
    return pl.pallas_call(
        functools.partial(_proj_kernel, kv_transposed=kv_transposed),
        grid=(nt,),
        in_specs=[tok(d),
                  pl.BlockSpec((1, d), lambda i: (0, 0)),
                  pl.BlockSpec((d, W_PAD), lambda i: (0, 0)),
                  tab, tab, tab],
        out_specs=out_specs,
        out_shape=out_shape,
        compiler_params=_cparams("parallel"),
        name="proj",
    )(x2d, g, w_pad, *tabs)


def _memkv_kernel(m_ref, g_ref, w_ref, mk_ref, mv_ref):
    x = m_ref[0]
    ms = jnp.mean(x * x, axis=-1, keepdims=True)
    h = (x * lax.rsqrt(ms + EPS) * g_ref[...]).astype(BF16)
    u = jnp.dot(h, w_ref[...], preferred_element_type=F32)
    mk_ref[0] = u[:, :D_M]
    mv_ref[0] = u[:, D_M:]


def _mem_kv(mem, g, w_bf):
    b, nm, d = mem.shape
    return pl.pallas_call(
        _memkv_kernel,
        grid=(b,),
        in_specs=[pl.BlockSpec((1, nm, d), lambda i: (i, 0, 0)),
                  pl.BlockSpec((1, d), lambda i: (0, 0)),
                  pl.BlockSpec((d, 2 * D_M), lambda i: (0, 0))],
        out_specs=[pl.BlockSpec((1, nm, D_M), lambda i: (i, 0, 0))] * 2,
        out_shape=[jax.ShapeDtypeStruct((b, nm, D_M), F32)] * 2,
        compiler_params=_cparams("parallel"),
        name="mem_kv",
    )(mem, g, w_bf)


def _flash_step(s, m, acc_ref, c, vt):
    m_new = jnp.maximum(m, jnp.max(s, axis=0, keepdims=True))
    alpha = jnp.exp2(m - m_new)
    p = jnp.exp2(s - m_new).astype(BF16)
    acc_ref[c] = alpha * acc_ref[c] + jnp.dot(vt, p, preferred_element_type=F32)
    return m_new


def _vt_with_ones(v_t):
    vt = v_t.astype(BF16)
    ones = jnp.ones((SUM_ROWS, v_t.shape[1]), BF16)
    parts = []
    for c in range(v_t.shape[0] // LANES):
        parts += [vt[c * LANES:(c + 1) * LANES], ones]
    return jnp.concatenate(parts, axis=0)


def _moba_kernel(q_ref, k_ref, vt_ref, km_ref, o_ref, qaug_scr, acc_scr, s_scr):
    qi = pl.program_id(1)
    n_pairs = D_A // LANES
    blk_id = lax.broadcasted_iota(jnp.int32, (TILE, LANES), 1)
    blk_f = blk_id.astype(F32)
    past = blk_id < qi
    for c in range(n_pairs):
        cs = slice(c * LANES, (c + 1) * LANES)
        qc = q_ref[0, :, cs]
        kmc = km_ref[0, :, cs]
        biases = []
        for hh in range(2):
            qm = jnp.where(_half_mask(hh), qc, 0.0)
            g = lax.dot_general(qm, kmc, NT, precision=lax.Precision.HIGHEST, preferred_element_type=F32)
            g = jnp.where(past, g, -jnp.inf)
            sel = jnp.zeros((TILE, LANES), jnp.bool_)
            for _ in range(MOBA_TOPK):
                m = jnp.max(g, axis=1, keepdims=True)
                first = jnp.min(jnp.where(g == m, blk_f, 1e9), axis=1, keepdims=True)
                pick = (blk_f == first) & past
                sel = sel | pick
                g = jnp.where(pick, -jnp.inf, g)
            biases.append(jnp.where(sel, 0.0, NEG).astype(BF16))
        qaug_scr[c] = jnp.concatenate([_head_pair_queries(qc), jnp.concatenate(biases, axis=0)], axis=1)

    own0 = pl.multiple_of(qi * TILE, TILE)
    ko = k_ref[0, pl.ds(own0, TILE), :]
    vto = vt_ref[0, qi]
    krow = lax.broadcasted_iota(jnp.int32, (TILE, 2 * TILE), 0)
    qcol = lax.broadcasted_iota(jnp.int32, (TILE, 2 * TILE), 1) % TILE
    ms = []
    for c in range(n_pairs):
        cs = slice(c * LANES, (c + 1) * LANES)
        s = lax.dot_general(ko[:, cs], qaug_scr[c, :, :LANES], NT, preferred_element_type=F32)
        s = jnp.where(krow <= qcol, s, NEG)
        m = jnp.max(s, axis=0, keepdims=True)
        p = jnp.exp2(s - m).astype(BF16)
        ms.append(m)
        acc_scr[c] = jnp.dot(vto[c * VT_ROWS:(c + 1) * VT_ROWS, :], p, preferred_element_type=F32)

    def logits(j, c):
        j0 = pl.multiple_of(j * TILE, TILE)
        onehot = jnp.where(blk_id == j, 1.0, 0.0).astype(BF16)
        k_aug = jnp.concatenate([k_ref[0, pl.ds(j0, TILE), c * LANES:(c + 1) * LANES], onehot], axis=1)
        return lax.dot_general(k_aug, qaug_scr[c], NT, preferred_element_type=F32)

    s_scr[0] = logits(0, 0)

    def body(j, ms):
        vtj = vt_ref[0, j]
        new_m = []
        for c in range(n_pairs):
            if c + 1 < n_pairs:
                s_scr[c + 1] = logits(j, c + 1)
            else:
                s_scr[0] = logits(jnp.minimum(j + 1, qi - 1), 0)
            new_m.append(_flash_step(s_scr[c], ms[c], acc_scr, c, vtj[c * VT_ROWS:(c + 1) * VT_ROWS, :]))
        return tuple(new_m)

    lax.fori_loop(0, qi, body, tuple(ms))
    o_ref[0] = jnp.concatenate([_pair_output(acc_scr[c]) for c in range(n_pairs)], axis=1)


def _moba_prompt(qa, ka_bf, va_t, kmean_pad):
    b, t, _ = qa.shape
    nt = t // TILE
    return pl.pallas_call(
        _moba_kernel,
        grid=(b, nt),
        in_specs=[pl.BlockSpec((1, TILE, D_A), lambda i, j: (i, j, 0)),
                  pl.BlockSpec((1, t, D_A), lambda i, j: (i, 0, 0)),
                  pl.BlockSpec((1, nt, va_t.shape[2], TILE), lambda i, j: (i, 0, 0, 0)),
                  pl.BlockSpec((1, LANES, D_A), lambda i, j: (i, 0, 0))],
        out_specs=pl.BlockSpec((1, TILE, D_A), lambda i, j: (i, j, 0)),
        out_shape=jax.ShapeDtypeStruct((b, t, D_A), F32),
        scratch_shapes=[pltpu.VMEM((D_A // LANES, 2 * TILE, 2 * LANES), BF16),
                        pltpu.VMEM((D_A // LANES, VT_ROWS, 2 * TILE), F32),
                        pltpu.VMEM((D_A // LANES, TILE, 2 * TILE), F32)],
        compiler_params=_cparams("parallel", "arbitrary"),
        name="moba_prompt",
    )(qa, ka_bf, va_t, kmean_pad)


def _dsa_kernel(q_ref, iq_ref, iw_ref, ik_ref, k_ref, vt_ref, o_ref,
                key_scr, hi_scr, lo_scr, cst_scr, acc_scr, s_scr, *, k_sel, idx_bits):
    qi = pl.program_id(1)
    n_tiles = qi + 1
    n_pairs = D_B // LANES
    kf = float(k_sel)
    krow = lax.broadcasted_iota(jnp.int32, (TILE, TILE), 0)
    q_pos = lax.broadcasted_iota(jnp.int32, (TILE, TILE), 1) + qi * TILE
    key_pos = lambda j: krow + j * TILE

    iq = iq_ref[0]
    iw_t = (iw_ref[0] * (IDX_HEADS ** -0.5)).T
    iqm = [(jnp.where(_half_mask(n % 2), iq[:, (n // 2) * LANES:(n // 2 + 1) * LANES], 0.0)
            * (IDX_DIM ** -0.5)).astype(BF16) for n in range(IDX_HEADS)]

    def score_body(j, carry):
        j0 = pl.multiple_of(j * TILE, TILE)
        ikj = ik_ref[0, pl.ds(j0, TILE), :]
        score = jnp.zeros((TILE, TILE), F32)
        for n in range(IDX_HEADS):
            rel = lax.dot_general(ikj, iqm[n], NT, preferred_element_type=F32)
            score = score + iw_t[n:n + 1, :] * jnp.maximum(rel, 0.0)
        key = jnp.where(key_pos(j) > q_pos, jnp.int32(INT_MIN), _sortable(score))
        key_scr[j] = key
        hi_scr[j] = lax.shift_right_arithmetic(key, 16).astype(jnp.int16)
        lo_scr[j] = ((key & 0xFFFF) + I16_MIN).astype(jnp.int16)
        return carry

    lax.fori_loop(0, n_tiles, score_body, 0)

    def count16(plane, pred):
        def body(j, acc):
            w = jnp.where(pred(plane[j]), jnp.int16(1), jnp.int16(0))
            for r in range(TILE // 64):
                acc = acc + w[r * 64:(r + 1) * 64]
            return acc
        acc = lax.fori_loop(0, n_tiles, body, jnp.zeros((64, TILE), jnp.int16))
        return jnp.sum(acc.astype(jnp.int32).astype(F32), axis=0, keepdims=True)

    def count32(pred):
        def body(j, acc):
            w = jnp.where(pred(key_scr[j], j), 1.0, 0.0)
            for r in range(TILE // 64):
                acc = acc + w[r * 64:(r + 1) * 64]
            return acc
        acc = lax.fori_loop(0, n_tiles, body, jnp.zeros((64, TILE), F32))
        return jnp.sum(acc, axis=0, keepdims=True)

    def search16(plane, target):
        zero = jnp.zeros((1, TILE), jnp.int32)
        c0 = count16(plane, lambda x: x >= zero.astype(jnp.int16))
        thr = jnp.where(c0 >= target, zero, jnp.int32(I16_MIN))

        def bit(i, thr):
            cand = thr | lax.shift_left(jnp.int32(1), 14 - i)
            c = count16(plane, lambda x: x >= cand.astype(jnp.int16))
            return jnp.where(c >= target, cand, thr)

        return lax.fori_loop(0, 15, bit, thr)

    thr_hi = search16(hi_scr, kf)
    thr_hi16 = thr_hi.astype(jnp.int16)
    above = count16(hi_scr, lambda x: x > thr_hi16)

    def mask_lo(j, carry):
        lo_scr[j] = jnp.where(hi_scr[j] == thr_hi16, lo_scr[j], jnp.int16(I16_MIN))
        return carry

    lax.fori_loop(0, n_tiles, mask_lo, 0)
    thr_lo = search16(lo_scr, kf - above)
    thr = thr_hi * 65536 + (thr_lo - I16_MIN)

    cnt_ge = count32(lambda k, j: k >= thr)
    need = kf - count32(lambda k, j: k > thr)
    cst_scr[...] = jnp.full(cst_scr.shape, jnp.int32(2 ** 30))

    @pl.when(jnp.max(cnt_ge) > kf)
    def _():
        def cbit(i, cpos):
            cand = cpos | lax.shift_left(jnp.int32(1), idx_bits - 1 - i)
            cnt = count32(lambda k, j: (k == thr) & (key_pos(j) < cand))
            return jnp.where(cnt < need, cand, cpos)
        cpos = lax.fori_loop(0, idx_bits, cbit, jnp.zeros((1, TILE), jnp.int32))
        cst_scr[...] = jnp.broadcast_to(cpos, cst_scr.shape)

    cstar = cst_scr[:1, :]

    def to_bias(j, carry):
        k = key_scr[j]
        kp = key_pos(j)
        keep = ((k > thr) | ((k == thr) & (kp <= cstar))) & (kp <= q_pos)
        key_scr[j] = lax.bitcast_convert_type(jnp.where(keep, 0.0, NEG).astype(F32), jnp.int32)
        return carry

    lax.fori_loop(0, n_tiles, to_bias, 0)

    q_pairs = [_head_pair_queries(q_ref[0, :, c * LANES:(c + 1) * LANES]) for c in range(n_pairs)]
    for c in range(n_pairs):
        acc_scr[c] = jnp.zeros((VT_ROWS, 2 * TILE), F32)

    def logits(j, c):
        j0 = pl.multiple_of(j * TILE, TILE)
        return lax.dot_general(k_ref[0, pl.ds(j0, TILE), c * LANES:(c + 1) * LANES], q_pairs[c], NT,
                               preferred_element_type=F32)

    s_scr[0] = logits(0, 0)

    def body(j, ms):
        vtj = vt_ref[0, j]
        bias = lax.bitcast_convert_type(key_scr[j], F32)
        bias2 = jnp.concatenate([bias, bias], axis=1)
        new_m = []
        for c in range(n_pairs):
            if c + 1 < n_pairs:
                s_scr[c + 1] = logits(j, c + 1)
            else:
                s_scr[0] = logits(jnp.minimum(j + 1, n_tiles - 1), 0)
            new_m.append(_flash_step(s_scr[c] + bias2, ms[c], acc_scr, c, vtj[c * VT_ROWS:(c + 1) * VT_ROWS, :]))
        return tuple(new_m)

    lax.fori_loop(0, n_tiles, body, tuple(jnp.full((1, 2 * TILE), NEG, F32) for _ in range(n_pairs)))
    o_ref[0] = jnp.concatenate([_pair_output(acc_scr[c]) for c in range(n_pairs)], axis=1)


def _dsa_prompt(qb, iq, iw, ik_bf, kb_bf, vb_t):
    b, t, _ = qb.shape
    nt = t // TILE
    k_sel = min(IDX_TOPK, t // 4)
    kern = functools.partial(_dsa_kernel, k_sel=k_sel, idx_bits=max(1, (t - 1).bit_length()))
    res = lambda w: pl.BlockSpec((1, t, w), lambda i, j: (i, 0, 0))
    til = lambda w: pl.BlockSpec((1, TILE, w), lambda i, j: (i, j, 0))
    return pl.pallas_call(
        kern,
        grid=(b, nt),
        in_specs=[til(D_B), til(IDX_HEADS * IDX_DIM), til(LANES), res(LANES), res(D_B),
                  pl.BlockSpec((1, nt, vb_t.shape[2], TILE), lambda i, j: (i, 0, 0, 0))],
        out_specs=til(D_B),
        out_shape=jax.ShapeDtypeStruct((b, t, D_B), F32),
        scratch_shapes=[pltpu.VMEM((nt, TILE, TILE), jnp.int32),
                        pltpu.VMEM((nt, TILE, TILE), jnp.int16),
                        pltpu.VMEM((nt, TILE, TILE), jnp.int16),
                        pltpu.VMEM((8, TILE), jnp.int32),
                        pltpu.VMEM((D_B // LANES, VT_ROWS, 2 * TILE), F32),
                        pltpu.VMEM((D_B // LANES, TILE, 2 * TILE), F32)],
        compiler_params=_cparams("parallel", "arbitrary"),
        name="dsa_prompt",
    )(qb, iq, iw, ik_bf, kb_bf, vb_t)


def _merge_kernel(x_ref, oa_ref, ga_ref, ob_ref, gb_ref, qm_ref, gm_ref, mk_ref, mv_ref, wo_ref, fg_ref, y_ref,
                  *, final):
    lane = lax.broadcasted_iota(jnp.int32, (1, LANES), 1)
    mk = mk_ref[0].astype(BF16)
    mv = mv_ref[0].astype(BF16)
    oms = []
    for c in range(D_M // LANES):
        cs = slice(c * LANES, (c + 1) * LANES)
        qc = qm_ref[0, :, cs]
        heads = []
        for hh in range(2):
            q_bf = (jnp.where(_half_mask(hh), qc, 0.0) * (HEAD_DIM ** -0.5)).astype(BF16)
            s = lax.dot_general(q_bf, mk[:, cs], NT, preferred_element_type=F32)
            m = jnp.max(s, axis=1, keepdims=True)
            p = jnp.exp(s - m)
            l = jnp.sum(p, axis=1, keepdims=True)
            heads.append(jnp.dot(p.astype(BF16), mv[:, cs], preferred_element_type=F32) / l)
        oms.append(jnp.where(lane < HEAD_DIM, heads[0], heads[1]))
    om = jnp.concatenate(oms, axis=1)
    silu = lambda g: g / (1.0 + jnp.exp(-g))
    ymix = jnp.concatenate([oa_ref[0] * silu(ga_ref[0]), ob_ref[0] * silu(gb_ref[0]), om * silu(gm_ref[0])],
                           axis=1).astype(BF16)
    z = x_ref[0] + jnp.dot(ymix, wo_ref[...], preferred_element_type=F32)
    if final:
        ms = jnp.mean(z * z, axis=-1, keepdims=True)
        z = z * lax.rsqrt(ms + EPS) * fg_ref[...]
    y_ref[0] = z


def _merge(x, oa, ga, ob, gb, qm, gm, mk, mv, wo_bf, fg, tm, final):
    b, t, d = x.shape
    til = lambda w: pl.BlockSpec((1, tm, w), lambda i, j: (i, j, 0))
    memspec = pl.BlockSpec((1, N_MEM, D_M), lambda i, j: (i, 0, 0))
    return pl.pallas_call(
        functools.partial(_merge_kernel, final=final),
        grid=(b, t // tm),
        in_specs=[til(d), til(D_A), til(D_A), til(D_B), til(D_B), til(D_M), til(D_M), memspec, memspec,
                  pl.BlockSpec((d, d), lambda i, j: (0, 0)),
                  pl.BlockSpec((1, d), lambda i, j: (0, 0))],
        out_specs=til(d),
        out_shape=jax.ShapeDtypeStruct((b, t, d), F32),
        compiler_params=_cparams("parallel", "parallel"),
        name="merge",
    )(x, oa, ga, ob, gb, qm, gm, mk, mv, wo_bf, fg)


def _expand_heads(q, n_heads):
    t, w = q.shape
    qt = jnp.concatenate([q] * n_heads, axis=0)
    r = lax.broadcasted_iota(jnp.int32, (n_heads * t, w), 0)
    l = lax.broadcasted_iota(jnp.int32, (n_heads * t, w), 1)
    keep = (l >= (r // t) * HEAD_DIM) & (l < (r // t + 1) * HEAD_DIM)
    return jnp.where(keep, qt, 0.0)


def _collapse_heads(o, n_heads, t):
    w = o.shape[1]
    l = lax.broadcasted_iota(jnp.int32, (t, w), 1)
    out = jnp.zeros((t, w), F32)
    for h in range(n_heads):
        keep = (l >= h * HEAD_DIM) & (l < (h + 1) * HEAD_DIM)
        out = out + jnp.where(keep, o[h * t:(h + 1) * t, :], 0.0)
    return out


def _pages_t(refs):
    pages = [r[0, 0].reshape(r.shape[2] * r.shape[3], r.shape[4]) for r in refs]
    return pages[0] if len(pages) == 1 else jnp.concatenate(pages, axis=1)


def _cache_view(cache):
    return jnp.transpose(cache, (0, 1, 3, 4, 2))


def _values_with_ones(vt):
    return jnp.concatenate([vt.astype(BF16), jnp.ones((SUM_ROWS, vt.shape[1]), BF16)], axis=0)


def _softmax_pv(s_scr, vt_scr, s_new, v_new):
    nsteps = s_scr.shape[0]
    d = v_new.shape[1]
    m = jnp.max(jnp.max(s_scr[...], axis=0), axis=1, keepdims=True)
    m = jnp.maximum(m, jnp.max(s_new, axis=1, keepdims=True))
    p_new = jnp.exp(s_new - m)
    l_new = jnp.sum(p_new, axis=1, keepdims=True)
    acc_new = jnp.dot(p_new.astype(BF16), v_new, preferred_element_type=F32)

    def body(i, acc):
        p = jnp.exp(s_scr[i] - m).astype(BF16)
        return acc + lax.dot_general(p, vt_scr[i], NT, preferred_element_type=F32)

    acc = lax.fori_loop(0, nsteps, body, jnp.zeros((s_new.shape[0], d + SUM_ROWS), F32),
                        unroll=nsteps if nsteps <= 8 else 8)
    return (acc[:, :d] + acc_new) / (acc[:, d:d + 1] + l_new)


def _paged_specs(layer, block_tail, per_step):
    zeros = (0,) * len(block_tail)
    return [pl.BlockSpec((1, 1) + block_tail, lambda s, j, pt, i=i: (layer, pt[s, j * per_step + i]) + zeros)
            for i in range(per_step)]


def _moba_dec_kernel(pt_ref, q_ref, *rest, ppb, bps):
    n_pg = ppb * bps
    k_refs, v_refs = rest[:n_pg], rest[n_pg:2 * n_pg]
    kn_ref, vn_ref, o_ref, s_scr, vt_scr, g_scr, qxt_scr = rest[2 * n_pg:]
    j = pl.program_id(1)
    nsteps = pl.num_programs(1)
    t = q_ref.shape[1]
    rows = H_A * t
    blk_w = s_scr.shape[2] // bps
    qx = _expand_heads(q_ref[0], H_A)
    qx_bf = (qx * (HEAD_DIM ** -0.5)).astype(BF16)

    @pl.when(j == 0)
    def _():
        qxt_scr[...] = jnp.concatenate([qx, jnp.zeros((LANES - rows, D_A), F32)], axis=0).T

    kt = _pages_t(k_refs)
    s_scr[j] = jnp.dot(qx_bf, kt.astype(BF16), preferred_element_type=F32)
    vt_scr[j] = _values_with_ones(_pages_t(v_refs))
    for b in range(bps):
        kmean = jnp.mean(kt[:, b * blk_w:(b + 1) * blk_w], axis=1, keepdims=True)
        g_scr[pl.ds(j * bps + b, 1), :] = jnp.sum(qxt_scr[...] * kmean, axis=0, keepdims=True)

    @pl.when(j == nsteps - 1)
    def _():
        npad = kn_ref.shape[1]
        r = lax.broadcasted_iota(jnp.int32, (rows, npad), 0)
        cidx = lax.broadcasted_iota(jnp.int32, (rows, npad), 1)
        s_new = lax.dot_general(qx_bf, kn_ref[0].astype(BF16), NT, preferred_element_type=F32)
        s_new = jnp.where(cidx <= (r % t), s_new, NEG)

        n_blocks = g_scr.shape[0]
        g = g_scr[...]
        bidx = lax.broadcasted_iota(jnp.int32, g.shape, 0).astype(F32)
        sel = jnp.zeros(g.shape, jnp.bool_)
        for _ in range(min(MOBA_TOPK, n_blocks)):
            mx = jnp.max(g, axis=0, keepdims=True)
            first = jnp.min(jnp.where(g == mx, bidx, 1e9), axis=0, keepdims=True)
            pick = bidx == first
            sel = sel | pick
            g = jnp.where(pick, -jnp.inf, g)
        bias_t = jnp.where(sel, 0.0, NEG).T
        for i in range(s_scr.shape[0]):
            cols = [jnp.broadcast_to(bias_t[:rows, i * bps + b:i * bps + b + 1], (rows, blk_w)) for b in range(bps)]
            s_scr[i] = s_scr[i] + (cols[0] if bps == 1 else jnp.concatenate(cols, axis=1))
        o_ref[0] = _collapse_heads(_softmax_pv(s_scr, vt_scr, s_new, vn_ref[0].astype(BF16)), H_A, t)


def _moba_decode(layer, page_table, qa, cache_kt, cache_vt, kn_pad, vn_pad):
    nseq, t, _ = qa.shape
    n_pages = page_table.shape[1]
    page_size = cache_kt.shape[4]
    ppb = MOBA_BLOCK // page_size
    nb = n_pages // ppb
    bps = max(1, KV_PAGES_PER_STEP // ppb)
    while nb % bps:
        bps //= 2
    rows = H_A * t
    seq = lambda w, r: pl.BlockSpec((1, r, w), lambda s, j, pt: (s, 0, 0))
    kspecs = _paged_specs(layer, cache_kt.shape[2:], ppb * bps)
    grid_spec = pltpu.PrefetchScalarGridSpec(
        num_scalar_prefetch=1,
        grid=(nseq, nb // bps),
        in_specs=[seq(D_A, t)] + kspecs + kspecs + [seq(D_A, kn_pad.shape[1])] * 2,
        out_specs=seq(D_A, t),
        scratch_shapes=[pltpu.VMEM((nb // bps, rows, bps * MOBA_BLOCK), F32),
                        pltpu.VMEM((nb // bps, D_A + SUM_ROWS, bps * MOBA_BLOCK), BF16),
                        pltpu.VMEM((nb, LANES), F32),
                        pltpu.VMEM((D_A, LANES), F32)],
    )
    assert rows <= LANES
    return pl.pallas_call(
        functools.partial(_moba_dec_kernel, ppb=ppb, bps=bps),
        grid_spec=grid_spec,
        out_shape=jax.ShapeDtypeStruct((nseq, t, D_A), F32),
        compiler_params=_cparams("parallel", "arbitrary"),
        name="moba_decode",
    )(page_table, qa, *([cache_kt] * (ppb * bps)), *([cache_vt] * (ppb * bps)), kn_pad, vn_pad)


def _idx_dec_kernel(pt_ref, iq_ref, iw_ref, *rest, per_step, k_sel, idx_bits, past, t):
    ik_refs = rest[:per_step]
    ikn_ref, bias_ref, key_scr = rest[per_step:]
    j = pl.program_id(1)
    nsteps = pl.num_programs(1)
    width = key_scr.shape[2]
    iq_bf = iq_ref[0].astype(BF16)
    iw = iw_ref[0][:, :1]

    def scores(rel):
        wrel = iw * jnp.maximum(rel * (IDX_DIM ** -0.5), 0.0)
        sc = wrel[0:t]
        for n in range(1, IDX_HEADS):
            sc = sc + wrel[n * t:(n + 1) * t]
        return sc

    ik_t = jnp.concatenate([r[0, 0] for r in ik_refs], axis=1)
    key_scr[j] = _sortable(scores(jnp.dot(iq_bf, ik_t.astype(BF16), preferred_element_type=F32)))

    @pl.when(j == nsteps - 1)
    def _():
        kf = float(k_sel)
        npad = ikn_ref.shape[1]
        r_new = lax.broadcasted_iota(jnp.int32, (t, npad), 0)
        c_new = lax.broadcasted_iota(jnp.int32, (t, npad), 1)
        rel_new = lax.dot_general(iq_bf, ikn_ref[0].astype(BF16), NT, preferred_element_type=F32)
        key_new = jnp.where(c_new <= r_new, _sortable(scores(rel_new)), jnp.int32(INT_MIN))
        tail = jnp.full((t, width - npad), jnp.int32(INT_MIN))
        key_scr[nsteps] = jnp.concatenate([key_new, tail], axis=1)

        keys = key_scr[...]
        pos = (lax.broadcasted_iota(jnp.int32, keys.shape, 0) * width
               + lax.broadcasted_iota(jnp.int32, keys.shape, 2))

        def count(pred):
            w = jnp.sum(jnp.where(pred, 1.0, 0.0), axis=0)
            acc = w[:, :LANES]
            for s in range(1, width // LANES):
                acc = acc + w[:, s * LANES:(s + 1) * LANES]
            return jnp.sum(acc, axis=1, keepdims=True)[None]

        zero = jnp.zeros((1, t, 1), jnp.int32)
        thr = jnp.where(count(keys >= zero) >= kf, zero, jnp.int32(INT_MIN))

        def bit(i, thr):
            cand = thr | lax.shift_left(jnp.int32(1), 30 - i)
            return jnp.where(count(keys >= cand) >= kf, cand, thr)

        thr = lax.fori_loop(0, 31, bit, thr)
        need = kf - count(keys > thr)

        def cbit(i, cpos):
            cand = cpos | lax.shift_left(jnp.int32(1), idx_bits - 1 - i)
            return jnp.where(count((keys == thr) & (pos < cand)) < need, cand, cpos)

        cstar = lax.fori_loop(0, idx_bits, cbit, zero)
        keep = (keys > thr) | ((keys == thr) & (pos <= cstar))
        bias_ref[0] = jnp.where(keep, 0.0, NEG).astype(F32)


def _idx_decode(layer, page_table, iq_rows, iw_rows, cache_ik, ikn_pad, t):
    nseq = iq_rows.shape[0]
    n_pages = page_table.shape[1]
    page_size = cache_ik.shape[3]
    per_step = IDX_PAGES_PER_STEP
    while n_pages % per_step:
        per_step //= 2
    nsteps = n_pages // per_step
    width = per_step * page_size
    past = n_pages * page_size
    k_sel = min(IDX_TOPK, (past + t) // 4)
    idx_bits = max(1, ((nsteps + 1) * width - 1).bit_length())
    seq = lambda r, w: pl.BlockSpec((1, r, w), lambda s, j, pt: (s, 0, 0))
    grid_spec = pltpu.PrefetchScalarGridSpec(
        num_scalar_prefetch=1,
        grid=(nseq, nsteps),
        in_specs=[seq(IDX_HEADS * t, IDX_DIM), seq(IDX_HEADS * t, LANES)]
        + _paged_specs(layer, cache_ik.shape[2:], per_step) + [seq(ikn_pad.shape[1], IDX_DIM)],
        out_specs=pl.BlockSpec((1, nsteps + 1, t, width), lambda s, j, pt: (s, 0, 0, 0)),
        scratch_shapes=[pltpu.VMEM((nsteps + 1, t, width), jnp.int32)],
    )
    kern = functools.partial(_idx_dec_kernel, per_step=per_step, k_sel=k_sel, idx_bits=idx_bits, past=past, t=t)
    return pl.pallas_call(
        kern,
        grid_spec=grid_spec,
        out_shape=jax.ShapeDtypeStruct((nseq, nsteps + 1, t, width), F32),
        compiler_params=_cparams("parallel", "arbitrary"),
        name="idx_decode",
    )(page_table, iq_rows, iw_rows, *([cache_ik] * per_step), ikn_pad)


def _dsa_dec_kernel(pt_ref, q_ref, b_ref, bn_ref, *rest, pps):
    k_refs, v_refs = rest[:pps], rest[pps:2 * pps]
    kn_ref, vn_ref, o_ref, s_scr, vt_scr = rest[2 * pps:]
    j = pl.program_id(1)
    nsteps = pl.num_programs(1)
    t = q_ref.shape[1]
    qx_bf = (_expand_heads(q_ref[0], H_B) * (HEAD_DIM ** -0.5)).astype(BF16)

    kt = _pages_t(k_refs).astype(BF16)
    s_scr[j] = jnp.dot(qx_bf, kt, preferred_element_type=F32) + jnp.concatenate([b_ref[0, 0]] * H_B, axis=0)
    vt_scr[j] = _values_with_ones(_pages_t(v_refs))

    @pl.when(j == nsteps - 1)
    def _():
        s_new = lax.dot_general(qx_bf, kn_ref[0].astype(BF16), NT, preferred_element_type=F32)
        s_new = s_new + jnp.concatenate([bn_ref[0, 0]] * H_B, axis=0)
        o_ref[0] = _collapse_heads(_softmax_pv(s_scr, vt_scr, s_new, vn_ref[0].astype(BF16)), H_B, t)


def _dsa_decode(layer, page_table, qb, bias, cache_kt, cache_vt, kn_pad, vn_pad):
    nseq, t, _ = qb.shape
    n_pages = page_table.shape[1]
    page_size = cache_kt.shape[4]
    pps = KV_PAGES_PER_STEP
    while n_pages % pps or bias.shape[3] % (pps * page_size):
        pps //= 2
    width = bias.shape[3]
    chunk = pps * page_size
    assert n_pages % pps == 0 and width % chunk == 0
    per_w = width // chunk
    rows = H_B * t
    npad = kn_pad.shape[1]
    seq = lambda w, r: pl.BlockSpec((1, r, w), lambda s, j, pt: (s, 0, 0))
    bspec = pl.BlockSpec((1, 1, t, chunk), lambda s, j, pt: (s, j // per_w, 0, j % per_w))
    bnew = pl.BlockSpec((1, 1, t, npad), lambda s, j, pt: (s, bias.shape[1] - 1, 0, 0))
    kspecs = _paged_specs(layer, cache_kt.shape[2:], pps)
    grid_spec = pltpu.PrefetchScalarGridSpec(
        num_scalar_prefetch=1,
        grid=(nseq, n_pages // pps),
        in_specs=[seq(D_B, t), bspec, bnew] + kspecs + kspecs + [seq(D_B, npad)] * 2,
        out_specs=seq(D_B, t),
        scratch_shapes=[pltpu.VMEM((n_pages // pps, rows, chunk), F32),
                        pltpu.VMEM((n_pages // pps, D_B + SUM_ROWS, chunk), BF16)],
    )
    return pl.pallas_call(
        functools.partial(_dsa_dec_kernel, pps=pps),
        grid_spec=grid_spec,
        out_shape=jax.ShapeDtypeStruct((nseq, t, D_B), F32),
        compiler_params=_cparams("parallel", "arbitrary"),
        name="dsa_decode",
    )(page_table, qb, bias, bias, *([cache_kt] * pps), *([cache_vt] * pps), kn_pad, vn_pad)


def _rope_tables(pos):
    half = ROT_DIM // 2
    inv = jnp.power(ROPE_THETA, -jnp.arange(half, dtype=F32) / half)
    ang = pos.astype(F32)[:, None] * inv[None, :]
    cos, sin = jnp.cos(ang), jnp.sin(ang)
    n = pos.shape[0]
    ones = jnp.ones((n, HEAD_DIM - ROT_DIM), F32)
    zeros = jnp.zeros((n, HEAD_DIM - ROT_DIM), F32)
    zh = jnp.zeros((n, half), F32)
    tc = jnp.concatenate([cos, cos, ones], axis=1)
    ts1 = jnp.concatenate([-sin, zh, zeros], axis=1)
    ts2 = jnp.concatenate([zh, sin, zeros], axis=1)
    dup = lambda a: jnp.concatenate([a, a], axis=1)
    return dup(tc), dup(ts1), dup(ts2)


def _pad_w_in(w):
    d = w.shape[0]
    main = w[:, :8 * _G]
    o = 8 * _G
    iq = w[:, o:o + 256]
    ik = w[:, o + 256:o + 320]
    iw = w[:, o + 320:o + 324]
    qm = w[:, o + 324:o + 324 + D_M]
    gm = w[:, o + 324 + D_M:o + 324 + 2 * D_M]
    iw_pad = jnp.concatenate([iw, jnp.zeros((d, LANES - IDX_HEADS), w.dtype)], axis=1)
    return jnp.concatenate([main, iq, ik, ik, iw_pad, qm, gm], axis=1).astype(BF16)


def kernel(x_prompt, x_sample, mem_prompt, cache_k_a, cache_v_a, cache_k_b, cache_v_b, cache_k_idx,
           cache_mem_k, cache_mem_v, page_table, norm_g, w_in, mem_norm_g, w_mem_kv, w_out, final_norm_g):
    depth = w_in.shape[0]
    b, t, d = x_prompt.shape
    nseq, ts, _ = x_sample.shape
    n_pages = page_table.shape[1]
    page_size = cache_k_a.shape[2]
    past = n_pages * page_size
    assert t % TILE == 0 and t // TILE <= LANES and page_size == LANES
    assert past % MOBA_BLOCK == 0 and ts <= LANES and ts % 8 == 0

    tabs_p = _rope_tables(jnp.arange(t, dtype=jnp.int32))
    pos_s = past + jnp.arange(ts, dtype=jnp.int32)
    tabs_s = tuple(jnp.tile(a, (nseq, 1)) for a in _rope_tables(pos_s))
    fg = final_norm_g.reshape(1, d)
    nbp = t // TILE

    xp, xs = x_prompt, x_sample
    outs = [[] for _ in range(12)]
    for l in range(depth):
        w_pad = _pad_w_in(w_in[l])
        g = norm_g[l].reshape(1, d)
        wo_bf = w_out[l].astype(BF16)
        last = l == depth - 1

        (qa, ka, va, ga, qb, kb, vb, gb, iq, ik, iw, qm, gm,
         ka_bf, va_t, kb_bf, vb_t, ik_bf, kmean) = _project(xp.reshape(b * t, d), g, w_pad, tabs_p, nbp, TILE, True)
        r3 = lambda a: a.reshape(b, t, a.shape[-1])
        r4 = lambda a: a.reshape(b, nbp, a.shape[-2], TILE)
        mk, mv = _mem_kv(mem_prompt, mem_norm_g[l].reshape(1, d), w_mem_kv[l].astype(BF16))
        kmean_pad = jnp.pad(kmean.reshape(b, nbp, D_A), ((0, 0), (0, LANES - nbp), (0, 0)))
        oa = _moba_prompt(r3(qa), r3(ka_bf), r4(va_t), kmean_pad)
        ob = _dsa_prompt(r3(qb), r3(iq).astype(BF16), r3(iw), r3(ik_bf), r3(kb_bf), r4(vb_t))
        xp = _merge(xp, oa, r3(ga), ob, r3(gb), r3(qm), r3(gm), mk, mv, wo_bf, fg, TILE, last)
        for lst, a, h in ((outs[0], ka, H_A), (outs[1], va, H_A), (outs[2], kb, H_B), (outs[3], vb, H_B)):
            lst.append(jnp.transpose(a.reshape(b, h, HEAD_DIM, t), (0, 3, 1, 2)))
        outs[4].append(jnp.transpose(ik, (0, 2, 1)))
        outs[5].append(mk.reshape(b, N_MEM, H_M, HEAD_DIM))
        outs[6].append(mv.reshape(b, N_MEM, H_M, HEAD_DIM))

        n_s = nseq * ts
        tm_s = n_s if n_s <= TILE else TILE
        (qa, ka, va, ga, qb, kb, vb, gb, iq, ik, iw, qm, gm,
         _, _, _, _, _, _) = _project(xs.reshape(n_s, d), g, w_pad, tabs_s, n_s // tm_s, tm_s, False)
        s3 = lambda a: a.reshape(nseq, ts, a.shape[-1])
        padn = lambda a: jnp.pad(s3(a), ((0, 0), (0, LANES - ts), (0, 0)))
        oa = _moba_decode(l, page_table, s3(qa), _cache_view(cache_k_a), _cache_view(cache_v_a), padn(ka), padn(va))
        iq_rows = s3(iq).reshape(nseq, ts, IDX_HEADS, IDX_DIM).transpose(0, 2, 1, 3).reshape(
            nseq, IDX_HEADS * ts, IDX_DIM)
        iw_rows = (s3(iw)[:, :, :IDX_HEADS] * (IDX_HEADS ** -0.5)).transpose(0, 2, 1).reshape(
            nseq, IDX_HEADS * ts, 1)
        iw_rows = jnp.broadcast_to(iw_rows, (nseq, IDX_HEADS * ts, LANES))
        bias = _idx_decode(l, page_table, iq_rows, iw_rows, jnp.transpose(cache_k_idx, (0, 1, 3, 2)), padn(ik), ts)
        ob = _dsa_decode(l, page_table, s3(qb), bias, _cache_view(cache_k_b), _cache_view(cache_v_b),
                         padn(kb), padn(vb))
        mks = cache_mem_k[l].reshape(nseq, N_MEM, D_M)
        mvs = cache_mem_v[l].reshape(nseq, N_MEM, D_M)
        xs = _merge(xs, oa, s3(ga), ob, s3(gb), s3(qm), s3(gm), mks, mvs, wo_bf, fg, ts, last)
        for lst, a, shp in ((outs[7], ka, (nseq, ts, H_A, HEAD_DIM)), (outs[8], va, (nseq, ts, H_A, HEAD_DIM)),
                            (outs[9], kb, (nseq, ts, H_B, HEAD_DIM)), (outs[10], vb, (nseq, ts, H_B, HEAD_DIM)),
                            (outs[11], ik, (nseq, ts, IDX_DIM))):
            lst.append(a.reshape(shp))

    stk = [jnp.stack(o) for o in outs]
    return (xp, xs, stk[0], stk[1], stk[2], stk[3], stk[4], stk[5], stk[6],
            stk[7], stk[8], stk[9], stk[10], stk[11])
```

```python
import functools

import jax
import jax.numpy as jnp
from jax import lax
from jax.experimental import pallas as pl
from jax.experimental.pallas import tpu as pltpu

HEAD_DIM = 64
H_A = 6
H_B = 6
H_M = 4
D_A = H_A * HEAD_DIM
D_B = H_B * HEAD_DIM
D_M = H_M * HEAD_DIM
N_MEM = 256
MOBA_BLOCK = 256
MOBA_TOPK = 3
IDX_HEADS = 4
IDX_DIM = 64
IDX_TOPK = 256
ROPE_THETA = 500000.0
ROT_DIM = HEAD_DIM // 4
EPS = 1e-6

LANES = 128
TILE = 256
NEG = -1e30
LOG2E = 1.4426950408889634
SUM_ROWS = 16
VT_ROWS = LANES + SUM_ROWS
INT_MIN = -(2 ** 31)
I16_MIN = -(2 ** 15)
VMEM_LIMIT = 60 * 1024 * 1024
IDX_PAGES_PER_STEP = 32
KV_PAGES_PER_STEP = 16

_G = D_A
OFF_QA, OFF_KA, OFF_VA, OFF_GA = 0, _G, 2 * _G, 3 * _G
OFF_QB, OFF_KB, OFF_VB, OFF_GB = 4 * _G, 5 * _G, 6 * _G, 7 * _G
OFF_IQ = 8 * _G
OFF_IK = OFF_IQ + 256
OFF_IW = OFF_IK + LANES
OFF_QM = OFF_IW + LANES
OFF_GM = OFF_QM + D_M
W_PAD = OFF_GM + D_M

F32 = jnp.float32
BF16 = jnp.bfloat16
NT = (((1,), (1,)), ((), ()))


def _cparams(*sem):
    return pltpu.CompilerParams(dimension_semantics=sem, vmem_limit_bytes=VMEM_LIMIT)


def _half_mask(hh):
    lane = lax.broadcasted_iota(jnp.int32, (1, LANES), 1)
    return (lane >= HEAD_DIM * hh) & (lane < HEAD_DIM * (hh + 1))


def _sortable(score):
    bits = lax.bitcast_convert_type(score, jnp.int32)
    key = bits ^ (lax.shift_right_arithmetic(bits, 31) & jnp.int32(0x7FFFFFFF))
    return jnp.where(score == 0.0, jnp.int32(0), key)


def _head_pair_queries(qc):
    scaled = qc * (HEAD_DIM ** -0.5 * LOG2E)
    return jnp.concatenate([jnp.where(_half_mask(0), scaled, 0.0), jnp.where(_half_mask(1), scaled, 0.0)],
                           axis=0).astype(BF16)


def _pair_output(acc):
    t = acc.shape[1] // 2
    o = acc[:LANES] / acc[LANES:LANES + 1]
    d = lax.broadcasted_iota(jnp.int32, (LANES, t), 0)
    return jnp.where(d < HEAD_DIM, o[:, :t], o[:, t:]).T


def _proj_kernel(x_ref, g_ref, w_ref, tc_ref, ts1_ref, ts2_ref,
                 qa_ref, ka_ref, va_ref, ga_ref, qb_ref, kb_ref, vb_ref, gb_ref,
                 iq_ref, ik_ref, iw_ref, qm_ref, gm_ref,
                 kabf_ref, vat_ref, kbbf_ref, vbt_ref, ikbf_ref, kmean_ref, *, kv_transposed):
    x = x_ref[...]
    ms = jnp.mean(x * x, axis=-1, keepdims=True)
    h = (x * lax.rsqrt(ms + EPS) * g_ref[...]).astype(BF16)
    tc, ts1, ts2 = tc_ref[...], ts1_ref[...], ts2_ref[...]

    def mm(off, width):
        return jnp.dot(h, w_ref[:, off:off + width], preferred_element_type=F32)

    def rope(u):
        outs = []
        for c in range(u.shape[1] // LANES):
            uc = u[:, c * LANES:(c + 1) * LANES]
            outs.append(uc * tc + pltpu.roll(uc, LANES - ROT_DIM // 2, 1) * ts1
                        + pltpu.roll(uc, ROT_DIM // 2, 1) * ts2)
        return outs[0] if len(outs) == 1 else jnp.concatenate(outs, axis=1)

    def put(ref, val, val_t=None):
        if kv_transposed:
            ref[0] = val.T if val_t is None else val_t
        else:
            ref[...] = val

    qa_ref[...] = rope(mm(OFF_QA, D_A))
    ka = rope(mm(OFF_KA, D_A))
    put(ka_ref, ka)
    kabf_ref[...] = ka.astype(BF16)
    kmean_ref[0] = jnp.mean(ka, axis=0, keepdims=True)
    va = mm(OFF_VA, D_A)
    va_t = va.T
    put(va_ref, va, va_t)
    vat_ref[0] = _vt_with_ones(va_t)
    ga_ref[...] = mm(OFF_GA, D_A)
    qb_ref[...] = rope(mm(OFF_QB, D_B))
    kb = rope(mm(OFF_KB, D_B))
    put(kb_ref, kb)
    kbbf_ref[...] = kb.astype(BF16)
    vb = mm(OFF_VB, D_B)
    vb_t = vb.T
    put(vb_ref, vb, vb_t)
    vbt_ref[0] = _vt_with_ones(vb_t)
    gb_ref[...] = mm(OFF_GB, D_B)
    iq_ref[...] = rope(mm(OFF_IQ, IDX_HEADS * IDX_DIM))
    ik2 = rope(mm(OFF_IK, LANES))
    put(ik_ref, ik2[:, :IDX_DIM], ik2.T[:IDX_DIM] if kv_transposed else None)
    ikbf_ref[...] = ik2.astype(BF16)
    iw_ref[...] = mm(OFF_IW, LANES)
    qm_ref[...] = mm(OFF_QM, D_M)
    gm_ref[...] = mm(OFF_GM, D_M)


def _project(x2d, g, w_pad, tabs, n_tab_tiles, tm, kv_transposed):
    n, d = x2d.shape
    nt = n // tm
    tok = lambda w: pl.BlockSpec((tm, w), lambda i: (i, 0))
    tab = pl.BlockSpec((tm, LANES), lambda i: (i % n_tab_tiles, 0))
    vt = lambda w: pl.BlockSpec((1, w // LANES * VT_ROWS, tm), lambda i: (i, 0, 0))
    vt_shape = lambda w: jax.ShapeDtypeStruct((nt, w // LANES * VT_ROWS, tm), BF16)
    widths = [D_A] * 4 + [D_B] * 4 + [IDX_HEADS * IDX_DIM, IDX_DIM, LANES, D_M, D_M]
    out_shape = [jax.ShapeDtypeStruct((n, w), F32) for w in widths]
    out_specs = [tok(w) for w in widths]
    if kv_transposed:
        for idx in (1, 2, 5, 6, 9):
            w = widths[idx]
            out_shape[idx] = jax.ShapeDtypeStruct((nt // n_tab_tiles, w, n_tab_tiles * tm), F32)
            out_specs[idx] = pl.BlockSpec((1, w, tm), lambda i: (i // n_tab_tiles, 0, i % n_tab_tiles))
    out_shape += [jax.ShapeDtypeStruct((n, D_A), BF16), vt_shape(D_A),
                  jax.ShapeDtypeStruct((n, D_B), BF16), vt_shape(D_B),
                  jax.ShapeDtypeStruct((n, LANES), BF16), jax.ShapeDtypeStruct((nt, 1, D_A), F32)]
    out_specs += [tok(D_A), vt(D_A), tok(D_B), vt(D_B), tok(LANES), pl.BlockSpec((1, 1, D_A), lambda i: (i, 0, 0))]
    return pl.pallas_call(
        functools.partial(_proj_kernel, kv_transposed=kv_transposed),
        grid=(nt,),
        in_specs=[tok(d),
                  pl.BlockSpec((1, d), lambda i: (0, 0)),
                  pl.BlockSpec((d, W_PAD), lambda i: (0, 0)),
                  tab, tab, tab],
        out_specs=out_specs,
        out_shape=out_shape,
        compiler_params=_cparams("parallel"),
        name="proj",
    )(x2d, g, w_pad, *tabs)


def _memkv_kernel(m_ref, g_ref, w_ref, mk_ref, mv_ref):
    x = m_ref[0]
    ms = jnp.mean(x * x, axis=-1, keepdims=True)
    h = (x * lax.rsqrt(ms + EPS) * g_ref[...]).astype(BF16)
    u = jnp.dot(h, w_ref[...], preferred_element_type=F32)
    mk_ref[0] = u[:, :D_M]
    mv_ref[0] = u[:, D_M:]


def _mem_kv(mem, g, w_bf):
    b, nm, d = mem.shape
    return pl.pallas_call(
        _memkv_kernel,
        grid=(b,),
        in_specs=[pl.BlockSpec((1, nm, d), lambda i: (i, 0, 0)),
                  pl.BlockSpec((1, d), lambda i: (0, 0)),
                  pl.BlockSpec((d, 2 * D_M), lambda i: (0, 0))],
        out_specs=[pl.BlockSpec((1, nm, D_M), lambda i: (i, 0, 0))] * 2,
        out_shape=[jax.ShapeDtypeStruct((b, nm, D_M), F32)] * 2,
        compiler_params=_cparams("parallel"),
        name="mem_kv",
    )(mem, g, w_bf)


def _flash_step(s, m, acc_ref, c, vt):
    m_new = jnp.maximum(m, jnp.max(s, axis=0, keepdims=True))
    alpha = jnp.exp2(m - m_new)
    p = jnp.exp2(s - m_new).astype(BF16)
    acc_ref[c] = alpha * acc_ref[c] + jnp.dot(vt, p, preferred_element_type=F32)
    return m_new


def _vt_with_ones(v_t):
    vt = v_t.astype(BF16)
    ones = jnp.ones((SUM_ROWS, v_t.shape[1]), BF16)
    parts = []
    for c in range(v_t.shape[0] // LANES):
        parts += [vt[c * LANES:(c + 1) * LANES], ones]
    return jnp.concatenate(parts, axis=0)


def _moba_kernel(q_ref, k_ref, vt_ref, km_ref, o_ref, qaug_scr, acc_scr, s_scr):
    qi = pl.program_id(1)
    n_pairs = D_A // LANES
    blk_id = lax.broadcasted_iota(jnp.int32, (TILE, LANES), 1)
    blk_f = blk_id.astype(F32)
    past = blk_id < qi
    for c in range(n_pairs):
        cs = slice(c * LANES, (c + 1) * LANES)
        qc = q_ref[0, :, cs]
        kmc = km_ref[0, :, cs]
        biases = []
        for hh in range(2):
            qm = jnp.where(_half_mask(hh), qc, 0.0)
            g = lax.dot_general(qm, kmc, NT, precision=lax.Precision.HIGHEST, preferred_element_type=F32)
            g = jnp.where(past, g, -jnp.inf)
            sel = jnp.zeros((TILE, LANES), jnp.bool_)
            for _ in range(MOBA_TOPK):
                m = jnp.max(g, axis=1, keepdims=True)
                first = jnp.min(jnp.where(g == m, blk_f, 1e9), axis=1, keepdims=True)
                pick = (blk_f == first) & past
                sel = sel | pick
                g = jnp.where(pick, -jnp.inf, g)
            biases.append(jnp.where(sel, 0.0, NEG).astype(BF16))
        qaug_scr[c] = jnp.concatenate([_head_pair_queries(qc), jnp.concatenate(biases, axis=0)], axis=1)

    own0 = pl.multiple_of(qi * TILE, TILE)
    ko = k_ref[0, pl.ds(own0, TILE), :]
    vto = vt_ref[0, qi]
    krow = lax.broadcasted_iota(jnp.int32, (TILE, 2 * TILE), 0)
    qcol = lax.broadcasted_iota(jnp.int32, (TILE, 2 * TILE), 1) % TILE
    ms = []
    for c in range(n_pairs):
        cs = slice(c * LANES, (c + 1) * LANES)
        s = lax.dot_general(ko[:, cs], qaug_scr[c, :, :LANES], NT, preferred_element_type=F32)
        s = jnp.where(krow <= qcol, s, NEG)
        m = jnp.max(s, axis=0, keepdims=True)
        p = jnp.exp2(s - m).astype(BF16)
        ms.append(m)
        acc_scr[c] = jnp.dot(vto[c * VT_ROWS:(c + 1) * VT_ROWS, :], p, preferred_element_type=F32)

    def logits(j, c):
        j0 = pl.multiple_of(j * TILE, TILE)
        onehot = jnp.where(blk_id == j, 1.0, 0.0).astype(BF16)
        k_aug = jnp.concatenate([k_ref[0, pl.ds(j0, TILE), c * LANES:(c + 1) * LANES], onehot], axis=1)
        return lax.dot_general(k_aug, qaug_scr[c], NT, preferred_element_type=F32)

    s_scr[0] = logits(0, 0)

    def body(j, ms):
        vtj = vt_ref[0, j]
        new_m = []
        for c in range(n_pairs):
            if c + 1 < n_pairs:
                s_scr[c + 1] = logits(j, c + 1)
            else:
                s_scr[0] = logits(jnp.minimum(j + 1, qi - 1), 0)
            new_m.append(_flash_step(s_scr[c], ms[c], acc_scr, c, vtj[c * VT_ROWS:(c + 1) * VT_ROWS, :]))
        return tuple(new_m)

    lax.fori_loop(0, qi, body, tuple(ms))
    o_ref[0] = jnp.concatenate([_pair_output(acc_scr[c]) for c in range(n_pairs)], axis=1)


def _moba_prompt(qa, ka_bf, va_t, kmean_pad):
    b, t, _ = qa.shape
    nt = t // TILE
    return pl.pallas_call(
        _moba_kernel,
        grid=(b, nt),
        in_specs=[pl.BlockSpec((1, TILE, D_A), lambda i, j: (i, j, 0)),
                  pl.BlockSpec((1, t, D_A), lambda i, j: (i, 0, 0)),
                  pl.BlockSpec((1, nt, va_t.shape[2], TILE), lambda i, j: (i, 0, 0, 0)),
                  pl.BlockSpec((1, LANES, D_A), lambda i, j: (i, 0, 0))],
        out_specs=pl.BlockSpec((1, TILE, D_A), lambda i, j: (i, j, 0)),
        out_shape=jax.ShapeDtypeStruct((b, t, D_A), F32),
        scratch_shapes=[pltpu.VMEM((D_A // LANES, 2 * TILE, 2 * LANES), BF16),
                        pltpu.VMEM((D_A // LANES, VT_ROWS, 2 * TILE), F32),
                        pltpu.VMEM((D_A // LANES, TILE, 2 * TILE), F32)],
        compiler_params=_cparams("parallel", "arbitrary"),
        name="moba_prompt",
    )(qa, ka_bf, va_t, kmean_pad)


def _dsa_kernel(q_ref, iq_ref, iw_ref, ik_ref, k_ref, vt_ref, o_ref,
                key_scr, hi_scr, lo_scr, rel_a, rel_b, acc_scr, s_scr, *, k_sel, idx_bits):
    qi = pl.program_id(1)
    n_tiles = qi + 1
    n_pairs = D_B // LANES
    kf = float(k_sel)
    krow = lax.broadcasted_iota(jnp.int32, (TILE, TILE), 0)
    q_pos = lax.broadcasted_iota(jnp.int32, (TILE, TILE), 1) + qi * TILE
    key_pos = lambda j: krow + j * TILE

    iq = iq_ref[0]
    iw_t = (iw_ref[0] * (IDX_HEADS ** -0.5)).T
    iqm = [(jnp.where(_half_mask(n % 2), iq[:, (n // 2) * LANES:(n // 2 + 1) * LANES], 0.0)
            * (IDX_DIM ** -0.5)).astype(BF16) for n in range(IDX_HEADS)]

    def relevances(j, out_ref):
        j0 = pl.multiple_of(j * TILE, TILE)
        ikj = ik_ref[0, pl.ds(j0, TILE), :]
        for n in range(IDX_HEADS):
            out_ref[n] = lax.dot_general(ikj, iqm[n], NT, preferred_element_type=F32)

    def store_keys(j, rel_ref):
        score = jnp.zeros((TILE, TILE), F32)
        for n in range(IDX_HEADS):
            score = score + iw_t[n:n + 1, :] * jnp.maximum(rel_ref[n], 0.0)
        key = jnp.where(key_pos(j) > q_pos, jnp.int32(INT_MIN), _sortable(score))
        key_scr[j] = key
        hi_scr[j] = lax.shift_right_arithmetic(key, 16).astype(jnp.int16)
        lo_scr[j] = ((key & 0xFFFF) + I16_MIN).astype(jnp.int16)

    last = n_tiles - 1
    relevances(0, rel_a)

    def score_body(i, carry):
        j = 2 * i
        relevances(jnp.minimum(j + 1, last), rel_b)
        store_keys(j, rel_a)
        relevances(jnp.minimum(j + 2, last), rel_a)
        store_keys(jnp.minimum(j + 1, last), rel_b)
        return carry

    lax.fori_loop(0, (n_tiles + 1) // 2, score_body, 0)

    def count16(plane, pred):
        def body(j, acc):
            w = jnp.where(pred(plane[j]), jnp.int16(1), jnp.int16(0))
            for r in range(TILE // 64):
                acc = acc + w[r * 64:(r + 1) * 64]
            return acc
        acc = lax.fori_loop(0, n_tiles, body, jnp.zeros((64, TILE), jnp.int16))
        return jnp.sum(acc.astype(jnp.int32).astype(F32), axis=0, keepdims=True)

    def count32(pred):
        def body(j, acc):
            w = jnp.where(pred(key_scr[j], j), 1.0, 0.0)
            for r in range(TILE // 64):
                acc = acc + w[r * 64:(r + 1) * 64]
            return acc
        acc = lax.fori_loop(0, n_tiles, body, jnp.zeros((64, TILE), F32))
        return jnp.sum(acc, axis=0, keepdims=True)

    def search16(plane, target):
        zero = jnp.zeros((1, TILE), jnp.int32)
        total = (n_tiles * TILE).astype(F32)
        c0 = count16(plane, lambda x: x >= zero.astype(jnp.int16))
        ok0 = c0 >= target
        init = (jnp.where(ok0, zero, jnp.int32(I16_MIN)), jnp.where(ok0, c0, total))

        def bit(i, carry):
            thr, cnt = carry
            cand = thr | lax.shift_left(jnp.int32(1), 14 - i)
            c = count16(plane, lambda x: x >= cand.astype(jnp.int16))
            ok = c >= target
            return jnp.where(ok, cand, thr), jnp.where(ok, c, cnt)

        return lax.fori_loop(0, 15, bit, init)

    thr_hi, cnt_hi = search16(hi_scr, kf)
    thr_hi16 = thr_hi.astype(jnp.int16)
    above = count16(hi_scr, lambda x: x > thr_hi16)

    def mask_lo(j, carry):
        lo_scr[j] = jnp.where(hi_scr[j] == thr_hi16, lo_scr[j], jnp.int16(I16_MIN))
        return carry

    lax.fori_loop(0, n_tiles, mask_lo, 0)
    thr_lo, cnt_lo = search16(lo_scr, kf - above)
    thr = thr_hi * 65536 + (thr_lo - I16_MIN)
    cnt_ge = above + jnp.where(thr_lo > I16_MIN, cnt_lo, cnt_hi - above)
    has_ties = jnp.max(cnt_ge) > kf

    @pl.when(jnp.logical_not(has_ties))
    def _():
        floor = jnp.maximum(thr, jnp.int32(INT_MIN + 1))

        def to_bias(j, carry):
            key_scr[j] = lax.bitcast_convert_type(jnp.where(key_scr[j] >= floor, 0.0, NEG).astype(F32), jnp.int32)
            return carry

        lax.fori_loop(0, n_tiles, to_bias, 0)

    @pl.when(has_ties)
    def _():
        need = kf - count32(lambda k, j: k > thr)

        def cbit(i, cpos):
            cand = cpos | lax.shift_left(jnp.int32(1), idx_bits - 1 - i)
            cnt = count32(lambda k, j: (k == thr) & (key_pos(j) < cand))
            return jnp.where(cnt < need, cand, cpos)

        cstar = lax.fori_loop(0, idx_bits, cbit, jnp.zeros((1, TILE), jnp.int32))

        def to_bias(j, carry):
            k = key_scr[j]
            kp = key_pos(j)
            keep = ((k > thr) | ((k == thr) & (kp <= cstar))) & (kp <= q_pos)
            key_scr[j] = lax.bitcast_convert_type(jnp.where(keep, 0.0, NEG).astype(F32), jnp.int32)
            return carry

        lax.fori_loop(0, n_tiles, to_bias, 0)

    q_pairs = [_head_pair_queries(q_ref[0, :, c * LANES:(c + 1) * LANES]) for c in range(n_pairs)]
    for c in range(n_pairs):
        acc_scr[c] = jnp.zeros((VT_ROWS, 2 * TILE), F32)

    def logits(j, c):
        j0 = pl.multiple_of(j * TILE, TILE)
        return lax.dot_general(k_ref[0, pl.ds(j0, TILE), c * LANES:(c + 1) * LANES], q_pairs[c], NT,
                               preferred_element_type=F32)

    s_scr[0] = logits(0, 0)

    def body(j, ms):
        vtj = vt_ref[0, j]
        bias = lax.bitcast_convert_type(key_scr[j], F32)
        bias2 = jnp.concatenate([bias, bias], axis=1)
        new_m = []
        for c in range(n_pairs):
            if c + 1 < n_pairs:
                s_scr[c + 1] = logits(j, c + 1)
            else:
                s_scr[0] = logits(jnp.minimum(j + 1, n_tiles - 1), 0)
            new_m.append(_flash_step(s_scr[c] + bias2, ms[c], acc_scr, c, vtj[c * VT_ROWS:(c + 1) * VT_ROWS, :]))
        return tuple(new_m)

    lax.fori_loop(0, n_tiles, body, tuple(jnp.full((1, 2 * TILE), NEG, F32) for _ in range(n_pairs)))
    o_ref[0] = jnp.concatenate([_pair_output(acc_scr[c]) for c in range(n_pairs)], axis=1)


def _dsa_prompt(qb, iq, iw, ik_bf, kb_bf, vb_t):
    b, t, _ = qb.shape
    nt = t // TILE
    k_sel = min(IDX_TOPK, t // 4)
    kern = functools.partial(_dsa_kernel, k_sel=k_sel, idx_bits=max(1, (t - 1).bit_length()))
    res = lambda w: pl.BlockSpec((1, t, w), lambda i, j: (i, 0, 0))
    til = lambda w: pl.BlockSpec((1, TILE, w), lambda i, j: (i, j, 0))
    return pl.pallas_call(
        kern,
        grid=(b, nt),
        in_specs=[til(D_B), til(IDX_HEADS * IDX_DIM), til(LANES), res(LANES), res(D_B),
                  pl.BlockSpec((1, nt, vb_t.shape[2], TILE), lambda i, j: (i, 0, 0, 0))],
        out_specs=til(D_B),
        out_shape=jax.ShapeDtypeStruct((b, t, D_B), F32),
        scratch_shapes=[pltpu.VMEM((nt, TILE, TILE), jnp.int32),
                        pltpu.VMEM((nt, TILE, TILE), jnp.int16),
                        pltpu.VMEM((nt, TILE, TILE), jnp.int16),
                        pltpu.VMEM((IDX_HEADS, TILE, TILE), F32),
                        pltpu.VMEM((IDX_HEADS, TILE, TILE), F32),
                        pltpu.VMEM((D_B // LANES, VT_ROWS, 2 * TILE), F32),
                        pltpu.VMEM((D_B // LANES, TILE, 2 * TILE), F32)],
        compiler_params=_cparams("parallel", "arbitrary"),
        name="dsa_prompt",
    )(qb, iq, iw, ik_bf, kb_bf, vb_t)


def _merge_kernel(x_ref, oa_ref, ga_ref, ob_ref, gb_ref, qm_ref, gm_ref, mk_ref, mv_ref, wo_ref, fg_ref, y_ref,
                  *, final):
    lane = lax.broadcasted_iota(jnp.int32, (1, LANES), 1)
    mk = mk_ref[0].astype(BF16)
    mv = mv_ref[0].astype(BF16)
    oms = []
    for c in range(D_M // LANES):
        cs = slice(c * LANES, (c + 1) * LANES)
        qc = qm_ref[0, :, cs]
        heads = []
        for hh in range(2):
            q_bf = (jnp.where(_half_mask(hh), qc, 0.0) * (HEAD_DIM ** -0.5)).astype(BF16)
            s = lax.dot_general(q_bf, mk[:, cs], NT, preferred_element_type=F32)
            m = jnp.max(s, axis=1, keepdims=True)
            p = jnp.exp(s - m)
            l = jnp.sum(p, axis=1, keepdims=True)
            heads.append(jnp.dot(p.astype(BF16), mv[:, cs], preferred_element_type=F32) / l)
        oms.append(jnp.where(lane < HEAD_DIM, heads[0], heads[1]))
    om = jnp.concatenate(oms, axis=1)
    silu = lambda g: g / (1.0 + jnp.exp(-g))
    ymix = jnp.concatenate([oa_ref[0] * silu(ga_ref[0]), ob_ref[0] * silu(gb_ref[0]), om * silu(gm_ref[0])],
                           axis=1).astype(BF16)
    z = x_ref[0] + jnp.dot(ymix, wo_ref[...], preferred_element_type=F32)
    if final:
        ms = jnp.mean(z * z, axis=-1, keepdims=True)
        z = z * lax.rsqrt(ms + EPS) * fg_ref[...]
    y_ref[0] = z


def _merge(x, oa, ga, ob, gb, qm, gm, mk, mv, wo_bf, fg, tm, final):
    b, t, d = x.shape
    til = lambda w: pl.BlockSpec((1, tm, w), lambda i, j: (i, j, 0))
    memspec = pl.BlockSpec((1, N_MEM, D_M), lambda i, j: (i, 0, 0))
    return pl.pallas_call(
        functools.partial(_merge_kernel, final=final),
        grid=(b, t // tm),
        in_specs=[til(d), til(D_A), til(D_A), til(D_B), til(D_B), til(D_M), til(D_M), memspec, memspec,
                  pl.BlockSpec((d, d), lambda i, j: (0, 0)),
                  pl.BlockSpec((1, d), lambda i, j: (0, 0))],
        out_specs=til(d),
        out_shape=jax.ShapeDtypeStruct((b, t, d), F32),
        compiler_params=_cparams("parallel", "parallel"),
        name="merge",
    )(x, oa, ga, ob, gb, qm, gm, mk, mv, wo_bf, fg)


def _expand_heads(q, n_heads):
    t, w = q.shape
    qt = jnp.concatenate([q] * n_heads, axis=0)
    r = lax.broadcasted_iota(jnp.int32, (n_heads * t, w), 0)
    l = lax.broadcasted_iota(jnp.int32, (n_heads * t, w), 1)
    keep = (l >= (r // t) * HEAD_DIM) & (l < (r // t + 1) * HEAD_DIM)
    return jnp.where(keep, qt, 0.0)


def _collapse_heads(o, n_heads, t):
    w = o.shape[1]
    l = lax.broadcasted_iota(jnp.int32, (t, w), 1)
    out = jnp.zeros((t, w), F32)
    for h in range(n_heads):
        keep = (l >= h * HEAD_DIM) & (l < (h + 1) * HEAD_DIM)
        out = out + jnp.where(keep, o[h * t:(h + 1) * t, :], 0.0)
    return out


def _pages_t(refs):
    pages = [r[0, 0].reshape(r.shape[2] * r.shape[3], r.shape[4]) for r in refs]
    return pages[0] if len(pages) == 1 else jnp.concatenate(pages, axis=1)


def _cache_view(cache):
    return jnp.transpose(cache, (0, 1, 3, 4, 2))


def _values_with_ones(vt):
    return jnp.concatenate([vt.astype(BF16), jnp.ones((SUM_ROWS, vt.shape[1]), BF16)], axis=0)


def _softmax_pv(s_scr, vt_scr, s_new, v_new):
    nsteps = s_scr.shape[0]
    d = v_new.shape[1]
    m = jnp.max(jnp.max(s_scr[...], axis=0), axis=1, keepdims=True)
    m = jnp.maximum(m, jnp.max(s_new, axis=1, keepdims=True))
    p_new = jnp.exp(s_new - m)
    l_new = jnp.sum(p_new, axis=1, keepdims=True)
    acc_new = jnp.dot(p_new.astype(BF16), v_new, preferred_element_type=F32)

    def body(i, acc):
        p = jnp.exp(s_scr[i] - m).astype(BF16)
        return acc + lax.dot_general(p, vt_scr[i], NT, preferred_element_type=F32)

    acc = lax.fori_loop(0, nsteps, body, jnp.zeros((s_new.shape[0], d + SUM_ROWS), F32),
                        unroll=nsteps if nsteps <= 8 else 8)
    return (acc[:, :d] + acc_new) / (acc[:, d:d + 1] + l_new)


def _paged_specs(layer, block_tail, per_step):
    zeros = (0,) * len(block_tail)
    return [pl.BlockSpec((1, 1) + block_tail, lambda s, j, pt, i=i: (layer, pt[s, j * per_step + i]) + zeros)
            for i in range(per_step)]


def _moba_dec_kernel(pt_ref, q_ref, *rest, ppb, bps):
    n_pg = ppb * bps
    k_refs, v_refs = rest[:n_pg], rest[n_pg:2 * n_pg]
    kn_ref, vn_ref, o_ref, s_scr, vt_scr, g_scr, qxt_scr = rest[2 * n_pg:]
    j = pl.program_id(1)
    nsteps = pl.num_programs(1)
    t = q_ref.shape[1]
    rows = H_A * t
    blk_w = s_scr.shape[2] // bps
    qx = _expand_heads(q_ref[0], H_A)
    qx_bf = (qx * (HEAD_DIM ** -0.5)).astype(BF16)

    @pl.when(j == 0)
    def _():
        qxt_scr[...] = jnp.concatenate([qx, jnp.zeros((LANES - rows, D_A), F32)], axis=0).T

    kt = _pages_t(k_refs)
    s_scr[j] = jnp.dot(qx_bf, kt.astype(BF16), preferred_element_type=F32)
    vt_scr[j] = _values_with_ones(_pages_t(v_refs))
    for b in range(bps):
        kmean = jnp.mean(kt[:, b * blk_w:(b + 1) * blk_w], axis=1, keepdims=True)
        g_scr[pl.ds(j * bps + b, 1), :] = jnp.sum(qxt_scr[...] * kmean, axis=0, keepdims=True)

    @pl.when(j == nsteps - 1)
    def _():
        npad = kn_ref.shape[1]
        r = lax.broadcasted_iota(jnp.int32, (rows, npad), 0)
        cidx = lax.broadcasted_iota(jnp.int32, (rows, npad), 1)
        s_new = lax.dot_general(qx_bf, kn_ref[0].astype(BF16), NT, preferred_element_type=F32)
        s_new = jnp.where(cidx <= (r % t), s_new, NEG)

        n_blocks = g_scr.shape[0]
        g = g_scr[...]
        bidx = lax.broadcasted_iota(jnp.int32, g.shape, 0).astype(F32)
        sel = jnp.zeros(g.shape, jnp.bool_)
        for _ in range(min(MOBA_TOPK, n_blocks)):
            mx = jnp.max(g, axis=0, keepdims=True)
            first = jnp.min(jnp.where(g == mx, bidx, 1e9), axis=0, keepdims=True)
            pick = bidx == first
            sel = sel | pick
            g = jnp.where(pick, -jnp.inf, g)
        bias_t = jnp.where(sel, 0.0, NEG).T
        for i in range(s_scr.shape[0]):
            cols = [jnp.broadcast_to(bias_t[:rows, i * bps + b:i * bps + b + 1], (rows, blk_w)) for b in range(bps)]
            s_scr[i] = s_scr[i] + (cols[0] if bps == 1 else jnp.concatenate(cols, axis=1))
        o_ref[0] = _collapse_heads(_softmax_pv(s_scr, vt_scr, s_new, vn_ref[0].astype(BF16)), H_A, t)


def _moba_decode(layer, page_table, qa, cache_kt, cache_vt, kn_pad, vn_pad):
    nseq, t, _ = qa.shape
    n_pages = page_table.shape[1]
    page_size = cache_kt.shape[4]
    ppb = MOBA_BLOCK // page_size
    nb = n_pages // ppb
    bps = max(1, KV_PAGES_PER_STEP // ppb)
    while nb % bps:
        bps //= 2
    rows = H_A * t
    seq = lambda w, r: pl.BlockSpec((1, r, w), lambda s, j, pt: (s, 0, 0))
    kspecs = _paged_specs(layer, cache_kt.shape[2:], ppb * bps)
    grid_spec = pltpu.PrefetchScalarGridSpec(
        num_scalar_prefetch=1,
        grid=(nseq, nb // bps),
        in_specs=[seq(D_A, t)] + kspecs + kspecs + [seq(D_A, kn_pad.shape[1])] * 2,
        out_specs=seq(D_A, t),
        scratch_shapes=[pltpu.VMEM((nb // bps, rows, bps * MOBA_BLOCK), F32),
                        pltpu.VMEM((nb // bps, D_A + SUM_ROWS, bps * MOBA_BLOCK), BF16),
                        pltpu.VMEM((nb, LANES), F32),
                        pltpu.VMEM((D_A, LANES), F32)],
    )
    assert rows <= LANES
    return pl.pallas_call(
        functools.partial(_moba_dec_kernel, ppb=ppb, bps=bps),
        grid_spec=grid_spec,
        out_shape=jax.ShapeDtypeStruct((nseq, t, D_A), F32),
        compiler_params=_cparams("parallel", "arbitrary"),
        name="moba_decode",
    )(page_table, qa, *([cache_kt] * (ppb * bps)), *([cache_vt] * (ppb * bps)), kn_pad, vn_pad)


def _idx_dec_kernel(pt_ref, iq_ref, iw_ref, *rest, per_step, k_sel, idx_bits, past, t):
    ik_refs = rest[:per_step]
    ikn_ref, bias_ref, key_scr = rest[per_step:]
    j = pl.program_id(1)
    nsteps = pl.num_programs(1)
    width = key_scr.shape[2]
    iq_bf = iq_ref[0].astype(BF16)
    iw = iw_ref[0][:, :1]

    def scores(rel):
        wrel = iw * jnp.maximum(rel * (IDX_DIM ** -0.5), 0.0)
        sc = wrel[0:t]
        for n in range(1, IDX_HEADS):
            sc = sc + wrel[n * t:(n + 1) * t]
        return sc

    ik_t = jnp.concatenate([r[0, 0] for r in ik_refs], axis=1)
    key_scr[j] = _sortable(scores(jnp.dot(iq_bf, ik_t.astype(BF16), preferred_element_type=F32)))

    @pl.when(j == nsteps - 1)
    def _():
        kf = float(k_sel)
        npad = ikn_ref.shape[1]
        r_new = lax.broadcasted_iota(jnp.int32, (t, npad), 0)
        c_new = lax.broadcasted_iota(jnp.int32, (t, npad), 1)
        rel_new = lax.dot_general(iq_bf, ikn_ref[0].astype(BF16), NT, preferred_element_type=F32)
        key_new = jnp.where(c_new <= r_new, _sortable(scores(rel_new)), jnp.int32(INT_MIN))
        tail = jnp.full((t, width - npad), jnp.int32(INT_MIN))
        key_scr[nsteps] = jnp.concatenate([key_new, tail], axis=1)

        keys = key_scr[...]
        pos = (lax.broadcasted_iota(jnp.int32, keys.shape, 0) * width
               + lax.broadcasted_iota(jnp.int32, keys.shape, 2))

        def count(pred):
            w = jnp.sum(jnp.where(pred, 1.0, 0.0), axis=0)
            acc = w[:, :LANES]
            for s in range(1, width // LANES):
                acc = acc + w[:, s * LANES:(s + 1) * LANES]
            return jnp.sum(acc, axis=1, keepdims=True)[None]

        zero = jnp.zeros((1, t, 1), jnp.int32)
        thr = jnp.where(count(keys >= zero) >= kf, zero, jnp.int32(INT_MIN))

        def bit(i, thr):
            cand = thr | lax.shift_left(jnp.int32(1), 30 - i)
            return jnp.where(count(keys >= cand) >= kf, cand, thr)

        thr = lax.fori_loop(0, 31, bit, thr)
        need = kf - count(keys > thr)

        def cbit(i, cpos):
            cand = cpos | lax.shift_left(jnp.int32(1), idx_bits - 1 - i)
            return jnp.where(count((keys == thr) & (pos < cand)) < need, cand, cpos)

        cstar = lax.fori_loop(0, idx_bits, cbit, zero)
        keep = (keys > thr) | ((keys == thr) & (pos <= cstar))
        bias_ref[0] = jnp.where(keep, 0.0, NEG).astype(F32)


def _idx_decode(layer, page_table, iq_rows, iw_rows, cache_ik, ikn_pad, t):
    nseq = iq_rows.shape[0]
    n_pages = page_table.shape[1]
    page_size = cache_ik.shape[3]
    per_step = IDX_PAGES_PER_STEP
    while n_pages % per_step:
        per_step //= 2
    nsteps = n_pages // per_step
    width = per_step * page_size
    past = n_pages * page_size
    k_sel = min(IDX_TOPK, (past + t) // 4)
    idx_bits = max(1, ((nsteps + 1) * width - 1).bit_length())
    seq = lambda r, w: pl.BlockSpec((1, r, w), lambda s, j, pt: (s, 0, 0))
    grid_spec = pltpu.PrefetchScalarGridSpec(
        num_scalar_prefetch=1,
        grid=(nseq, nsteps),
        in_specs=[seq(IDX_HEADS * t, IDX_DIM), seq(IDX_HEADS * t, LANES)]
        + _paged_specs(layer, cache_ik.shape[2:], per_step) + [seq(ikn_pad.shape[1], IDX_DIM)],
        out_specs=pl.BlockSpec((1, nsteps + 1, t, width), lambda s, j, pt: (s, 0, 0, 0)),
        scratch_shapes=[pltpu.VMEM((nsteps + 1, t, width), jnp.int32)],
    )
    kern = functools.partial(_idx_dec_kernel, per_step=per_step, k_sel=k_sel, idx_bits=idx_bits, past=past, t=t)
    return pl.pallas_call(
        kern,
        grid_spec=grid_spec,
        out_shape=jax.ShapeDtypeStruct((nseq, nsteps + 1, t, width), F32),
        compiler_params=_cparams("parallel", "arbitrary"),
        name="idx_decode",
    )(page_table, iq_rows, iw_rows, *([cache_ik] * per_step), ikn_pad)


def _dsa_dec_kernel(pt_ref, q_ref, b_ref, bn_ref, *rest, pps):
    k_refs, v_refs = rest[:pps], rest[pps:2 * pps]
    kn_ref, vn_ref, o_ref, s_scr, vt_scr = rest[2 * pps:]
    j = pl.program_id(1)
    nsteps = pl.num_programs(1)
    t = q_ref.shape[1]
    qx_bf = (_expand_heads(q_ref[0], H_B) * (HEAD_DIM ** -0.5)).astype(BF16)

    kt = _pages_t(k_refs).astype(BF16)
    s_scr[j] = jnp.dot(qx_bf, kt, preferred_element_type=F32) + jnp.concatenate([b_ref[0, 0]] * H_B, axis=0)
    vt_scr[j] = _values_with_ones(_pages_t(v_refs))

    @pl.when(j == nsteps - 1)
    def _():
        s_new = lax.dot_general(qx_bf, kn_ref[0].astype(BF16), NT, preferred_element_type=F32)
        s_new = s_new + jnp.concatenate([bn_ref[0, 0]] * H_B, axis=0)
        o_ref[0] = _collapse_heads(_softmax_pv(s_scr, vt_scr, s_new, vn_ref[0].astype(BF16)), H_B, t)


def _dsa_decode(layer, page_table, qb, bias, cache_kt, cache_vt, kn_pad, vn_pad):
    nseq, t, _ = qb.shape
    n_pages = page_table.shape[1]
    page_size = cache_kt.shape[4]
    pps = KV_PAGES_PER_STEP
    while n_pages % pps or bias.shape[3] % (pps * page_size):
        pps //= 2
    width = bias.shape[3]
    chunk = pps * page_size
    assert n_pages % pps == 0 and width % chunk == 0
    per_w = width // chunk
    rows = H_B * t
    npad = kn_pad.shape[1]
    seq = lambda w, r: pl.BlockSpec((1, r, w), lambda s, j, pt: (s, 0, 0))
    bspec = pl.BlockSpec((1, 1, t, chunk), lambda s, j, pt: (s, j // per_w, 0, j % per_w))
    bnew = pl.BlockSpec((1, 1, t, npad), lambda s, j, pt: (s, bias.shape[1] - 1, 0, 0))
    kspecs = _paged_specs(layer, cache_kt.shape[2:], pps)
    grid_spec = pltpu.PrefetchScalarGridSpec(
        num_scalar_prefetch=1,
        grid=(nseq, n_pages // pps),
        in_specs=[seq(D_B, t), bspec, bnew] + kspecs + kspecs + [seq(D_B, npad)] * 2,
        out_specs=seq(D_B, t),
        scratch_shapes=[pltpu.VMEM((n_pages // pps, rows, chunk), F32),
                        pltpu.VMEM((n_pages // pps, D_B + SUM_ROWS, chunk), BF16)],
    )
    return pl.pallas_call(
        functools.partial(_dsa_dec_kernel, pps=pps),
        grid_spec=grid_spec,
        out_shape=jax.ShapeDtypeStruct((nseq, t, D_B), F32),
        compiler_params=_cparams("parallel", "arbitrary"),
        name="dsa_decode",
    )(page_table, qb, bias, bias, *([cache_kt] * pps), *([cache_vt] * pps), kn_pad, vn_pad)


def _rope_tables(pos):
    half = ROT_DIM // 2
    inv = jnp.power(ROPE_THETA, -jnp.arange(half, dtype=F32) / half)
    ang = pos.astype(F32)[:, None] * inv[None, :]
    cos, sin = jnp.cos(ang), jnp.sin(ang)
    n = pos.shape[0]
    ones = jnp.ones((n, HEAD_DIM - ROT_DIM), F32)
    zeros = jnp.zeros((n, HEAD_DIM - ROT_DIM), F32)
    zh = jnp.zeros((n, half), F32)
    tc = jnp.concatenate([cos, cos, ones], axis=1)
    ts1 = jnp.concatenate([-sin, zh, zeros], axis=1)
    ts2 = jnp.concatenate([zh, sin, zeros], axis=1)
    dup = lambda a: jnp.concatenate([a, a], axis=1)
    return dup(tc), dup(ts1), dup(ts2)


def _pad_w_in(w):
    d = w.shape[0]
    main = w[:, :8 * _G]
    o = 8 * _G
    iq = w[:, o:o + 256]
    ik = w[:, o + 256:o + 320]
    iw = w[:, o + 320:o + 324]
    qm = w[:, o + 324:o + 324 + D_M]
    gm = w[:, o + 324 + D_M:o + 324 + 2 * D_M]
    iw_pad = jnp.concatenate([iw, jnp.zeros((d, LANES - IDX_HEADS), w.dtype)], axis=1)
    return jnp.concatenate([main, iq, ik, ik, iw_pad, qm, gm], axis=1).astype(BF16)


def kernel(x_prompt, x_sample, mem_prompt, cache_k_a, cache_v_a, cache_k_b, cache_v_b, cache_k_idx,
           cache_mem_k, cache_mem_v, page_table, norm_g, w_in, mem_norm_g, w_mem_kv, w_out, final_norm_g):
    depth = w_in.shape[0]
    b, t, d = x_prompt.shape
    nseq, ts, _ = x_sample.shape
    n_pages = page_table.shape[1]
    page_size = cache_k_a.shape[2]
    past = n_pages * page_size
    assert t % TILE == 0 and t // TILE <= LANES and page_size == LANES
    assert past % MOBA_BLOCK == 0 and ts <= LANES and ts % 8 == 0

    tabs_p = _rope_tables(jnp.arange(t, dtype=jnp.int32))
    pos_s = past + jnp.arange(ts, dtype=jnp.int32)
    tabs_s = tuple(jnp.tile(a, (nseq, 1)) for a in _rope_tables(pos_s))
    fg = final_norm_g.reshape(1, d)
    nbp = t // TILE

    xp, xs = x_prompt, x_sample
    outs = [[] for _ in range(12)]
    for l in range(depth):
        w_pad = _pad_w_in(w_in[l])
        g = norm_g[l].reshape(1, d)
        wo_bf = w_out[l].astype(BF16)
        last = l == depth - 1

        (qa, ka, va, ga, qb, kb, vb, gb, iq, ik, iw, qm, gm,
         ka_bf, va_t, kb_bf, vb_t, ik_bf, kmean) = _project(xp.reshape(b * t, d), g, w_pad, tabs_p, nbp, TILE, True)
        r3 = lambda a: a.reshape(b, t, a.shape[-1])
        r4 = lambda a: a.reshape(b, nbp, a.shape[-2], TILE)
        mk, mv = _mem_kv(mem_prompt, mem_norm_g[l].reshape(1, d), w_mem_kv[l].astype(BF16))
        kmean_pad = jnp.pad(kmean.reshape(b, nbp, D_A), ((0, 0), (0, LANES - nbp), (0, 0)))
        oa = _moba_prompt(r3(qa), r3(ka_bf), r4(va_t), kmean_pad)
        ob = _dsa_prompt(r3(qb), r3(iq).astype(BF16), r3(iw), r3(ik_bf), r3(kb_bf), r4(vb_t))
        xp = _merge(xp, oa, r3(ga), ob, r3(gb), r3(qm), r3(gm), mk, mv, wo_bf, fg, TILE, last)
        for lst, a, h in ((outs[0], ka, H_A), (outs[1], va, H_A), (outs[2], kb, H_B), (outs[3], vb, H_B)):
            lst.append(jnp.transpose(a.reshape(b, h, HEAD_DIM, t), (0, 3, 1, 2)))
        outs[4].append(jnp.transpose(ik, (0, 2, 1)))
        outs[5].append(mk.reshape(b, N_MEM, H_M, HEAD_DIM))
        outs[6].append(mv.reshape(b, N_MEM, H_M, HEAD_DIM))

        n_s = nseq * ts
        tm_s = n_s if n_s <= TILE else TILE
        (qa, ka, va, ga, qb, kb, vb, gb, iq, ik, iw, qm, gm,
         _, _, _, _, _, _) = _project(xs.reshape(n_s, d), g, w_pad, tabs_s, n_s // tm_s, tm_s, False)
        s3 = lambda a: a.reshape(nseq, ts, a.shape[-1])
        padn = lambda a: jnp.pad(s3(a), ((0, 0), (0, LANES - ts), (0, 0)))
        oa = _moba_decode(l, page_table, s3(qa), _cache_view(cache_k_a), _cache_view(cache_v_a), padn(ka), padn(va))
        iq_rows = s3(iq).reshape(nseq, ts, IDX_HEADS, IDX_DIM).transpose(0, 2, 1, 3).reshape(
            nseq, IDX_HEADS * ts, IDX_DIM)
        iw_rows = (s3(iw)[:, :, :IDX_HEADS] * (IDX_HEADS ** -0.5)).transpose(0, 2, 1).reshape(
            nseq, IDX_HEADS * ts, 1)
        iw_rows = jnp.broadcast_to(iw_rows, (nseq, IDX_HEADS * ts, LANES))
        bias = _idx_decode(l, page_table, iq_rows, iw_rows, jnp.transpose(cache_k_idx, (0, 1, 3, 2)), padn(ik), ts)
        ob = _dsa_decode(l, page_table, s3(qb), bias, _cache_view(cache_k_b), _cache_view(cache_v_b),
                         padn(kb), padn(vb))
        mks = cache_mem_k[l].reshape(nseq, N_MEM, D_M)
        mvs = cache_mem_v[l].reshape(nseq, N_MEM, D_M)
        xs = _merge(xs, oa, s3(ga), ob, s3(gb), s3(qm), s3(gm), mks, mvs, wo_bf, fg, ts, last)
        for lst, a, shp in ((outs[7], ka, (nseq, ts, H_A, HEAD_DIM)), (outs[8], va, (nseq, ts, H_A, HEAD_DIM)),
                            (outs[9], kb, (nseq, ts, H_B, HEAD_DIM)), (outs[10], vb, (nseq, ts, H_B, HEAD_DIM)),
                            (outs[11], ik, (nseq, ts, IDX_DIM))):
            lst.append(a.reshape(shp))

    stk = [jnp.stack(o) for o in outs]
    return (xp, xs, stk[0], stk[1], stk[2], stk[3], stk[4], stk[5], stk[6],
            stk[7], stk[8], stk[9], stk[10], stk[11])
```

```python
import functools

import jax
import jax.numpy as jnp
from jax import lax
from jax.experimental import pallas as pl
from jax.experimental.pallas import tpu as pltpu

HEAD_DIM = 64
H_A = 6
H_B = 6
H_M = 4
D_A = H_A * HEAD_DIM
D_B = H_B * HEAD_DIM
D_M = H_M * HEAD_DIM
N_MEM = 256
MOBA_BLOCK = 256
MOBA_TOPK = 3
IDX_HEADS = 4
IDX_DIM = 64
IDX_TOPK = 256
ROPE_THETA = 500000.0
ROT_DIM = HEAD_DIM // 4
EPS = 1e-6

LANES = 128
TILE = 256
NEG = -1e30
LOG2E = 1.4426950408889634
SUM_ROWS = 16
VT_ROWS = LANES + SUM_ROWS
INT_MIN = -(2 ** 31)
I16_MIN = -(2 ** 15)
VMEM_LIMIT = 60 * 1024 * 1024
IDX_PAGES_PER_STEP = 32
KV_PAGES_PER_STEP = 16

_G = D_A
OFF_QA, OFF_KA, OFF_VA, OFF_GA = 0, _G, 2 * _G, 3 * _G
OFF_QB, OFF_KB, OFF_VB, OFF_GB = 4 * _G, 5 * _G, 6 * _G, 7 * _G
OFF_IQ = 8 * _G
OFF_IK = OFF_IQ + 256
OFF_IW = OFF_IK + LANES
OFF_QM = OFF_IW + LANES
OFF_GM = OFF_QM + D_M
W_PAD = OFF_GM + D_M

F32 = jnp.float32
BF16 = jnp.bfloat16
NT = (((1,), (1,)), ((), ()))


def _cparams(*sem):
    return pltpu.CompilerParams(dimension_semantics=sem, vmem_limit_bytes=VMEM_LIMIT)


def _half_mask(hh):
    lane = lax.broadcasted_iota(jnp.int32, (1, LANES), 1)
    return (lane >= HEAD_DIM * hh) & (lane < HEAD_DIM * (hh + 1))


def _sortable(score):
    bits = lax.bitcast_convert_type(score, jnp.int32)
    key = bits ^ (lax.shift_right_arithmetic(bits, 31) & jnp.int32(0x7FFFFFFF))
    return jnp.where(score == 0.0, jnp.int32(0), key)


def _head_pair_queries(qc):
    scaled = qc * (HEAD_DIM ** -0.5 * LOG2E)
    return jnp.concatenate([jnp.where(_half_mask(0), scaled, 0.0), jnp.where(_half_mask(1), scaled, 0.0)],
                           axis=0).astype(BF16)


def _pair_output(acc):
    t = acc.shape[1] // 2
    o = acc[:LANES] / acc[LANES:LANES + 1]
    d = lax.broadcasted_iota(jnp.int32, (LANES, t), 0)
    return jnp.where(d < HEAD_DIM, o[:, :t], o[:, t:]).T


def _proj_kernel(x_ref, g_ref, w_ref, tc_ref, ts1_ref, ts2_ref,
                 qa_ref, ka_ref, va_ref, ga_ref, qb_ref, kb_ref, vb_ref, gb_ref,
                 iq_ref, ik_ref, iw_ref, qm_ref, gm_ref,
                 kabf_ref, vat_ref, kbbf_ref, vbt_ref, ikbf_ref, kmean_ref, *, kv_transposed):
    x = x_ref[...]
    ms = jnp.mean(x * x, axis=-1, keepdims=True)
    h = (x * lax.rsqrt(ms + EPS) * g_ref[...]).astype(BF16)
    tc, ts1, ts2 = tc_ref[...], ts1_ref[...], ts2_ref[...]

    def mm(off, width):
        return jnp.dot(h, w_ref[:, off:off + width], preferred_element_type=F32)

    def rope(u):
        outs = []
        for c in range(u.shape[1] // LANES):
            uc = u[:, c * LANES:(c + 1) * LANES]
            outs.append(uc * tc + pltpu.roll(uc, LANES - ROT_DIM // 2, 1) * ts1
                        + pltpu.roll(uc, ROT_DIM // 2, 1) * ts2)
        return outs[0] if len(outs) == 1 else jnp.concatenate(outs, axis=1)

    def put(ref, val, val_t=None):
        if kv_transposed:
            ref[0] = val.T if val_t is None else val_t
        else:
            ref[...] = val

    qa_ref[...] = rope(mm(OFF_QA, D_A))
    ka = rope(mm(OFF_KA, D_A))
    put(ka_ref, ka)
    kabf_ref[...] = ka.astype(BF16)
    kmean_ref[0] = jnp.mean(ka, axis=0, keepdims=True)
    va = mm(OFF_VA, D_A)
    va_t = va.T
    put(va_ref, va, va_t)
    vat_ref[0] = _vt_with_ones(va_t)
    ga_ref[...] = mm(OFF_GA, D_A)
    qb_ref[...] = rope(mm(OFF_QB, D_B))
    kb = rope(mm(OFF_KB, D_B))
    put(kb_ref, kb)
    kbbf_ref[...] = kb.astype(BF16)
    vb = mm(OFF_VB, D_B)
    vb_t = vb.T
    put(vb_ref, vb, vb_t)
    vbt_ref[0] = _vt_with_ones(vb_t)
    gb_ref[...] = mm(OFF_GB, D_B)
    iq_ref[...] = rope(mm(OFF_IQ, IDX_HEADS * IDX_DIM))
    ik2 = rope(mm(OFF_IK, LANES))
    put(ik_ref, ik2[:, :IDX_DIM], ik2.T[:IDX_DIM] if kv_transposed else None)
    ikbf_ref[...] = ik2.astype(BF16)
    iw_ref[...] = mm(OFF_IW, LANES)
    qm_ref[...] = mm(OFF_QM, D_M)
    gm_ref[...] = mm(OFF_GM, D_M)


def _project(x2d, g, w_pad, tabs, n_tab_tiles, tm, kv_transposed):
    n, d = x2d.shape
    nt = n // tm
    tok = lambda w: pl.BlockSpec((tm, w), lambda i: (i, 0))
    tab = pl.BlockSpec((tm, LANES), lambda i: (i % n_tab_tiles, 0))
    vt = lambda w: pl.BlockSpec((1, w // LANES * VT_ROWS, tm), lambda i: (i, 0, 0))
    vt_shape = lambda w: jax.ShapeDtypeStruct((nt, w // LANES * VT_ROWS, tm), BF16)
    widths = [D_A] * 4 + [D_B] * 4 + [IDX_HEADS * IDX_DIM, IDX_DIM, LANES, D_M, D_M]
    out_shape = [jax.ShapeDtypeStruct((n, w), F32) for w in widths]
    out_specs = [tok(w) for w in widths]
    if kv_transposed:
        for idx in (1, 2, 5, 6, 9):
            w = widths[idx]
            out_shape[idx] = jax.ShapeDtypeStruct((nt // n_tab_tiles, w, n_tab_tiles * tm), F32)
            out_specs[idx] = pl.BlockSpec((1, w, tm), lambda i: (i // n_tab_tiles, 0, i % n_tab_tiles))
    out_shape += [jax.ShapeDtypeStruct((n, D_A), BF16), vt_shape(D_A),
                  jax.ShapeDtypeStruct((n, D_B), BF16), vt_shape(D_B),
                  jax.ShapeDtypeStruct((n, LANES), BF16), jax.ShapeDtypeStruct((nt, 1, D_A), F32)]
    out_specs += [tok(D_A), vt(D_A), tok(D_B), vt(D_B), tok(LANES), pl.BlockSpec((1, 1, D_A), lambda i: (i, 0, 0))]
    return pl.pallas_call(
        functools.partial(_proj_kernel, kv_transposed=kv_transposed),
        grid=(nt,),
        in_specs=[tok(d),
                  pl.BlockSpec((1, d), lambda i: (0, 0)),
                  pl.BlockSpec((d, W_PAD), lambda i: (0, 0)),
                  tab, tab, tab],
        out_specs=out_specs,
        out_shape=out_shape,
        compiler_params=_cparams("parallel"),
        name="proj",
    )(x2d, g, w_pad, *tabs)


def _memkv_kernel(m_ref, g_ref, w_ref, mk_ref, mv_ref):
    x = m_ref[0]
    ms = jnp.mean(x * x, axis=-1, keepdims=True)
    h = (x * lax.rsqrt(ms + EPS) * g_ref[...]).astype(BF16)
    u = jnp.dot(h, w_ref[...], preferred_element_type=F32)
    mk_ref[0] = u[:, :D_M]
    mv_ref[0] = u[:, D_M:]


def _mem_kv(mem, g, w_bf):
    b, nm, d = mem.shape
    return pl.pallas_call(
        _memkv_kernel,
        grid=(b,),
        in_specs=[pl.BlockSpec((1, nm, d), lambda i: (i, 0, 0)),
                  pl.BlockSpec((1, d), lambda i: (0, 0)),
                  pl.BlockSpec((d, 2 * D_M), lambda i: (0, 0))],
        out_specs=[pl.BlockSpec((1, nm, D_M), lambda i: (i, 0, 0))] * 2,
        out_shape=[jax.ShapeDtypeStruct((b, nm, D_M), F32)] * 2,
        compiler_params=_cparams("parallel"),
        name="mem_kv",
    )(mem, g, w_bf)


def _flash_step(s, m, acc_ref, c, vt):
    m_new = jnp.maximum(m, jnp.max(s, axis=0, keepdims=True))
    alpha = jnp.exp2(m - m_new)
    p = jnp.exp2(s - m_new).astype(BF16)
    acc_ref[c] = alpha * acc_ref[c] + jnp.dot(vt, p, preferred_element_type=F32)
    return m_new


def _vt_with_ones(v_t):
    vt = v_t.astype(BF16)
    ones = jnp.ones((SUM_ROWS, v_t.shape[1]), BF16)
    parts = []
    for c in range(v_t.shape[0] // LANES):
        parts += [vt[c * LANES:(c + 1) * LANES], ones]
    return jnp.concatenate(parts, axis=0)


def _moba_kernel(q_ref, k_ref, vt_ref, km_ref, o_ref, qaug_scr, acc_scr, s_scr):
    qi = pl.program_id(1)
    n_pairs = D_A // LANES
    blk_id = lax.broadcasted_iota(jnp.int32, (TILE, LANES), 1)
    blk_f = blk_id.astype(F32)
    past = blk_id < qi
    for c in range(n_pairs):
        cs = slice(c * LANES, (c + 1) * LANES)
        qc = q_ref[0, :, cs]
        kmc = km_ref[0, :, cs]
        biases = []
        for hh in range(2):
            qm = jnp.where(_half_mask(hh), qc, 0.0)
            g = lax.dot_general(qm, kmc, NT, precision=lax.Precision.HIGHEST, preferred_element_type=F32)
            g = jnp.where(past, g, -jnp.inf)
            sel = jnp.zeros((TILE, LANES), jnp.bool_)
            for _ in range(MOBA_TOPK):
                m = jnp.max(g, axis=1, keepdims=True)
                first = jnp.min(jnp.where(g == m, blk_f, 1e9), axis=1, keepdims=True)
                pick = (blk_f == first) & past
                sel = sel | pick
                g = jnp.where(pick, -jnp.inf, g)
            biases.append(jnp.where(sel, 0.0, NEG).astype(BF16))
        qaug_scr[c] = jnp.concatenate([_head_pair_queries(qc), jnp.concatenate(biases, axis=0)], axis=1)

    own0 = pl.multiple_of(qi * TILE, TILE)
    ko = k_ref[0, pl.ds(own0, TILE), :]
    vto = vt_ref[0, qi]
    krow = lax.broadcasted_iota(jnp.int32, (TILE, 2 * TILE), 0)
    qcol = lax.broadcasted_iota(jnp.int32, (TILE, 2 * TILE), 1) % TILE
    ms = []
    for c in range(n_pairs):
        cs = slice(c * LANES, (c + 1) * LANES)
        s = lax.dot_general(ko[:, cs], qaug_scr[c, :, :LANES], NT, preferred_element_type=F32)
        s = jnp.where(krow <= qcol, s, NEG)
        m = jnp.max(s, axis=0, keepdims=True)
        p = jnp.exp2(s - m).astype(BF16)
        ms.append(m)
        acc_scr[c] = jnp.dot(vto[c * VT_ROWS:(c + 1) * VT_ROWS, :], p, preferred_element_type=F32)

    def logits(j, c):
        j0 = pl.multiple_of(j * TILE, TILE)
        onehot = jnp.where(blk_id == j, 1.0, 0.0).astype(BF16)
        k_aug = jnp.concatenate([k_ref[0, pl.ds(j0, TILE), c * LANES:(c + 1) * LANES], onehot], axis=1)
        return lax.dot_general(k_aug, qaug_scr[c], NT, preferred_element_type=F32)

    s_scr[0] = logits(0, 0)

    def body(j, ms):
        vtj = vt_ref[0, j]
        new_m = []
        for c in range(n_pairs):
            if c + 1 < n_pairs:
                s_scr[c + 1] = logits(j, c + 1)
            else:
                s_scr[0] = logits(jnp.minimum(j + 1, qi - 1), 0)
            new_m.append(_flash_step(s_scr[c], ms[c], acc_scr, c, vtj[c * VT_ROWS:(c + 1) * VT_ROWS, :]))
        return tuple(new_m)

    lax.fori_loop(0, qi, body, tuple(ms))
    o_ref[0] = jnp.concatenate([_pair_output(acc_scr[c]) for c in range(n_pairs)], axis=1)


def _moba_prompt(qa, ka_bf, va_t, kmean_pad):
    b, t, _ = qa.shape
    nt = t // TILE
    return pl.pallas_call(
        _moba_kernel,
        grid=(b, nt),
        in_specs=[pl.BlockSpec((1, TILE, D_A), lambda i, j: (i, j, 0)),
                  pl.BlockSpec((1, t, D_A), lambda i, j: (i, 0, 0)),
                  pl.BlockSpec((1, nt, va_t.shape[2], TILE), lambda i, j: (i, 0, 0, 0)),
                  pl.BlockSpec((1, LANES, D_A), lambda i, j: (i, 0, 0))],
        out_specs=pl.BlockSpec((1, TILE, D_A), lambda i, j: (i, j, 0)),
        out_shape=jax.ShapeDtypeStruct((b, t, D_A), F32),
        scratch_shapes=[pltpu.VMEM((D_A // LANES, 2 * TILE, 2 * LANES), BF16),
                        pltpu.VMEM((D_A // LANES, VT_ROWS, 2 * TILE), F32),
                        pltpu.VMEM((D_A // LANES, TILE, 2 * TILE), F32)],
        compiler_params=_cparams("parallel", "arbitrary"),
        name="moba_prompt",
    )(qa, ka_bf, va_t, kmean_pad)


def _dsa_kernel(q_ref, iq_ref, iw_ref, ik_ref, k_ref, vt_ref, o_ref,
                key_scr, hi_scr, lo_scr, rel_a, rel_b, acc_scr, s_scr, *, k_sel, idx_bits):
    qi = pl.program_id(1)
    n_tiles = qi + 1
    n_pairs = D_B // LANES
    kf = float(k_sel)
    krow = lax.broadcasted_iota(jnp.int32, (TILE, TILE), 0)
    q_pos = lax.broadcasted_iota(jnp.int32, (TILE, TILE), 1) + qi * TILE
    key_pos = lambda j: krow + j * TILE

    iq = iq_ref[0]
    iw_t = (iw_ref[0] * (IDX_HEADS ** -0.5)).T
    iqm = [(jnp.where(_half_mask(n % 2), iq[:, (n // 2) * LANES:(n // 2 + 1) * LANES], 0.0)
            * (IDX_DIM ** -0.5)).astype(BF16) for n in range(IDX_HEADS)]

    def relevances(j, out_ref):
        j0 = pl.multiple_of(j * TILE, TILE)
        ikj = ik_ref[0, pl.ds(j0, TILE), :]
        for n in range(IDX_HEADS):
            out_ref[n] = lax.dot_general(ikj, iqm[n], NT, preferred_element_type=F32)

    def store_keys(j, rel_ref):
        score = jnp.zeros((TILE, TILE), F32)
        for n in range(IDX_HEADS):
            score = score + iw_t[n:n + 1, :] * jnp.maximum(rel_ref[n], 0.0)
        key = jnp.where(key_pos(j) > q_pos, jnp.int32(INT_MIN), _sortable(score))
        key_scr[j] = key
        hi_scr[j] = lax.shift_right_arithmetic(key, 16).astype(jnp.int16)
        lo_scr[j] = ((key & 0xFFFF) + I16_MIN).astype(jnp.int16)

    last = n_tiles - 1
    relevances(0, rel_a)

    def score_body(i, carry):
        j = 2 * i
        relevances(jnp.minimum(j + 1, last), rel_b)
        store_keys(j, rel_a)
        relevances(jnp.minimum(j + 2, last), rel_a)
        store_keys(jnp.minimum(j + 1, last), rel_b)
        return carry

    n_pairs_kt = (n_tiles + 1) // 2
    lax.fori_loop(0, n_pairs_kt, score_body, 0)

    @pl.when(n_tiles % 2 == 1)
    def _():
        hi_scr[n_tiles] = jnp.full((TILE, TILE), I16_MIN, jnp.int16)
        lo_scr[n_tiles] = jnp.full((TILE, TILE), I16_MIN, jnp.int16)

    def count16(plane, pred):
        def body(i, acc):
            for jj in (2 * i, 2 * i + 1):
                w = jnp.where(pred(plane[jj]), jnp.int16(1), jnp.int16(0))
                for r in range(TILE // 64):
                    acc = acc + w[r * 64:(r + 1) * 64]
            return acc
        acc = lax.fori_loop(0, n_pairs_kt, body, jnp.zeros((64, TILE), jnp.int16))
        return jnp.sum(acc.astype(jnp.int32).astype(F32), axis=0, keepdims=True)

    def count32(pred):
        def body(j, acc):
            w = jnp.where(pred(key_scr[j], j), 1.0, 0.0)
            for r in range(TILE // 64):
                acc = acc + w[r * 64:(r + 1) * 64]
            return acc
        acc = lax.fori_loop(0, n_tiles, body, jnp.zeros((64, TILE), F32))
        return jnp.sum(acc, axis=0, keepdims=True)

    def search16(plane, target):
        zero = jnp.zeros((1, TILE), jnp.int32)
        total = (n_tiles * TILE).astype(F32)
        c0 = count16(plane, lambda x: x >= zero.astype(jnp.int16))
        ok0 = c0 >= target
        init = (jnp.where(ok0, zero, jnp.int32(I16_MIN)), jnp.where(ok0, c0, total))

        def bit(i, carry):
            thr, cnt = carry
            cand = thr | lax.shift_left(jnp.int32(1), 14 - i)
            c = count16(plane, lambda x: x >= cand.astype(jnp.int16))
            ok = c >= target
            return jnp.where(ok, cand, thr), jnp.where(ok, c, cnt)

        return lax.fori_loop(0, 15, bit, init)

    thr_hi, cnt_hi = search16(hi_scr, kf)
    thr_hi16 = thr_hi.astype(jnp.int16)
    above = count16(hi_scr, lambda x: x > thr_hi16)

    def mask_lo(i, carry):
        for jj in (2 * i, 2 * i + 1):
            lo_scr[jj] = jnp.where(hi_scr[jj] == thr_hi16, lo_scr[jj], jnp.int16(I16_MIN))
        return carry

    lax.fori_loop(0, n_pairs_kt, mask_lo, 0)
    thr_lo, cnt_lo = search16(lo_scr, kf - above)
    thr = thr_hi * 65536 + (thr_lo - I16_MIN)
    cnt_ge = above + jnp.where(thr_lo > I16_MIN, cnt_lo, cnt_hi - above)
    has_ties = jnp.max(cnt_ge) > kf

    @pl.when(jnp.logical_not(has_ties))
    def _():
        floor = jnp.maximum(thr, jnp.int32(INT_MIN + 1))

        def to_bias(j, carry):
            key_scr[j] = lax.bitcast_convert_type(jnp.where(key_scr[j] >= floor, 0.0, NEG).astype(F32), jnp.int32)
            return carry

        lax.fori_loop(0, n_tiles, to_bias, 0)

    @pl.when(has_ties)
    def _():
        need = kf - count32(lambda k, j: k > thr)

        def cbit(i, cpos):
            cand = cpos | lax.shift_left(jnp.int32(1), idx_bits - 1 - i)
            cnt = count32(lambda k, j: (k == thr) & (key_pos(j) < cand))
            return jnp.where(cnt < need, cand, cpos)

        cstar = lax.fori_loop(0, idx_bits, cbit, jnp.zeros((1, TILE), jnp.int32))

        def to_bias(j, carry):
            k = key_scr[j]
            kp = key_pos(j)
            keep = ((k > thr) | ((k == thr) & (kp <= cstar))) & (kp <= q_pos)
            key_scr[j] = lax.bitcast_convert_type(jnp.where(keep, 0.0, NEG).astype(F32), jnp.int32)
            return carry

        lax.fori_loop(0, n_tiles, to_bias, 0)

    q_pairs = [_head_pair_queries(q_ref[0, :, c * LANES:(c + 1) * LANES]) for c in range(n_pairs)]
    for c in range(n_pairs):
        acc_scr[c] = jnp.zeros((VT_ROWS, 2 * TILE), F32)

    def logits(j, c):
        j0 = pl.multiple_of(j * TILE, TILE)
        return lax.dot_general(k_ref[0, pl.ds(j0, TILE), c * LANES:(c + 1) * LANES], q_pairs[c], NT,
                               preferred_element_type=F32)

    s_scr[0] = logits(0, 0)

    def body(j, ms):
        vtj = vt_ref[0, j]
        bias = lax.bitcast_convert_type(key_scr[j], F32)
        bias2 = jnp.concatenate([bias, bias], axis=1)
        new_m = []
        for c in range(n_pairs):
            if c + 1 < n_pairs:
                s_scr[c + 1] = logits(j, c + 1)
            else:
                s_scr[0] = logits(jnp.minimum(j + 1, n_tiles - 1), 0)
            new_m.append(_flash_step(s_scr[c] + bias2, ms[c], acc_scr, c, vtj[c * VT_ROWS:(c + 1) * VT_ROWS, :]))
        return tuple(new_m)

    lax.fori_loop(0, n_tiles, body, tuple(jnp.full((1, 2 * TILE), NEG, F32) for _ in range(n_pairs)))
    o_ref[0] = jnp.concatenate([_pair_output(acc_scr[c]) for c in range(n_pairs)], axis=1)


def _dsa_prompt(qb, iq, iw, ik_bf, kb_bf, vb_t):
    b, t, _ = qb.shape
    nt = t // TILE
    k_sel = min(IDX_TOPK, t // 4)
    kern = functools.partial(_dsa_kernel, k_sel=k_sel, idx_bits=max(1, (t - 1).bit_length()))
    res = lambda w: pl.BlockSpec((1, t, w), lambda i, j: (i, 0, 0))
    til = lambda w: pl.BlockSpec((1, TILE, w), lambda i, j: (i, j, 0))
    return pl.pallas_call(
        kern,
        grid=(b, nt),
        in_specs=[til(D_B), til(IDX_HEADS * IDX_DIM), til(LANES), res(LANES), res(D_B),
                  pl.BlockSpec((1, nt, vb_t.shape[2], TILE), lambda i, j: (i, 0, 0, 0))],
        out_specs=til(D_B),
        out_shape=jax.ShapeDtypeStruct((b, t, D_B), F32),
        scratch_shapes=[pltpu.VMEM((nt, TILE, TILE), jnp.int32),
                        pltpu.VMEM((nt + nt % 2, TILE, TILE), jnp.int16),
                        pltpu.VMEM((nt + nt % 2, TILE, TILE), jnp.int16),
                        pltpu.VMEM((IDX_HEADS, TILE, TILE), F32),
                        pltpu.VMEM((IDX_HEADS, TILE, TILE), F32),
                        pltpu.VMEM((D_B // LANES, VT_ROWS, 2 * TILE), F32),
                        pltpu.VMEM((D_B // LANES, TILE, 2 * TILE), F32)],
        compiler_params=_cparams("parallel", "arbitrary"),
        name="dsa_prompt",
    )(qb, iq, iw, ik_bf, kb_bf, vb_t)


def _merge_kernel(x_ref, oa_ref, ga_ref, ob_ref, gb_ref, qm_ref, gm_ref, mk_ref, mv_ref, wo_ref, fg_ref, y_ref,
                  *, final):
    lane = lax.broadcasted_iota(jnp.int32, (1, LANES), 1)
    mk = mk_ref[0].astype(BF16)
    mv = mv_ref[0].astype(BF16)
    oms = []
    for c in range(D_M // LANES):
        cs = slice(c * LANES, (c + 1) * LANES)
        qc = qm_ref[0, :, cs]
        heads = []
        for hh in range(2):
            q_bf = (jnp.where(_half_mask(hh), qc, 0.0) * (HEAD_DIM ** -0.5)).astype(BF16)
            s = lax.dot_general(q_bf, mk[:, cs], NT, preferred_element_type=F32)
            m = jnp.max(s, axis=1, keepdims=True)
            p = jnp.exp(s - m)
            l = jnp.sum(p, axis=1, keepdims=True)
            heads.append(jnp.dot(p.astype(BF16), mv[:, cs], preferred_element_type=F32) / l)
        oms.append(jnp.where(lane < HEAD_DIM, heads[0], heads[1]))
    om = jnp.concatenate(oms, axis=1)
    silu = lambda g: g / (1.0 + jnp.exp(-g))
    ymix = jnp.concatenate([oa_ref[0] * silu(ga_ref[0]), ob_ref[0] * silu(gb_ref[0]), om * silu(gm_ref[0])],
                           axis=1).astype(BF16)
    z = x_ref[0] + jnp.dot(ymix, wo_ref[...], preferred_element_type=F32)
    if final:
        ms = jnp.mean(z * z, axis=-1, keepdims=True)
        z = z * lax.rsqrt(ms + EPS) * fg_ref[...]
    y_ref[0] = z


def _merge(x, oa, ga, ob, gb, qm, gm, mk, mv, wo_bf, fg, tm, final):
    b, t, d = x.shape
    til = lambda w: pl.BlockSpec((1, tm, w), lambda i, j: (i, j, 0))
    memspec = pl.BlockSpec((1, N_MEM, D_M), lambda i, j: (i, 0, 0))
    return pl.pallas_call(
        functools.partial(_merge_kernel, final=final),
        grid=(b, t // tm),
        in_specs=[til(d), til(D_A), til(D_A), til(D_B), til(D_B), til(D_M), til(D_M), memspec, memspec,
                  pl.BlockSpec((d, d), lambda i, j: (0, 0)),
                  pl.BlockSpec((1, d), lambda i, j: (0, 0))],
        out_specs=til(d),
        out_shape=jax.ShapeDtypeStruct((b, t, d), F32),
        compiler_params=_cparams("parallel", "parallel"),
        name="merge",
    )(x, oa, ga, ob, gb, qm, gm, mk, mv, wo_bf, fg)


def _expand_heads(q, n_heads):
    t, w = q.shape
    qt = jnp.concatenate([q] * n_heads, axis=0)
    r = lax.broadcasted_iota(jnp.int32, (n_heads * t, w), 0)
    l = lax.broadcasted_iota(jnp.int32, (n_heads * t, w), 1)
    keep = (l >= (r // t) * HEAD_DIM) & (l < (r // t + 1) * HEAD_DIM)
    return jnp.where(keep, qt, 0.0)


def _collapse_heads(o, n_heads, t):
    w = o.shape[1]
    l = lax.broadcasted_iota(jnp.int32, (t, w), 1)
    out = jnp.zeros((t, w), F32)
    for h in range(n_heads):
        keep = (l >= h * HEAD_DIM) & (l < (h + 1) * HEAD_DIM)
        out = out + jnp.where(keep, o[h * t:(h + 1) * t, :], 0.0)
    return out


def _pages_t(refs):
    pages = [r[0, 0].reshape(r.shape[2] * r.shape[3], r.shape[4]) for r in refs]
    return pages[0] if len(pages) == 1 else jnp.concatenate(pages, axis=1)


def _cache_view(cache):
    return jnp.transpose(cache, (0, 1, 3, 4, 2))


def _values_with_ones(vt):
    return jnp.concatenate([vt.astype(BF16), jnp.ones((SUM_ROWS, vt.shape[1]), BF16)], axis=0)


def _softmax_pv(s_scr, vt_scr, s_new, v_new):
    nsteps = s_scr.shape[0]
    d = v_new.shape[1]
    m = jnp.max(jnp.max(s_scr[...], axis=0), axis=1, keepdims=True)
    m = jnp.maximum(m, jnp.max(s_new, axis=1, keepdims=True))
    p_new = jnp.exp(s_new - m)
    l_new = jnp.sum(p_new, axis=1, keepdims=True)
    acc_new = jnp.dot(p_new.astype(BF16), v_new, preferred_element_type=F32)

    def body(i, acc):
        p = jnp.exp(s_scr[i] - m).astype(BF16)
        return acc + lax.dot_general(p, vt_scr[i], NT, preferred_element_type=F32)

    acc = lax.fori_loop(0, nsteps, body, jnp.zeros((s_new.shape[0], d + SUM_ROWS), F32),
                        unroll=nsteps if nsteps <= 8 else 8)
    return (acc[:, :d] + acc_new) / (acc[:, d:d + 1] + l_new)


def _paged_specs(layer, block_tail, per_step):
    zeros = (0,) * len(block_tail)
    return [pl.BlockSpec((1, 1) + block_tail, lambda s, j, pt, i=i: (layer, pt[s, j * per_step + i]) + zeros)
            for i in range(per_step)]


def _moba_dec_kernel(pt_ref, q_ref, *rest, ppb, bps):
    n_pg = ppb * bps
    k_refs, v_refs = rest[:n_pg], rest[n_pg:2 * n_pg]
    kn_ref, vn_ref, o_ref, s_scr, vt_scr, g_scr, qxt_scr = rest[2 * n_pg:]
    j = pl.program_id(1)
    nsteps = pl.num_programs(1)
    t = q_ref.shape[1]
    rows = H_A * t
    blk_w = s_scr.shape[2] // bps
    qx = _expand_heads(q_ref[0], H_A)
    qx_bf = (qx * (HEAD_DIM ** -0.5)).astype(BF16)

    @pl.when(j == 0)
    def _():
        qxt_scr[...] = jnp.concatenate([qx, jnp.zeros((LANES - rows, D_A), F32)], axis=0).T

    kt = _pages_t(k_refs)
    s_scr[j] = jnp.dot(qx_bf, kt.astype(BF16), preferred_element_type=F32)
    vt_scr[j] = _values_with_ones(_pages_t(v_refs))
    for b in range(bps):
        kmean = jnp.mean(kt[:, b * blk_w:(b + 1) * blk_w], axis=1, keepdims=True)
        g_scr[pl.ds(j * bps + b, 1), :] = jnp.sum(qxt_scr[...] * kmean, axis=0, keepdims=True)

    @pl.when(j == nsteps - 1)
    def _():
        npad = kn_ref.shape[1]
        r = lax.broadcasted_iota(jnp.int32, (rows, npad), 0)
        cidx = lax.broadcasted_iota(jnp.int32, (rows, npad), 1)
        s_new = lax.dot_general(qx_bf, kn_ref[0].astype(BF16), NT, preferred_element_type=F32)
        s_new = jnp.where(cidx <= (r % t), s_new, NEG)

        n_blocks = g_scr.shape[0]
        g = g_scr[...]
        bidx = lax.broadcasted_iota(jnp.int32, g.shape, 0).astype(F32)
        sel = jnp.zeros(g.shape, jnp.bool_)
        for _ in range(min(MOBA_TOPK, n_blocks)):
            mx = jnp.max(g, axis=0, keepdims=True)
            first = jnp.min(jnp.where(g == mx, bidx, 1e9), axis=0, keepdims=True)
            pick = bidx == first
            sel = sel | pick
            g = jnp.where(pick, -jnp.inf, g)
        bias_t = jnp.where(sel, 0.0, NEG).T
        for i in range(s_scr.shape[0]):
            cols = [jnp.broadcast_to(bias_t[:rows, i * bps + b:i * bps + b + 1], (rows, blk_w)) for b in range(bps)]
            s_scr[i] = s_scr[i] + (cols[0] if bps == 1 else jnp.concatenate(cols, axis=1))
        o_ref[0] = _collapse_heads(_softmax_pv(s_scr, vt_scr, s_new, vn_ref[0].astype(BF16)), H_A, t)


def _moba_decode(layer, page_table, qa, cache_kt, cache_vt, kn_pad, vn_pad):
    nseq, t, _ = qa.shape
    n_pages = page_table.shape[1]
    page_size = cache_kt.shape[4]
    ppb = MOBA_BLOCK // page_size
    nb = n_pages // ppb
    bps = max(1, KV_PAGES_PER_STEP // ppb)
    while nb % bps:
        bps //= 2
    rows = H_A * t
    seq = lambda w, r: pl.BlockSpec((1, r, w), lambda s, j, pt: (s, 0, 0))
    kspecs = _paged_specs(layer, cache_kt.shape[2:], ppb * bps)
    grid_spec = pltpu.PrefetchScalarGridSpec(
        num_scalar_prefetch=1,
        grid=(nseq, nb // bps),
        in_specs=[seq(D_A, t)] + kspecs + kspecs + [seq(D_A, kn_pad.shape[1])] * 2,
        out_specs=seq(D_A, t),
        scratch_shapes=[pltpu.VMEM((nb // bps, rows, bps * MOBA_BLOCK), F32),
                        pltpu.VMEM((nb // bps, D_A + SUM_ROWS, bps * MOBA_BLOCK), BF16),
                        pltpu.VMEM((nb, LANES), F32),
                        pltpu.VMEM((D_A, LANES), F32)],
    )
    assert rows <= LANES
    return pl.pallas_call(
        functools.partial(_moba_dec_kernel, ppb=ppb, bps=bps),
        grid_spec=grid_spec,
        out_shape=jax.ShapeDtypeStruct((nseq, t, D_A), F32),
        compiler_params=_cparams("parallel", "arbitrary"),
        name="moba_decode",
    )(page_table, qa, *([cache_kt] * (ppb * bps)), *([cache_vt] * (ppb * bps)), kn_pad, vn_pad)


def _idx_dec_kernel(pt_ref, iq_ref, iw_ref, *rest, per_step, k_sel, idx_bits, past, t):
    ik_refs = rest[:per_step]
    ikn_ref, bias_ref, key_scr = rest[per_step:]
    j = pl.program_id(1)
    nsteps = pl.num_programs(1)
    width = key_scr.shape[2]
    iq_bf = iq_ref[0].astype(BF16)
    iw = iw_ref[0][:, :1]

    def scores(rel):
        wrel = iw * jnp.maximum(rel * (IDX_DIM ** -0.5), 0.0)
        sc = wrel[0:t]
        for n in range(1, IDX_HEADS):
            sc = sc + wrel[n * t:(n + 1) * t]
        return sc

    ik_t = jnp.concatenate([r[0, 0] for r in ik_refs], axis=1)
    key_scr[j] = _sortable(scores(jnp.dot(iq_bf, ik_t.astype(BF16), preferred_element_type=F32)))

    @pl.when(j == nsteps - 1)
    def _():
        kf = float(k_sel)
        npad = ikn_ref.shape[1]
        r_new = lax.broadcasted_iota(jnp.int32, (t, npad), 0)
        c_new = lax.broadcasted_iota(jnp.int32, (t, npad), 1)
        rel_new = lax.dot_general(iq_bf, ikn_ref[0].astype(BF16), NT, preferred_element_type=F32)
        key_new = jnp.where(c_new <= r_new, _sortable(scores(rel_new)), jnp.int32(INT_MIN))
        tail = jnp.full((t, width - npad), jnp.int32(INT_MIN))
        key_scr[nsteps] = jnp.concatenate([key_new, tail], axis=1)

        keys = key_scr[...]
        pos = (lax.broadcasted_iota(jnp.int32, keys.shape, 0) * width
               + lax.broadcasted_iota(jnp.int32, keys.shape, 2))

        def count(pred):
            w = jnp.sum(jnp.where(pred, 1.0, 0.0), axis=0)
            acc = w[:, :LANES]
            for s in range(1, width // LANES):
                acc = acc + w[:, s * LANES:(s + 1) * LANES]
            return jnp.sum(acc, axis=1, keepdims=True)[None]

        zero = jnp.zeros((1, t, 1), jnp.int32)
        thr = jnp.where(count(keys >= zero) >= kf, zero, jnp.int32(INT_MIN))

        def bit(i, thr):
            cand = thr | lax.shift_left(jnp.int32(1), 30 - i)
            return jnp.where(count(keys >= cand) >= kf, cand, thr)

        thr = lax.fori_loop(0, 31, bit, thr)
        need = kf - count(keys > thr)

        def cbit(i, cpos):
            cand = cpos | lax.shift_left(jnp.int32(1), idx_bits - 1 - i)
            return jnp.where(count((keys == thr) & (pos < cand)) < need, cand, cpos)

        cstar = lax.fori_loop(0, idx_bits, cbit, zero)
        keep = (keys > thr) | ((keys == thr) & (pos <= cstar))
        bias_ref[0] = jnp.where(keep, 0.0, NEG).astype(F32)


def _idx_decode(layer, page_table, iq_rows, iw_rows, cache_ik, ikn_pad, t):
    nseq = iq_rows.shape[0]
    n_pages = page_table.shape[1]
    page_size = cache_ik.shape[3]
    per_step = IDX_PAGES_PER_STEP
    while n_pages % per_step:
        per_step //= 2
    nsteps = n_pages // per_step
    width = per_step * page_size
    past = n_pages * page_size
    k_sel = min(IDX_TOPK, (past + t) // 4)
    idx_bits = max(1, ((nsteps + 1) * width - 1).bit_length())
    seq = lambda r, w: pl.BlockSpec((1, r, w), lambda s, j, pt: (s, 0, 0))
    grid_spec = pltpu.PrefetchScalarGridSpec(
        num_scalar_prefetch=1,
        grid=(nseq, nsteps),
        in_specs=[seq(IDX_HEADS * t, IDX_DIM), seq(IDX_HEADS * t, LANES)]
        + _paged_specs(layer, cache_ik.shape[2:], per_step) + [seq(ikn_pad.shape[1], IDX_DIM)],
        out_specs=pl.BlockSpec((1, nsteps + 1, t, width), lambda s, j, pt: (s, 0, 0, 0)),
        scratch_shapes=[pltpu.VMEM((nsteps + 1, t, width), jnp.int32)],
    )
    kern = functools.partial(_idx_dec_kernel, per_step=per_step, k_sel=k_sel, idx_bits=idx_bits, past=past, t=t)
    return pl.pallas_call(
        kern,
        grid_spec=grid_spec,
        out_shape=jax.ShapeDtypeStruct((nseq, nsteps + 1, t, width), F32),
        compiler_params=_cparams("parallel", "arbitrary"),
        name="idx_decode",
    )(page_table, iq_rows, iw_rows, *([cache_ik] * per_step), ikn_pad)


def _dsa_dec_kernel(pt_ref, q_ref, b_ref, bn_ref, *rest, pps):
    k_refs, v_refs = rest[:pps], rest[pps:2 * pps]
    kn_ref, vn_ref, o_ref, s_scr, vt_scr = rest[2 * pps:]
    j = pl.program_id(1)
    nsteps = pl.num_programs(1)
    t = q_ref.shape[1]
    qx_bf = (_expand_heads(q_ref[0], H_B) * (HEAD_DIM ** -0.5)).astype(BF16)

    kt = _pages_t(k_refs).astype(BF16)
    s_scr[j] = jnp.dot(qx_bf, kt, preferred_element_type=F32) + jnp.concatenate([b_ref[0, 0]] * H_B, axis=0)
    vt_scr[j] = _values_with_ones(_pages_t(v_refs))

    @pl.when(j == nsteps - 1)
    def _():
        s_new = lax.dot_general(qx_bf, kn_ref[0].astype(BF16), NT, preferred_element_type=F32)
        s_new = s_new + jnp.concatenate([bn_ref[0, 0]] * H_B, axis=0)
        o_ref[0] = _collapse_heads(_softmax_pv(s_scr, vt_scr, s_new, vn_ref[0].astype(BF16)), H_B, t)


def _dsa_decode(layer, page_table, qb, bias, cache_kt, cache_vt, kn_pad, vn_pad):
    nseq, t, _ = qb.shape
    n_pages = page_table.shape[1]
    page_size = cache_kt.shape[4]
    pps = KV_PAGES_PER_STEP
    while n_pages % pps or bias.shape[3] % (pps * page_size):
        pps //= 2
    width = bias.shape[3]
    chunk = pps * page_size
    assert n_pages % pps == 0 and width % chunk == 0
    per_w = width // chunk
    rows = H_B * t
    npad = kn_pad.shape[1]
    seq = lambda w, r: pl.BlockSpec((1, r, w), lambda s, j, pt: (s, 0, 0))
    bspec = pl.BlockSpec((1, 1, t, chunk), lambda s, j, pt: (s, j // per_w, 0, j % per_w))
    bnew = pl.BlockSpec((1, 1, t, npad), lambda s, j, pt: (s, bias.shape[1] - 1, 0, 0))
    kspecs = _paged_specs(layer, cache_kt.shape[2:], pps)
    grid_spec = pltpu.PrefetchScalarGridSpec(
        num_scalar_prefetch=1,
        grid=(nseq, n_pages // pps),
        in_specs=[seq(D_B, t), bspec, bnew] + kspecs + kspecs + [seq(D_B, npad)] * 2,
        out_specs=seq(D_B, t),
        scratch_shapes=[pltpu.VMEM((n_pages // pps, rows, chunk), F32),
                        pltpu.VMEM((n_pages // pps, D_B + SUM_ROWS, chunk), BF16)],
    )
    return pl.pallas_call(
        functools.partial(_dsa_dec_kernel, pps=pps),
        grid_spec=grid_spec,
        out_shape=jax.ShapeDtypeStruct((nseq, t, D_B), F32),
        compiler_params=_cparams("parallel", "arbitrary"),
        name="dsa_decode",
    )(page_table, qb, bias, bias, *([cache_kt] * pps), *([cache_vt] * pps), kn_pad, vn_pad)


def _rope_tables(pos):
    half = ROT_DIM // 2
    inv = jnp.power(ROPE_THETA, -jnp.arange(half, dtype=F32) / half)
    ang = pos.astype(F32)[:, None] * inv[None, :]
    cos, sin = jnp.cos(ang), jnp.sin(ang)
    n = pos.shape[0]
    ones = jnp.ones((n, HEAD_DIM - ROT_DIM), F32)
    zeros = jnp.zeros((n, HEAD_DIM - ROT_DIM), F32)
    zh = jnp.zeros((n, half), F32)
    tc = jnp.concatenate([cos, cos, ones], axis=1)
    ts1 = jnp.concatenate([-sin, zh, zeros], axis=1)
    ts2 = jnp.concatenate([zh, sin, zeros], axis=1)
    dup = lambda a: jnp.concatenate([a, a], axis=1)
    return dup(tc), dup(ts1), dup(ts2)


def _pad_w_in(w):
    d = w.shape[0]
    main = w[:, :8 * _G]
    o = 8 * _G
    iq = w[:, o:o + 256]
    ik = w[:, o + 256:o + 320]
    iw = w[:, o + 320:o + 324]
    qm = w[:, o + 324:o + 324 + D_M]
    gm = w[:, o + 324 + D_M:o + 324 + 2 * D_M]
    iw_pad = jnp.concatenate([iw, jnp.zeros((d, LANES - IDX_HEADS), w.dtype)], axis=1)
    return jnp.concatenate([main, iq, ik, ik, iw_pad, qm, gm], axis=1).astype(BF16)


def kernel(x_prompt, x_sample, mem_prompt, cache_k_a, cache_v_a, cache_k_b, cache_v_b, cache_k_idx,
           cache_mem_k, cache_mem_v, page_table, norm_g, w_in, mem_norm_g, w_mem_kv, w_out, final_norm_g):
    depth = w_in.shape[0]
    b, t, d = x_prompt.shape
    nseq, ts, _ = x_sample.shape
    n_pages = page_table.shape[1]
    page_size = cache_k_a.shape[2]
    past = n_pages * page_size
    assert t % TILE == 0 and t // TILE <= LANES and page_size == LANES
    assert past % MOBA_BLOCK == 0 and ts <= LANES and ts % 8 == 0

    tabs_p = _rope_tables(jnp.arange(t, dtype=jnp.int32))
    pos_s = past + jnp.arange(ts, dtype=jnp.int32)
    tabs_s = tuple(jnp.tile(a, (nseq, 1)) for a in _rope_tables(pos_s))
    fg = final_norm_g.reshape(1, d)
    nbp = t // TILE

    xp, xs = x_prompt, x_sample
    outs = [[] for _ in range(12)]
    for l in range(depth):
        w_pad = _pad_w_in(w_in[l])
        g = norm_g[l].reshape(1, d)
        wo_bf = w_out[l].astype(BF16)
        last = l == depth - 1

        (qa, ka, va, ga, qb, kb, vb, gb, iq, ik, iw, qm, gm,
         ka_bf, va_t, kb_bf, vb_t, ik_bf, kmean) = _project(xp.reshape(b * t, d), g, w_pad, tabs_p, nbp, TILE, True)
        r3 = lambda a: a.reshape(b, t, a.shape[-1])
        r4 = lambda a: a.reshape(b, nbp, a.shape[-2], TILE)
        mk, mv = _mem_kv(mem_prompt, mem_norm_g[l].reshape(1, d), w_mem_kv[l].astype(BF16))
        kmean_pad = jnp.pad(kmean.reshape(b, nbp, D_A), ((0, 0), (0, LANES - nbp), (0, 0)))
        oa = _moba_prompt(r3(qa), r3(ka_bf), r4(va_t), kmean_pad)
        ob = _dsa_prompt(r3(qb), r3(iq).astype(BF16), r3(iw), r3(ik_bf), r3(kb_bf), r4(vb_t))
        xp = _merge(xp, oa, r3(ga), ob, r3(gb), r3(qm), r3(gm), mk, mv, wo_bf, fg, TILE, last)
        for lst, a, h in ((outs[0], ka, H_A), (outs[1], va, H_A), (outs[2], kb, H_B), (outs[3], vb, H_B)):
            lst.append(jnp.transpose(a.reshape(b, h, HEAD_DIM, t), (0, 3, 1, 2)))
        outs[4].append(jnp.transpose(ik, (0, 2, 1)))
        outs[5].append(mk.reshape(b, N_MEM, H_M, HEAD_DIM))
        outs[6].append(mv.reshape(b, N_MEM, H_M, HEAD_DIM))

        n_s = nseq * ts
        tm_s = n_s if n_s <= TILE else TILE
        (qa, ka, va, ga, qb, kb, vb, gb, iq, ik, iw, qm, gm,
         _, _, _, _, _, _) = _project(xs.reshape(n_s, d), g, w_pad, tabs_s, n_s // tm_s, tm_s, False)
        s3 = lambda a: a.reshape(nseq, ts, a.shape[-1])
        padn = lambda a: jnp.pad(s3(a), ((0, 0), (0, LANES - ts), (0, 0)))
        oa = _moba_decode(l, page_table, s3(qa), _cache_view(cache_k_a), _cache_view(cache_v_a), padn(ka), padn(va))
        iq_rows = s3(iq).reshape(nseq, ts, IDX_HEADS, IDX_DIM).transpose(0, 2, 1, 3).reshape(
            nseq, IDX_HEADS * ts, IDX_DIM)
        iw_rows = (s3(iw)[:, :, :IDX_HEADS] * (IDX_HEADS ** -0.5)).transpose(0, 2, 1).reshape(
            nseq, IDX_HEADS * ts, 1)
        iw_rows = jnp.broadcast_to(iw_rows, (nseq, IDX_HEADS * ts, LANES))
        bias = _idx_decode(l, page_table, iq_rows, iw_rows, jnp.transpose(cache_k_idx, (0, 1, 3, 2)), padn(ik), ts)
        ob = _dsa_decode(l, page_table, s3(qb), bias, _cache_view(cache_k_b), _cache_view(cache_v_b),
                         padn(kb), padn(vb))
        mks = cache_mem_k[l].reshape(nseq, N_MEM, D_M)
        mvs = cache_mem_v[l].reshape(nseq, N_MEM, D_M)
        xs = _merge(xs, oa, s3(ga), ob, s3(gb), s3(qm), s3(gm), mks, mvs, wo_bf, fg, ts, last)
        for lst, a, shp in ((outs[7], ka, (nseq, ts, H_A, HEAD_DIM)), (outs[8], va, (nseq, ts, H_A, HEAD_DIM)),
                            (outs[9], kb, (nseq, ts, H_B, HEAD_DIM)), (outs[10], vb, (nseq, ts, H_B, HEAD_DIM)),
                            (outs[11], ik, (nseq, ts, IDX_DIM))):
            lst.append(a.reshape(shp))

    stk = [jnp.stack(o) for o in outs]
    return (xp, xs, stk[0], stk[1], stk[2], stk[3], stk[4], stk[5], stk[6],
            stk[7], stk[8], stk[9], stk[10], stk[11])
```

```python
import functools

import jax
import jax.numpy as jnp
from jax import lax
from jax.experimental import pallas as pl
from jax.experimental.pallas import tpu as pltpu

HEAD_DIM = 64
H_A = 6
H_B = 6
H_M = 4
D_A = H_A * HEAD_DIM
D_B = H_B * HEAD_DIM
D_M = H_M * HEAD_DIM
N_MEM = 256
MOBA_BLOCK = 256
MOBA_TOPK = 3
IDX_HEADS = 4
IDX_DIM = 64
IDX_TOPK = 256
ROPE_THETA = 500000.0
ROT_DIM = HEAD_DIM // 4
EPS = 1e-6

LANES = 128
TILE = 256
NEG = -1e30
LOG2E = 1.4426950408889634
SUM_ROWS = 16
VT_ROWS = LANES + SUM_ROWS
INT_MIN = -(2 ** 31)
I16_MIN = -(2 ** 15)
NO_POS = 2 ** 15 - 1
VMEM_LIMIT = 60 * 1024 * 1024
IDX_PAGES_PER_STEP = 32
KV_PAGES_PER_STEP = 16

_G = D_A
OFF_QA, OFF_KA, OFF_VA, OFF_GA = 0, _G, 2 * _G, 3 * _G
OFF_QB, OFF_KB, OFF_VB, OFF_GB = 4 * _G, 5 * _G, 6 * _G, 7 * _G
OFF_IQ = 8 * _G
OFF_IK = OFF_IQ + 256
OFF_IW = OFF_IK + LANES
OFF_QM = OFF_IW + LANES
OFF_GM = OFF_QM + D_M
W_PAD = OFF_GM + D_M

F32 = jnp.float32
BF16 = jnp.bfloat16
NT = (((1,), (1,)), ((), ()))


def _cparams(*sem):
    return pltpu.CompilerParams(dimension_semantics=sem, vmem_limit_bytes=VMEM_LIMIT)


def _half_mask(hh):
    lane = lax.broadcasted_iota(jnp.int32, (1, LANES), 1)
    return (lane >= HEAD_DIM * hh) & (lane < HEAD_DIM * (hh + 1))


def _sortable(score):
    bits = lax.bitcast_convert_type(score, jnp.int32)
    key = bits ^ (lax.shift_right_arithmetic(bits, 31) & jnp.int32(0x7FFFFFFF))
    return jnp.where(score == 0.0, jnp.int32(0), key)


def _head_pair_queries(qc):
    scaled = qc * (HEAD_DIM ** -0.5 * LOG2E)
    return jnp.concatenate([jnp.where(_half_mask(0), scaled, 0.0), jnp.where(_half_mask(1), scaled, 0.0)],
                           axis=0).astype(BF16)


def _pair_output(acc):
    t = acc.shape[1] // 2
    o = acc[:LANES] / acc[LANES:LANES + 1]
    d = lax.broadcasted_iota(jnp.int32, (LANES, t), 0)
    return jnp.where(d < HEAD_DIM, o[:, :t], o[:, t:]).T


def _proj_kernel(x_ref, g_ref, w_ref, tc_ref, ts1_ref, ts2_ref,
                 qa_ref, ka_ref, va_ref, ga_ref, qb_ref, kb_ref, vb_ref, gb_ref,
                 iq_ref, ik_ref, iw_ref, qm_ref, gm_ref,
                 kabf_ref, vat_ref, kbbf_ref, vbt_ref, ikbf_ref, kmean_ref, *, kv_transposed):
    x = x_ref[...]
    ms = jnp.mean(x * x, axis=-1, keepdims=True)
    h = (x * lax.rsqrt(ms + EPS) * g_ref[...]).astype(BF16)
    tc, ts1, ts2 = tc_ref[...], ts1_ref[...], ts2_ref[...]

    def mm(off, width):
        return jnp.dot(h, w_ref[:, off:off + width], preferred_element_type=F32)

    def rope(u):
        outs = []
        for c in range(u.shape[1] // LANES):
            uc = u[:, c * LANES:(c + 1) * LANES]
            outs.append(uc * tc + pltpu.roll(uc, LANES - ROT_DIM // 2, 1) * ts1
                        + pltpu.roll(uc, ROT_DIM // 2, 1) * ts2)
        return outs[0] if len(outs) == 1 else jnp.concatenate(outs, axis=1)

    def put(ref, val, val_t=None):
        if kv_transposed:
            ref[0] = val.T if val_t is None else val_t
        else:
            ref[...] = val

    qa_ref[...] = rope(mm(OFF_QA, D_A))
    ka = rope(mm(OFF_KA, D_A))
    put(ka_ref, ka)
    kabf_ref[...] = ka.astype(BF16)
    kmean_ref[0] = jnp.mean(ka, axis=0, keepdims=True)
    va = mm(OFF_VA, D_A)
    va_t = va.T
    put(va_ref, va, va_t)
    vat_ref[0] = _vt_with_ones(va_t)
    ga_ref[...] = mm(OFF_GA, D_A)
    qb_ref[...] = rope(mm(OFF_QB, D_B))
    kb = rope(mm(OFF_KB, D_B))
    put(kb_ref, kb)
    kbbf_ref[...] = kb.astype(BF16)
    vb = mm(OFF_VB, D_B)
    vb_t = vb.T
    put(vb_ref, vb, vb_t)
    vbt_ref[0] = _vt_with_ones(vb_t)
    gb_ref[...] = mm(OFF_GB, D_B)
    iq_ref[...] = rope(mm(OFF_IQ, IDX_HEADS * IDX_DIM))
    ik2 = rope(mm(OFF_IK, LANES))
    put(ik_ref, ik2[:, :IDX_DIM], ik2.T[:IDX_DIM] if kv_transposed else None)
    ikbf_ref[...] = ik2.astype(BF16)
    iw_ref[...] = mm(OFF_IW, LANES)
    qm_ref[...] = mm(OFF_QM, D_M)
    gm_ref[...] = mm(OFF_GM, D_M)


def _project(x2d, g, w_pad, tabs, n_tab_tiles, tm, kv_transposed):
    n, d = x2d.shape
    nt = n // tm
    tok = lambda w: pl.BlockSpec((tm, w), lambda i: (i, 0))
    tab = pl.BlockSpec((tm, LANES), lambda i: (i % n_tab_tiles, 0))
    vt = lambda w: pl.BlockSpec((1, w // LANES * VT_ROWS, tm), lambda i: (i, 0, 0))
    vt_shape = lambda w: jax.ShapeDtypeStruct((nt, w // LANES * VT_ROWS, tm), BF16)
    widths = [D_A] * 4 + [D_B] * 4 + [IDX_HEADS * IDX_DIM, IDX_DIM, LANES, D_M, D_M]
    out_shape = [jax.ShapeDtypeStruct((n, w), F32) for w in widths]
    out_specs = [tok(w) for w in widths]
    if kv_transposed:
        for idx in (1, 2, 5, 6, 9):
            w = widths[idx]
            out_shape[idx] = jax.ShapeDtypeStruct((nt // n_tab_tiles, w, n_tab_tiles * tm), F32)
            out_specs[idx] = pl.BlockSpec((1, w, tm), lambda i: (i // n_tab_tiles, 0, i % n_tab_tiles))
    out_shape += [jax.ShapeDtypeStruct((n, D_A), BF16), vt_shape(D_A),
                  jax.ShapeDtypeStruct((n, D_B), BF16), vt_shape(D_B),
                  jax.ShapeDtypeStruct((n, LANES), BF16), jax.ShapeDtypeStruct((nt, 1, D_A), F32)]
    out_specs += [tok(D_A), vt(D_A), tok(D_B), vt(D_B), tok(LANES), pl.BlockSpec((1, 1, D_A), lambda i: (i, 0, 0))]
    return pl.pallas_call(
        functools.partial(_proj_kernel, kv_transposed=kv_transposed),
        grid=(nt,),
        in_specs=[tok(d),
                  pl.BlockSpec((1, d), lambda i: (0, 0)),
                  pl.BlockSpec((d, W_PAD), lambda i: (0, 0)),
                  tab, tab, tab],
        out_specs=out_specs,
        out_shape=out_shape,
        compiler_params=_cparams("parallel"),
        name="proj",
    )(x2d, g, w_pad, *tabs)


def _memkv_kernel(m_ref, g_ref, w_ref, mk_ref, mv_ref):
    x = m_ref[0]
    ms = jnp.mean(x * x, axis=-1, keepdims=True)
    h = (x * lax.rsqrt(ms + EPS) * g_ref[...]).astype(BF16)
    u = jnp.dot(h, w_ref[...], preferred_element_type=F32)
    mk_ref[0] = u[:, :D_M]
    mv_ref[0] = u[:, D_M:]


def _mem_kv(mem, g, w_bf):
    b, nm, d = mem.shape
    return pl.pallas_call(
        _memkv_kernel,
        grid=(b,),
        in_specs=[pl.BlockSpec((1, nm, d), lambda i: (i, 0, 0)),
                  pl.BlockSpec((1, d), lambda i: (0, 0)),
                  pl.BlockSpec((d, 2 * D_M), lambda i: (0, 0))],
        out_specs=[pl.BlockSpec((1, nm, D_M), lambda i: (i, 0, 0))] * 2,
        out_shape=[jax.ShapeDtypeStruct((b, nm, D_M), F32)] * 2,
        compiler_params=_cparams("parallel"),
        name="mem_kv",
    )(mem, g, w_bf)


def _flash_step(s, m, acc_ref, c, vt):
    m_new = jnp.maximum(m, jnp.max(s, axis=0, keepdims=True))
    alpha = jnp.exp2(m - m_new)
    p = jnp.exp2(s - m_new).astype(BF16)
    acc_ref[c] = alpha * acc_ref[c] + jnp.dot(vt, p, preferred_element_type=F32)
    return m_new


def _vt_with_ones(v_t):
    vt = v_t.astype(BF16)
    ones = jnp.ones((SUM_ROWS, v_t.shape[1]), BF16)
    parts = []
    for c in range(v_t.shape[0] // LANES):
        parts += [vt[c * LANES:(c + 1) * LANES], ones]
    return jnp.concatenate(parts, axis=0)


def _moba_kernel(q_ref, k_ref, vt_ref, km_ref, o_ref, qaug_scr, acc_scr, s_scr):
    qi = pl.program_id(1)
    n_pairs = D_A // LANES
    blk_id = lax.broadcasted_iota(jnp.int32, (TILE, LANES), 1)
    blk_f = blk_id.astype(F32)
    past = blk_id < qi
    for c in range(n_pairs):
        cs = slice(c * LANES, (c + 1) * LANES)
        qc = q_ref[0, :, cs]
        kmc = km_ref[0, :, cs]
        biases = []
        for hh in range(2):
            qm = jnp.where(_half_mask(hh), qc, 0.0)
            g = lax.dot_general(qm, kmc, NT, precision=lax.Precision.HIGHEST, preferred_element_type=F32)
            g = jnp.where(past, g, -jnp.inf)
            sel = jnp.zeros((TILE, LANES), jnp.bool_)
            for _ in range(MOBA_TOPK):
                m = jnp.max(g, axis=1, keepdims=True)
                first = jnp.min(jnp.where(g == m, blk_f, 1e9), axis=1, keepdims=True)
                pick = (blk_f == first) & past
                sel = sel | pick
                g = jnp.where(pick, -jnp.inf, g)
            biases.append(jnp.where(sel, 0.0, NEG).astype(BF16))
        qaug_scr[c] = jnp.concatenate([_head_pair_queries(qc), jnp.concatenate(biases, axis=0)], axis=1)

    own0 = pl.multiple_of(qi * TILE, TILE)
    ko = k_ref[0, pl.ds(own0, TILE), :]
    vto = vt_ref[0, qi]
    krow = lax.broadcasted_iota(jnp.int32, (TILE, 2 * TILE), 0)
    qcol = lax.broadcasted_iota(jnp.int32, (TILE, 2 * TILE), 1) % TILE
    ms = []
    for c in range(n_pairs):
        cs = slice(c * LANES, (c + 1) * LANES)
        s = lax.dot_general(ko[:, cs], qaug_scr[c, :, :LANES], NT, preferred_element_type=F32)
        s = jnp.where(krow <= qcol, s, NEG)
        m = jnp.max(s, axis=0, keepdims=True)
        p = jnp.exp2(s - m).astype(BF16)
        ms.append(m)
        acc_scr[c] = jnp.dot(vto[c * VT_ROWS:(c + 1) * VT_ROWS, :], p, preferred_element_type=F32)

    def logits(j, c):
        j0 = pl.multiple_of(j * TILE, TILE)
        onehot = jnp.where(blk_id == j, 1.0, 0.0).astype(BF16)
        k_aug = jnp.concatenate([k_ref[0, pl.ds(j0, TILE), c * LANES:(c + 1) * LANES], onehot], axis=1)
        return lax.dot_general(k_aug, qaug_scr[c], NT, preferred_element_type=F32)

    s_scr[0] = logits(0, 0)

    def body(j, ms):
        vtj = vt_ref[0, j]
        new_m = []
        for c in range(n_pairs):
            if c + 1 < n_pairs:
                s_scr[c + 1] = logits(j, c + 1)
            else:
                s_scr[0] = logits(jnp.minimum(j + 1, qi - 1), 0)
            new_m.append(_flash_step(s_scr[c], ms[c], acc_scr, c, vtj[c * VT_ROWS:(c + 1) * VT_ROWS, :]))
        return tuple(new_m)

    lax.fori_loop(0, qi, body, tuple(ms))
    o_ref[0] = jnp.concatenate([_pair_output(acc_scr[c]) for c in range(n_pairs)], axis=1)


def _moba_prompt(qa, ka_bf, va_t, kmean_pad):
    b, t, _ = qa.shape
    nt = t // TILE
    return pl.pallas_call(
        _moba_kernel,
        grid=(b, nt),
        in_specs=[pl.BlockSpec((1, TILE, D_A), lambda i, j: (i, j, 0)),
                  pl.BlockSpec((1, t, D_A), lambda i, j: (i, 0, 0)),
                  pl.BlockSpec((1, nt, va_t.shape[2], TILE), lambda i, j: (i, 0, 0, 0)),
                  pl.BlockSpec((1, LANES, D_A), lambda i, j: (i, 0, 0))],
        out_specs=pl.BlockSpec((1, TILE, D_A), lambda i, j: (i, j, 0)),
        out_shape=jax.ShapeDtypeStruct((b, t, D_A), F32),
        scratch_shapes=[pltpu.VMEM((D_A // LANES, 2 * TILE, 2 * LANES), BF16),
                        pltpu.VMEM((D_A // LANES, VT_ROWS, 2 * TILE), F32),
                        pltpu.VMEM((D_A // LANES, TILE, 2 * TILE), F32)],
        compiler_params=_cparams("parallel", "arbitrary"),
        name="moba_prompt",
    )(qa, ka_bf, va_t, kmean_pad)


def _dsa_kernel(q_ref, iq_ref, iw_ref, ik_ref, k_ref, vt_ref, o_ref,
                key_scr, hi_scr, lo_scr, rel_a, rel_b, acc_scr, s_scr, *, k_sel, idx_bits):
    qi = pl.program_id(1)
    n_tiles = qi + 1
    n_pairs = D_B // LANES
    kf = float(k_sel)
    krow = lax.broadcasted_iota(jnp.int32, (TILE, TILE), 0)
    q_pos = lax.broadcasted_iota(jnp.int32, (TILE, TILE), 1) + qi * TILE
    key_pos = lambda j: krow + j * TILE

    iq = iq_ref[0]
    iw_t = (iw_ref[0] * (IDX_HEADS ** -0.5)).T
    iqm = [(jnp.where(_half_mask(n % 2), iq[:, (n // 2) * LANES:(n // 2 + 1) * LANES], 0.0)
            * (IDX_DIM ** -0.5)).astype(BF16) for n in range(IDX_HEADS)]

    def relevances(j, out_ref):
        j0 = pl.multiple_of(j * TILE, TILE)
        ikj = ik_ref[0, pl.ds(j0, TILE), :]
        for n in range(IDX_HEADS):
            out_ref[n] = lax.dot_general(ikj, iqm[n], NT, preferred_element_type=F32)

    def store_keys(j, rel_ref):
        score = jnp.zeros((TILE, TILE), F32)
        for n in range(IDX_HEADS):
            score = score + iw_t[n:n + 1, :] * jnp.maximum(rel_ref[n], 0.0)
        key = jnp.where(key_pos(j) > q_pos, jnp.int32(INT_MIN), _sortable(score))
        key_scr[j] = key
        hi_scr[j] = lax.shift_right_arithmetic(key, 16).astype(jnp.int16)
        lo_scr[j] = ((key & 0xFFFF) + I16_MIN).astype(jnp.int16)

    last = n_tiles - 1
    relevances(0, rel_a)

    def score_body(i, carry):
        j = 2 * i
        relevances(jnp.minimum(j + 1, last), rel_b)
        store_keys(j, rel_a)
        relevances(jnp.minimum(j + 2, last), rel_a)
        store_keys(jnp.minimum(j + 1, last), rel_b)
        return carry

    n_pairs_kt = (n_tiles + 1) // 2
    lax.fori_loop(0, n_pairs_kt, score_body, 0)

    @pl.when(n_tiles % 2 == 1)
    def _():
        hi_scr[n_tiles] = jnp.full((TILE, TILE), I16_MIN, jnp.int16)
        lo_scr[n_tiles] = jnp.full((TILE, TILE), I16_MIN, jnp.int16)

    def count16(plane, pred):
        def body(i, acc):
            for jj in (2 * i, 2 * i + 1):
                w = jnp.where(pred(plane[jj]), jnp.int16(1), jnp.int16(0))
                for r in range(TILE // 64):
                    acc = acc + w[r * 64:(r + 1) * 64]
            return acc
        acc = lax.fori_loop(0, n_pairs_kt, body, jnp.zeros((64, TILE), jnp.int16))
        return jnp.sum(acc.astype(jnp.int32).astype(F32), axis=0, keepdims=True)

    def search16(plane, target):
        zero = jnp.zeros((1, TILE), jnp.int32)
        total = (n_tiles * TILE).astype(F32)
        c0 = count16(plane, lambda x: x >= zero.astype(jnp.int16))
        ok0 = c0 >= target
        init = (jnp.where(ok0, zero, jnp.int32(I16_MIN)), jnp.where(ok0, c0, total))

        def bit(i, carry):
            thr, cnt = carry
            cand = thr | lax.shift_left(jnp.int32(1), 14 - i)
            c = count16(plane, lambda x: x >= cand.astype(jnp.int16))
            ok = c >= target
            return jnp.where(ok, cand, thr), jnp.where(ok, c, cnt)

        return lax.fori_loop(0, 15, bit, init)

    thr_hi, cnt_hi = search16(hi_scr, kf)
    thr_hi16 = thr_hi.astype(jnp.int16)
    above = count16(hi_scr, lambda x: x > thr_hi16)

    def mask_lo(i, carry):
        for jj in (2 * i, 2 * i + 1):
            lo_scr[jj] = jnp.where(hi_scr[jj] == thr_hi16, lo_scr[jj], jnp.int16(I16_MIN))
        return carry

    lax.fori_loop(0, n_pairs_kt, mask_lo, 0)
    thr_lo, cnt_lo = search16(lo_scr, kf - above)
    thr = thr_hi * 65536 + (thr_lo - I16_MIN)
    cnt_ge = above + jnp.where(thr_lo > I16_MIN, cnt_lo, cnt_hi - above)
    has_ties = jnp.max(cnt_ge) > kf

    @pl.when(jnp.logical_not(has_ties))
    def _():
        floor = jnp.maximum(thr, jnp.int32(INT_MIN + 1))

        def to_bias(j, carry):
            key_scr[j] = lax.bitcast_convert_type(jnp.where(key_scr[j] >= floor, 0.0, NEG).astype(F32), jnp.int32)
            return carry

        lax.fori_loop(0, n_tiles, to_bias, 0)

    @pl.when(has_ties)
    def _():
        def tie_positions(j, carry):
            lo_scr[j] = jnp.where(key_scr[j] == thr, key_pos(j), jnp.int32(NO_POS)).astype(jnp.int16)
            return carry

        lax.fori_loop(0, n_tiles, tie_positions, 0)

        @pl.when(n_tiles % 2 == 1)
        def _():
            lo_scr[n_tiles] = jnp.full((TILE, TILE), NO_POS, jnp.int16)

        n_tied = count16(lo_scr, lambda x: x < jnp.int16(NO_POS))
        need = kf - (cnt_ge - n_tied)

        def cbit(i, cpos):
            cand = cpos | lax.shift_left(jnp.int32(1), idx_bits - 1 - i)
            cnt = count16(lo_scr, lambda x: x < cand.astype(jnp.int16))
            return jnp.where(cnt < need, cand, cpos)

        cstar = lax.fori_loop(0, idx_bits, cbit, jnp.zeros((1, TILE), jnp.int32))

        def to_bias(j, carry):
            k = key_scr[j]
            kp = key_pos(j)
            keep = ((k > thr) | ((k == thr) & (kp <= cstar))) & (kp <= q_pos)
            key_scr[j] = lax.bitcast_convert_type(jnp.where(keep, 0.0, NEG).astype(F32), jnp.int32)
            return carry

        lax.fori_loop(0, n_tiles, to_bias, 0)

    q_pairs = [_head_pair_queries(q_ref[0, :, c * LANES:(c + 1) * LANES]) for c in range(n_pairs)]
    for c in range(n_pairs):
        acc_scr[c] = jnp.zeros((VT_ROWS, 2 * TILE), F32)

    def logits(j, c):
        j0 = pl.multiple_of(j * TILE, TILE)
        return lax.dot_general(k_ref[0, pl.ds(j0, TILE), c * LANES:(c + 1) * LANES], q_pairs[c], NT,
                               preferred_element_type=F32)

    s_scr[0] = logits(0, 0)

    def body(j, ms):
        vtj = vt_ref[0, j]
        bias = lax.bitcast_convert_type(key_scr[j], F32)
        bias2 = jnp.concatenate([bias, bias], axis=1)
        new_m = []
        for c in range(n_pairs):
            if c + 1 < n_pairs:
                s_scr[c + 1] = logits(j, c + 1)
            else:
                s_scr[0] = logits(jnp.minimum(j + 1, n_tiles - 1), 0)
            new_m.append(_flash_step(s_scr[c] + bias2, ms[c], acc_scr, c, vtj[c * VT_ROWS:(c + 1) * VT_ROWS, :]))
        return tuple(new_m)

    lax.fori_loop(0, n_tiles, body, tuple(jnp.full((1, 2 * TILE), NEG, F32) for _ in range(n_pairs)))
    o_ref[0] = jnp.concatenate([_pair_output(acc_scr[c]) for c in range(n_pairs)], axis=1)


def _dsa_prompt(qb, iq, iw, ik_bf, kb_bf, vb_t):
    b, t, _ = qb.shape
    nt = t // TILE
    k_sel = min(IDX_TOPK, t // 4)
    assert t <= NO_POS and 2 * nt < 2 ** 15
    kern = functools.partial(_dsa_kernel, k_sel=k_sel, idx_bits=max(1, (t - 1).bit_length()))
    res = lambda w: pl.BlockSpec((1, t, w), lambda i, j: (i, 0, 0))
    til = lambda w: pl.BlockSpec((1, TILE, w), lambda i, j: (i, j, 0))
    return pl.pallas_call(
        kern,
        grid=(b, nt),
        in_specs=[til(D_B), til(IDX_HEADS * IDX_DIM), til(LANES), res(LANES), res(D_B),
                  pl.BlockSpec((1, nt, vb_t.shape[2], TILE), lambda i, j: (i, 0, 0, 0))],
        out_specs=til(D_B),
        out_shape=jax.ShapeDtypeStruct((b, t, D_B), F32),
        scratch_shapes=[pltpu.VMEM((nt, TILE, TILE), jnp.int32),
                        pltpu.VMEM((nt + nt % 2, TILE, TILE), jnp.int16),
                        pltpu.VMEM((nt + nt % 2, TILE, TILE), jnp.int16),
                        pltpu.VMEM((IDX_HEADS, TILE, TILE), F32),
                        pltpu.VMEM((IDX_HEADS, TILE, TILE), F32),
                        pltpu.VMEM((D_B // LANES, VT_ROWS, 2 * TILE), F32),
                        pltpu.VMEM((D_B // LANES, TILE, 2 * TILE), F32)],
        compiler_params=_cparams("parallel", "arbitrary"),
        name="dsa_prompt",
    )(qb, iq, iw, ik_bf, kb_bf, vb_t)


def _merge_kernel(x_ref, oa_ref, ga_ref, ob_ref, gb_ref, qm_ref, gm_ref, mk_ref, mv_ref, wo_ref, fg_ref, y_ref,
                  *, final):
    lane = lax.broadcasted_iota(jnp.int32, (1, LANES), 1)
    mk = mk_ref[0].astype(BF16)
    mv = mv_ref[0].astype(BF16)
    oms = []
    for c in range(D_M // LANES):
        cs = slice(c * LANES, (c + 1) * LANES)
        qc = qm_ref[0, :, cs]
        heads = []
        for hh in range(2):
            q_bf = (jnp.where(_half_mask(hh), qc, 0.0) * (HEAD_DIM ** -0.5)).astype(BF16)
            s = lax.dot_general(q_bf, mk[:, cs], NT, preferred_element_type=F32)
            m = jnp.max(s, axis=1, keepdims=True)
            p = jnp.exp(s - m)
            l = jnp.sum(p, axis=1, keepdims=True)
            heads.append(jnp.dot(p.astype(BF16), mv[:, cs], preferred_element_type=F32) / l)
        oms.append(jnp.where(lane < HEAD_DIM, heads[0], heads[1]))
    om = jnp.concatenate(oms, axis=1)
    silu = lambda g: g / (1.0 + jnp.exp(-g))
    ymix = jnp.concatenate([oa_ref[0] * silu(ga_ref[0]), ob_ref[0] * silu(gb_ref[0]), om * silu(gm_ref[0])],
                           axis=1).astype(BF16)
    z = x_ref[0] + jnp.dot(ymix, wo_ref[...], preferred_element_type=F32)
    if final:
        ms = jnp.mean(z * z, axis=-1, keepdims=True)
        z = z * lax.rsqrt(ms + EPS) * fg_ref[...]
    y_ref[0] = z


def _merge(x, oa, ga, ob, gb, qm, gm, mk, mv, wo_bf, fg, tm, final):
    b, t, d = x.shape
    til = lambda w: pl.BlockSpec((1, tm, w), lambda i, j: (i, j, 0))
    memspec = pl.BlockSpec((1, N_MEM, D_M), lambda i, j: (i, 0, 0))
    return pl.pallas_call(
        functools.partial(_merge_kernel, final=final),
        grid=(b, t // tm),
        in_specs=[til(d), til(D_A), til(D_A), til(D_B), til(D_B), til(D_M), til(D_M), memspec, memspec,
                  pl.BlockSpec((d, d), lambda i, j: (0, 0)),
                  pl.BlockSpec((1, d), lambda i, j: (0, 0))],
        out_specs=til(d),
        out_shape=jax.ShapeDtypeStruct((b, t, d), F32),
        compiler_params=_cparams("parallel", "parallel"),
        name="merge",
    )(x, oa, ga, ob, gb, qm, gm, mk, mv, wo_bf, fg)


def _expand_heads(q, n_heads):
    t, w = q.shape
    qt = jnp.concatenate([q] * n_heads, axis=0)
    r = lax.broadcasted_iota(jnp.int32, (n_heads * t, w), 0)
    l = lax.broadcasted_iota(jnp.int32, (n_heads * t, w), 1)
    keep = (l >= (r // t) * HEAD_DIM) & (l < (r // t + 1) * HEAD_DIM)
    return jnp.where(keep, qt, 0.0)


def _collapse_heads(o, n_heads, t):
    w = o.shape[1]
    l = lax.broadcasted_iota(jnp.int32, (t, w), 1)
    out = jnp.zeros((t, w), F32)
    for h in range(n_heads):
        keep = (l >= h * HEAD_DIM) & (l < (h + 1) * HEAD_DIM)
        out = out + jnp.where(keep, o[h * t:(h + 1) * t, :], 0.0)
    return out


def _pages_t(refs):
    pages = [r[0, 0].reshape(r.shape[2] * r.shape[3], r.shape[4]) for r in refs]
    return pages[0] if len(pages) == 1 else jnp.concatenate(pages, axis=1)


def _cache_view(cache):
    return jnp.transpose(cache, (0, 1, 3, 4, 2))


def _values_with_ones(vt):
    return jnp.concatenate([vt.astype(BF16), jnp.ones((SUM_ROWS, vt.shape[1]), BF16)], axis=0)


def _softmax_pv(s_scr, vt_scr, s_new, v_new):
    nsteps = s_scr.shape[0]
    d = v_new.shape[1]
    m = jnp.max(jnp.max(s_scr[...], axis=0), axis=1, keepdims=True)
    m = jnp.maximum(m, jnp.max(s_new, axis=1, keepdims=True))
    p_new = jnp.exp(s_new - m)
    l_new = jnp.sum(p_new, axis=1, keepdims=True)
    acc_new = jnp.dot(p_new.astype(BF16), v_new, preferred_element_type=F32)

    def body(i, acc):
        p = jnp.exp(s_scr[i] - m).astype(BF16)
        return acc + lax.dot_general(p, vt_scr[i], NT, preferred_element_type=F32)

    acc = lax.fori_loop(0, nsteps, body, jnp.zeros((s_new.shape[0], d + SUM_ROWS), F32),
                        unroll=nsteps if nsteps <= 8 else 8)
    return (acc[:, :d] + acc_new) / (acc[:, d:d + 1] + l_new)


def _paged_specs(layer, block_tail, per_step):
    zeros = (0,) * len(block_tail)
    return [pl.BlockSpec((1, 1) + block_tail, lambda s, j, pt, i=i: (layer, pt[s, j * per_step + i]) + zeros)
            for i in range(per_step)]


def _moba_dec_kernel(pt_ref, q_ref, *rest, ppb, bps):
    n_pg = ppb * bps
    k_refs, v_refs = rest[:n_pg], rest[n_pg:2 * n_pg]
    kn_ref, vn_ref, o_ref, s_scr, vt_scr, g_scr, qxt_scr = rest[2 * n_pg:]
    j = pl.program_id(1)
    nsteps = pl.num_programs(1)
    t = q_ref.shape[1]
    rows = H_A * t
    blk_w = s_scr.shape[2] // bps
    qx = _expand_heads(q_ref[0], H_A)
    qx_bf = (qx * (HEAD_DIM ** -0.5)).astype(BF16)

    @pl.when(j == 0)
    def _():
        qxt_scr[...] = jnp.concatenate([qx, jnp.zeros((LANES - rows, D_A), F32)], axis=0).T

    kt = _pages_t(k_refs)
    s_scr[j] = jnp.dot(qx_bf, kt.astype(BF16), preferred_element_type=F32)
    vt_scr[j] = _values_with_ones(_pages_t(v_refs))
    for b in range(bps):
        kmean = jnp.mean(kt[:, b * blk_w:(b + 1) * blk_w], axis=1, keepdims=True)
        g_scr[pl.ds(j * bps + b, 1), :] = jnp.sum(qxt_scr[...] * kmean, axis=0, keepdims=True)

    @pl.when(j == nsteps - 1)
    def _():
        npad = kn_ref.shape[1]
        r = lax.broadcasted_iota(jnp.int32, (rows, npad), 0)
        cidx = lax.broadcasted_iota(jnp.int32, (rows, npad), 1)
        s_new = lax.dot_general(qx_bf, kn_ref[0].astype(BF16), NT, preferred_element_type=F32)
        s_new = jnp.where(cidx <= (r % t), s_new, NEG)

        n_blocks = g_scr.shape[0]
        g = g_scr[...]
        bidx = lax.broadcasted_iota(jnp.int32, g.shape, 0).astype(F32)
        sel = jnp.zeros(g.shape, jnp.bool_)
        for _ in range(min(MOBA_TOPK, n_blocks)):
            mx = jnp.max(g, axis=0, keepdims=True)
            first = jnp.min(jnp.where(g == mx, bidx, 1e9), axis=0, keepdims=True)
            pick = bidx == first
            sel = sel | pick
            g = jnp.where(pick, -jnp.inf, g)
        bias_t = jnp.where(sel, 0.0, NEG).T
        for i in range(s_scr.shape[0]):
            cols = [jnp.broadcast_to(bias_t[:rows, i * bps + b:i * bps + b + 1], (rows, blk_w)) for b in range(bps)]
            s_scr[i] = s_scr[i] + (cols[0] if bps == 1 else jnp.concatenate(cols, axis=1))
        o_ref[0] = _collapse_heads(_softmax_pv(s_scr, vt_scr, s_new, vn_ref[0].astype(BF16)), H_A, t)


def _moba_decode(layer, page_table, qa, cache_kt, cache_vt, kn_pad, vn_pad):
    nseq, t, _ = qa.shape
    n_pages = page_table.shape[1]
    page_size = cache_kt.shape[4]
    ppb = MOBA_BLOCK // page_size
    nb = n_pages // ppb
    bps = max(1, KV_PAGES_PER_STEP // ppb)
    while nb % bps:
        bps //= 2
    rows = H_A * t
    seq = lambda w, r: pl.BlockSpec((1, r, w), lambda s, j, pt: (s, 0, 0))
    kspecs = _paged_specs(layer, cache_kt.shape[2:], ppb * bps)
    grid_spec = pltpu.PrefetchScalarGridSpec(
        num_scalar_prefetch=1,
        grid=(nseq, nb // bps),
        in_specs=[seq(D_A, t)] + kspecs + kspecs + [seq(D_A, kn_pad.shape[1])] * 2,
        out_specs=seq(D_A, t),
        scratch_shapes=[pltpu.VMEM((nb // bps, rows, bps * MOBA_BLOCK), F32),
                        pltpu.VMEM((nb // bps, D_A + SUM_ROWS, bps * MOBA_BLOCK), BF16),
                        pltpu.VMEM((nb, LANES), F32),
                        pltpu.VMEM((D_A, LANES), F32)],
    )
    assert rows <= LANES
    return pl.pallas_call(
        functools.partial(_moba_dec_kernel, ppb=ppb, bps=bps),
        grid_spec=grid_spec,
        out_shape=jax.ShapeDtypeStruct((nseq, t, D_A), F32),
        compiler_params=_cparams("parallel", "arbitrary"),
        name="moba_decode",
    )(page_table, qa, *([cache_kt] * (ppb * bps)), *([cache_vt] * (ppb * bps)), kn_pad, vn_pad)


def _idx_dec_kernel(pt_ref, iq_ref, iw_ref, *rest, per_step, k_sel, idx_bits, past, t):
    ik_refs = rest[:per_step]
    ikn_ref, bias_ref, key_scr = rest[per_step:]
    j = pl.program_id(1)
    nsteps = pl.num_programs(1)
    width = key_scr.shape[2]
    iq_bf = iq_ref[0].astype(BF16)
    iw = iw_ref[0][:, :1]

    def scores(rel):
        wrel = iw * jnp.maximum(rel * (IDX_DIM ** -0.5), 0.0)
        sc = wrel[0:t]
        for n in range(1, IDX_HEADS):
            sc = sc + wrel[n * t:(n + 1) * t]
        return sc

    ik_t = jnp.concatenate([r[0, 0] for r in ik_refs], axis=1)
    key_scr[j] = _sortable(scores(jnp.dot(iq_bf, ik_t.astype(BF16), preferred_element_type=F32)))

    @pl.when(j == nsteps - 1)
    def _():
        kf = float(k_sel)
        npad = ikn_ref.shape[1]
        r_new = lax.broadcasted_iota(jnp.int32, (t, npad), 0)
        c_new = lax.broadcasted_iota(jnp.int32, (t, npad), 1)
        rel_new = lax.dot_general(iq_bf, ikn_ref[0].astype(BF16), NT, preferred_element_type=F32)
        key_new = jnp.where(c_new <= r_new, _sortable(scores(rel_new)), jnp.int32(INT_MIN))
        tail = jnp.full((t, width - npad), jnp.int32(INT_MIN))
        key_scr[nsteps] = jnp.concatenate([key_new, tail], axis=1)

        keys = key_scr[...]
        pos = (lax.broadcasted_iota(jnp.int32, keys.shape, 0) * width
               + lax.broadcasted_iota(jnp.int32, keys.shape, 2))

        def count(pred):
            w = jnp.sum(jnp.where(pred, 1.0, 0.0), axis=0)
            acc = w[:, :LANES]
            for s in range(1, width // LANES):
                acc = acc + w[:, s * LANES:(s + 1) * LANES]
            return jnp.sum(acc, axis=1, keepdims=True)[None]

        zero = jnp.zeros((1, t, 1), jnp.int32)
        thr = jnp.where(count(keys >= zero) >= kf, zero, jnp.int32(INT_MIN))

        def bit(i, thr):
            cand = thr | lax.shift_left(jnp.int32(1), 30 - i)
            return jnp.where(count(keys >= cand) >= kf, cand, thr)

        thr = lax.fori_loop(0, 31, bit, thr)
        need = kf - count(keys > thr)

        def cbit(i, cpos):
            cand = cpos | lax.shift_left(jnp.int32(1), idx_bits - 1 - i)
            return jnp.where(count((keys == thr) & (pos < cand)) < need, cand, cpos)

        cstar = lax.fori_loop(0, idx_bits, cbit, zero)
        keep = (keys > thr) | ((keys == thr) & (pos <= cstar))
        bias_ref[0] = jnp.where(keep, 0.0, NEG).astype(F32)


def _idx_decode(layer, page_table, iq_rows, iw_rows, cache_ik, ikn_pad, t):
    nseq = iq_rows.shape[0]
    n_pages = page_table.shape[1]
    page_size = cache_ik.shape[3]
    per_step = IDX_PAGES_PER_STEP
    while n_pages % per_step:
        per_step //= 2
    nsteps = n_pages // per_step
    width = per_step * page_size
    past = n_pages * page_size
    k_sel = min(IDX_TOPK, (past + t) // 4)
    idx_bits = max(1, ((nsteps + 1) * width - 1).bit_length())
    seq = lambda r, w: pl.BlockSpec((1, r, w), lambda s, j, pt: (s, 0, 0))
    grid_spec = pltpu.PrefetchScalarGridSpec(
        num_scalar_prefetch=1,
        grid=(nseq, nsteps),
        in_specs=[seq(IDX_HEADS * t, IDX_DIM), seq(IDX_HEADS * t, LANES)]
        + _paged_specs(layer, cache_ik.shape[2:], per_step) + [seq(ikn_pad.shape[1], IDX_DIM)],
        out_specs=pl.BlockSpec((1, nsteps + 1, t, width), lambda s, j, pt: (s, 0, 0, 0)),
        scratch_shapes=[pltpu.VMEM((nsteps + 1, t, width), jnp.int32)],
    )
    kern = functools.partial(_idx_dec_kernel, per_step=per_step, k_sel=k_sel, idx_bits=idx_bits, past=past, t=t)
    return pl.pallas_call(
        kern,
        grid_spec=grid_spec,
        out_shape=jax.ShapeDtypeStruct((nseq, nsteps + 1, t, width), F32),
        compiler_params=_cparams("parallel", "arbitrary"),
        name="idx_decode",
    )(page_table, iq_rows, iw_rows, *([cache_ik] * per_step), ikn_pad)


def _dsa_dec_kernel(pt_ref, q_ref, b_ref, bn_ref, *rest, pps):
    k_refs, v_refs = rest[:pps], rest[pps:2 * pps]
    kn_ref, vn_ref, o_ref, s_scr, vt_scr = rest[2 * pps:]
    j = pl.program_id(1)
    nsteps = pl.num_programs(1)
    t = q_ref.shape[1]
    qx_bf = (_expand_heads(q_ref[0], H_B) * (HEAD_DIM ** -0.5)).astype(BF16)

    kt = _pages_t(k_refs).astype(BF16)
    s_scr[j] = jnp.dot(qx_bf, kt, preferred_element_type=F32) + jnp.concatenate([b_ref[0, 0]] * H_B, axis=0)
    vt_scr[j] = _values_with_ones(_pages_t(v_refs))

    @pl.when(j == nsteps - 1)
    def _():
        s_new = lax.dot_general(qx_bf, kn_ref[0].astype(BF16), NT, preferred_element_type=F32)
        s_new = s_new + jnp.concatenate([bn_ref[0, 0]] * H_B, axis=0)
        o_ref[0] = _collapse_heads(_softmax_pv(s_scr, vt_scr, s_new, vn_ref[0].astype(BF16)), H_B, t)


def _dsa_decode(layer, page_table, qb, bias, cache_kt, cache_vt, kn_pad, vn_pad):
    nseq, t, _ = qb.shape
    n_pages = page_table.shape[1]
    page_size = cache_kt.shape[4]
    pps = KV_PAGES_PER_STEP
    while n_pages % pps or bias.shape[3] % (pps * page_size):
        pps //= 2
    width = bias.shape[3]
    chunk = pps * page_size
    assert n_pages % pps == 0 and width % chunk == 0
    per_w = width // chunk
    rows = H_B * t
    npad = kn_pad.shape[1]
    seq = lambda w, r: pl.BlockSpec((1, r, w), lambda s, j, pt: (s, 0, 0))
    bspec = pl.BlockSpec((1, 1, t, chunk), lambda s, j, pt: (s, j // per_w, 0, j % per_w))
    bnew = pl.BlockSpec((1, 1, t, npad), lambda s, j, pt: (s, bias.shape[1] - 1, 0, 0))
    kspecs = _paged_specs(layer, cache_kt.shape[2:], pps)
    grid_spec = pltpu.PrefetchScalarGridSpec(
        num_scalar_prefetch=1,
        grid=(nseq, n_pages // pps),
        in_specs=[seq(D_B, t), bspec, bnew] + kspecs + kspecs + [seq(D_B, npad)] * 2,
        out_specs=seq(D_B, t),
        scratch_shapes=[pltpu.VMEM((n_pages // pps, rows, chunk), F32),
                        pltpu.VMEM((n_pages // pps, D_B + SUM_ROWS, chunk), BF16)],
    )
    return pl.pallas_call(
        functools.partial(_dsa_dec_kernel, pps=pps),
        grid_spec=grid_spec,
        out_shape=jax.ShapeDtypeStruct((nseq, t, D_B), F32),
        compiler_params=_cparams("parallel", "arbitrary"),
        name="dsa_decode",
    )(page_table, qb, bias, bias, *([cache_kt] * pps), *([cache_vt] * pps), kn_pad, vn_pad)


def _rope_tables(pos):
    half = ROT_DIM // 2
    inv = jnp.power(ROPE_THETA, -jnp.arange(half, dtype=F32) / half)
    ang = pos.astype(F32)[:, None] * inv[None, :]
    cos, sin = jnp.cos(ang), jnp.sin(ang)
    n = pos.shape[0]
    ones = jnp.ones((n, HEAD_DIM - ROT_DIM), F32)
    zeros = jnp.zeros((n, HEAD_DIM - ROT_DIM), F32)
    zh = jnp.zeros((n, half), F32)
    tc = jnp.concatenate([cos, cos, ones], axis=1)
    ts1 = jnp.concatenate([-sin, zh, zeros], axis=1)
    ts2 = jnp.concatenate([zh, sin, zeros], axis=1)
    dup = lambda a: jnp.concatenate([a, a], axis=1)
    return dup(tc), dup(ts1), dup(ts2)


def _pad_w_in(w):
    d = w.shape[0]
    main = w[:, :8 * _G]
    o = 8 * _G
    iq = w[:, o:o + 256]
    ik = w[:, o + 256:o + 320]
    iw = w[:, o + 320:o + 324]
    qm = w[:, o + 324:o + 324 + D_M]
    gm = w[:, o + 324 + D_M:o + 324 + 2 * D_M]
    iw_pad = jnp.concatenate([iw, jnp.zeros((d, LANES - IDX_HEADS), w.dtype)], axis=1)
    return jnp.concatenate([main, iq, ik, ik, iw_pad, qm, gm], axis=1).astype(BF16)


def kernel(x_prompt, x_sample, mem_prompt, cache_k_a, cache_v_a, cache_k_b, cache_v_b, cache_k_idx,
           cache_mem_k, cache_mem_v, page_table, norm_g, w_in, mem_norm_g, w_mem_kv, w_out, final_norm_g):
    depth = w_in.shape[0]
    b, t, d = x_prompt.shape
    nseq, ts, _ = x_sample.shape
    n_pages = page_table.shape[1]
    page_size = cache_k_a.shape[2]
    past = n_pages * page_size
    assert t % TILE == 0 and t // TILE <= LANES and page_size == LANES
    assert past % MOBA_BLOCK == 0 and ts <= LANES and ts % 8 == 0

    tabs_p = _rope_tables(jnp.arange(t, dtype=jnp.int32))
    pos_s = past + jnp.arange(ts, dtype=jnp.int32)
    tabs_s = tuple(jnp.tile(a, (nseq, 1)) for a in _rope_tables(pos_s))
    fg = final_norm_g.reshape(1, d)
    nbp = t // TILE

    xp, xs = x_prompt, x_sample
    outs = [[] for _ in range(12)]
    for l in range(depth):
        w_pad = _pad_w_in(w_in[l])
        g = norm_g[l].reshape(1, d)
        wo_bf = w_out[l].astype(BF16)
        last = l == depth - 1

        (qa, ka, va, ga, qb, kb, vb, gb, iq, ik, iw, qm, gm,
         ka_bf, va_t, kb_bf, vb_t, ik_bf, kmean) = _project(xp.reshape(b * t, d), g, w_pad, tabs_p, nbp, TILE, True)
        r3 = lambda a: a.reshape(b, t, a.shape[-1])
        r4 = lambda a: a.reshape(b, nbp, a.shape[-2], TILE)
        mk, mv = _mem_kv(mem_prompt, mem_norm_g[l].reshape(1, d), w_mem_kv[l].astype(BF16))
        kmean_pad = jnp.pad(kmean.reshape(b, nbp, D_A), ((0, 0), (0, LANES - nbp), (0, 0)))
        oa = _moba_prompt(r3(qa), r3(ka_bf), r4(va_t), kmean_pad)
        ob = _dsa_prompt(r3(qb), r3(iq).astype(BF16), r3(iw), r3(ik_bf), r3(kb_bf), r4(vb_t))
        xp = _merge(xp, oa, r3(ga), ob, r3(gb), r3(qm), r3(gm), mk, mv, wo_bf, fg, TILE, last)
        for lst, a, h in ((outs[0], ka, H_A), (outs[1], va, H_A), (outs[2], kb, H_B), (outs[3], vb, H_B)):
            lst.append(jnp.transpose(a.reshape(b, h, HEAD_DIM, t), (0, 3, 1, 2)))
        outs[4].append(jnp.transpose(ik, (0, 2, 1)))
        outs[5].append(mk.reshape(b, N_MEM, H_M, HEAD_DIM))
        outs[6].append(mv.reshape(b, N_MEM, H_M, HEAD_DIM))

        n_s = nseq * ts
        tm_s = n_s if n_s <= TILE else TILE
        (qa, ka, va, ga, qb, kb, vb, gb, iq, ik, iw, qm, gm,
         _, _, _, _, _, _) = _project(xs.reshape(n_s, d), g, w_pad, tabs_s, n_s // tm_s, tm_s, False)
        s3 = lambda a: a.reshape(nseq, ts, a.shape[-1])
        padn = lambda a: jnp.pad(s3(a), ((0, 0), (0, LANES - ts), (0, 0)))
        oa = _moba_decode(l, page_table, s3(qa), _cache_view(cache_k_a), _cache_view(cache_v_a), padn(ka), padn(va))
        iq_rows = s3(iq).reshape(nseq, ts, IDX_HEADS, IDX_DIM).transpose(0, 2, 1, 3).reshape(
            nseq, IDX_HEADS * ts, IDX_DIM)
        iw_rows = (s3(iw)[:, :, :IDX_HEADS] * (IDX_HEADS ** -0.5)).transpose(0, 2, 1).reshape(
            nseq, IDX_HEADS * ts, 1)
        iw_rows = jnp.broadcast_to(iw_rows, (nseq, IDX_HEADS * ts, LANES))
        bias = _idx_decode(l, page_table, iq_rows, iw_rows, jnp.transpose(cache_k_idx, (0, 1, 3, 2)), padn(ik), ts)
        ob = _dsa_decode(l, page_table, s3(qb), bias, _cache_view(cache_k_b), _cache_view(cache_v_b),
                         padn(kb), padn(vb))
        mks = cache_mem_k[l].reshape(nseq, N_MEM, D_M)
        mvs = cache_mem_v[l].reshape(nseq, N_MEM, D_M)
        xs = _merge(xs, oa, s3(ga), ob, s3(gb), s3(qm), s3(gm), mks, mvs, wo_bf, fg, ts, last)
        for lst, a, shp in ((outs[7], ka, (nseq, ts, H_A, HEAD_DIM)), (outs[8], va, (nseq, ts, H_A, HEAD_DIM)),
                            (outs[9], kb, (nseq, ts, H_B, HEAD_DIM)), (outs[10], vb, (nseq, ts, H_B, HEAD_DIM)),
                            (outs[11], ik, (nseq, ts, IDX_DIM))):
            lst.append(a.reshape(shp))

    stk = [jnp.stack(o) for o in outs]
    return (xp, xs, stk[0], stk[1], stk[2], stk[3], stk[4], stk[5], stk[6],
            stk[7], stk[8], stk[9], stk[10], stk[11])
```

```python
import functools

import jax
import jax.numpy as jnp
from jax import lax
from jax.experimental import pallas as pl
from jax.experimental.pallas import tpu as pltpu

HEAD_DIM = 64
H_A = 6
H_B = 6
H_M = 4
D_A = H_A * HEAD_DIM
D_B = H_B * HEAD_DIM
D_M = H_M * HEAD_DIM
N_MEM = 256
MOBA_BLOCK = 256
MOBA_TOPK = 3
IDX_HEADS = 4
IDX_DIM = 64
IDX_TOPK = 256
ROPE_THETA = 500000.0
ROT_DIM = HEAD_DIM // 4
EPS = 1e-6

LANES = 128
TILE = 256
NEG = -1e30
LOG2E = 1.4426950408889634
SUM_ROWS = 16
VT_ROWS = LANES + SUM_ROWS
INT_MIN = -(2 ** 31)
I16_MIN = -(2 ** 15)
NO_POS = 2 ** 15 - 1
VMEM_LIMIT = 60 * 1024 * 1024
IDX_PAGES_PER_STEP = 32
KV_PAGES_PER_STEP = 32

_G = D_A
OFF_QA, OFF_KA, OFF_VA, OFF_GA = 0, _G, 2 * _G, 3 * _G
OFF_QB, OFF_KB, OFF_VB, OFF_GB = 4 * _G, 5 * _G, 6 * _G, 7 * _G
OFF_IQ = 8 * _G
OFF_IK = OFF_IQ + 256
OFF_IW = OFF_IK + LANES
OFF_QM = OFF_IW + LANES
OFF_GM = OFF_QM + D_M
W_PAD = OFF_GM + D_M

F32 = jnp.float32
BF16 = jnp.bfloat16
NT = (((1,), (1,)), ((), ()))


def _cparams(*sem):
    return pltpu.CompilerParams(dimension_semantics=sem, vmem_limit_bytes=VMEM_LIMIT)


def _half_mask(hh):
    lane = lax.broadcasted_iota(jnp.int32, (1, LANES), 1)
    return (lane >= HEAD_DIM * hh) & (lane < HEAD_DIM * (hh + 1))


def _sortable(score):
    bits = lax.bitcast_convert_type(score, jnp.int32)
    key = bits ^ (lax.shift_right_arithmetic(bits, 31) & jnp.int32(0x7FFFFFFF))
    return jnp.where(score == 0.0, jnp.int32(0), key)


def _head_pair_queries(qc):
    scaled = qc * (HEAD_DIM ** -0.5 * LOG2E)
    return jnp.concatenate([jnp.where(_half_mask(0), scaled, 0.0), jnp.where(_half_mask(1), scaled, 0.0)],
                           axis=0).astype(BF16)


def _pair_output(acc):
    t = acc.shape[1] // 2
    o = acc[:LANES] / acc[LANES:LANES + 1]
    d = lax.broadcasted_iota(jnp.int32, (LANES, t), 0)
    return jnp.where(d < HEAD_DIM, o[:, :t], o[:, t:]).T


def _proj_kernel(x_ref, g_ref, w_ref, tc_ref, ts1_ref, ts2_ref,
                 qa_ref, ka_ref, va_ref, ga_ref, qb_ref, kb_ref, vb_ref, gb_ref,
                 iq_ref, ik_ref, iw_ref, qm_ref, gm_ref,
                 kabf_ref, vat_ref, kbbf_ref, vbt_ref, ikbf_ref, kmean_ref, *, kv_transposed):
    x = x_ref[...]
    ms = jnp.mean(x * x, axis=-1, keepdims=True)
    h = (x * lax.rsqrt(ms + EPS) * g_ref[...]).astype(BF16)
    tc, ts1, ts2 = tc_ref[...], ts1_ref[...], ts2_ref[...]

    def mm(off, width):
        return jnp.dot(h, w_ref[:, off:off + width], preferred_element_type=F32)

    def rope(u):
        outs = []
        for c in range(u.shape[1] // LANES):
            uc = u[:, c * LANES:(c + 1) * LANES]
            outs.append(uc * tc + pltpu.roll(uc, LANES - ROT_DIM // 2, 1) * ts1
                        + pltpu.roll(uc, ROT_DIM // 2, 1) * ts2)
        return outs[0] if len(outs) == 1 else jnp.concatenate(outs, axis=1)

    def put(ref, val, val_t=None):
        if kv_transposed:
            ref[0] = val.T if val_t is None else val_t
        else:
            ref[...] = val

    qa_ref[...] = rope(mm(OFF_QA, D_A))
    ka = rope(mm(OFF_KA, D_A))
    put(ka_ref, ka)
    kabf_ref[...] = ka.astype(BF16)
    kmean_ref[0] = jnp.mean(ka, axis=0, keepdims=True)
    va = mm(OFF_VA, D_A)
    va_t = va.T
    put(va_ref, va, va_t)
    vat_ref[0] = _vt_with_ones(va_t)
    ga_ref[...] = mm(OFF_GA, D_A)
    qb_ref[...] = rope(mm(OFF_QB, D_B))
    kb = rope(mm(OFF_KB, D_B))
    put(kb_ref, kb)
    kbbf_ref[...] = kb.astype(BF16)
    vb = mm(OFF_VB, D_B)
    vb_t = vb.T
    put(vb_ref, vb, vb_t)
    vbt_ref[0] = _vt_with_ones(vb_t)
    gb_ref[...] = mm(OFF_GB, D_B)
    iq_ref[...] = rope(mm(OFF_IQ, IDX_HEADS * IDX_DIM))
    ik2 = rope(mm(OFF_IK, LANES))
    put(ik_ref, ik2[:, :IDX_DIM], ik2.T[:IDX_DIM] if kv_transposed else None)
    ikbf_ref[...] = ik2.astype(BF16)
    iw_ref[...] = mm(OFF_IW, LANES)
    qm_ref[...] = mm(OFF_QM, D_M)
    gm_ref[...] = mm(OFF_GM, D_M)


def _project(x2d, g, w_pad, tabs, n_tab_tiles, tm, kv_transposed):
    n, d = x2d.shape
    nt = n // tm
    tok = lambda w: pl.BlockSpec((tm, w), lambda i: (i, 0))
    tab = pl.BlockSpec((tm, LANES), lambda i: (i % n_tab_tiles, 0))
    vt = lambda w: pl.BlockSpec((1, w // LANES * VT_ROWS, tm), lambda i: (i, 0, 0))
    vt_shape = lambda w: jax.ShapeDtypeStruct((nt, w // LANES * VT_ROWS, tm), BF16)
    widths = [D_A] * 4 + [D_B] * 4 + [IDX_HEADS * IDX_DIM, IDX_DIM, LANES, D_M, D_M]
    out_shape = [jax.ShapeDtypeStruct((n, w), F32) for w in widths]
    out_specs = [tok(w) for w in widths]
    if kv_transposed:
        for idx in (1, 2, 5, 6, 9):
            w = widths[idx]
            out_shape[idx] = jax.ShapeDtypeStruct((nt // n_tab_tiles, w, n_tab_tiles * tm), F32)
            out_specs[idx] = pl.BlockSpec((1, w, tm), lambda i: (i // n_tab_tiles, 0, i % n_tab_tiles))
    out_shape += [jax.ShapeDtypeStruct((n, D_A), BF16), vt_shape(D_A),
                  jax.ShapeDtypeStruct((n, D_B), BF16), vt_shape(D_B),
                  jax.ShapeDtypeStruct((n, LANES), BF16), jax.ShapeDtypeStruct((nt, 1, D_A), F32)]
    out_specs += [tok(D_A), vt(D_A), tok(D_B), vt(D_B), tok(LANES), pl.BlockSpec((1, 1, D_A), lambda i: (i, 0, 0))]
    return pl.pallas_call(
        functools.partial(_proj_kernel, kv_transposed=kv_transposed),
        grid=(nt,),
        in_specs=[tok(d),
                  pl.BlockSpec((1, d), lambda i: (0, 0)),
                  pl.BlockSpec((d, W_PAD), lambda i: (0, 0)),
                  tab, tab, tab],
        out_specs=out_specs,
        out_shape=out_shape,
        compiler_params=_cparams("parallel"),
        name="proj",
    )(x2d, g, w_pad, *tabs)


def _memkv_kernel(m_ref, g_ref, w_ref, mk_ref, mv_ref):
    x = m_ref[0]
    ms = jnp.mean(x * x, axis=-1, keepdims=True)
    h = (x * lax.rsqrt(ms + EPS) * g_ref[...]).astype(BF16)
    u = jnp.dot(h, w_ref[...], preferred_element_type=F32)
    mk_ref[0] = u[:, :D_M]
    mv_ref[0] = u[:, D_M:]


def _mem_kv(mem, g, w_bf):
    b, nm, d = mem.shape
    return pl.pallas_call(
        _memkv_kernel,
        grid=(b,),
        in_specs=[pl.BlockSpec((1, nm, d), lambda i: (i, 0, 0)),
                  pl.BlockSpec((1, d), lambda i: (0, 0)),
                  pl.BlockSpec((d, 2 * D_M), lambda i: (0, 0))],
        out_specs=[pl.BlockSpec((1, nm, D_M), lambda i: (i, 0, 0))] * 2,
        out_shape=[jax.ShapeDtypeStruct((b, nm, D_M), F32)] * 2,
        compiler_params=_cparams("parallel"),
        name="mem_kv",
    )(mem, g, w_bf)


def _flash_step(s, m, acc_ref, c, vt):
    m_new = jnp.maximum(m, jnp.max(s, axis=0, keepdims=True))
    alpha = jnp.exp2(m - m_new)
    p = jnp.exp2(s - m_new).astype(BF16)
    acc_ref[c] = alpha * acc_ref[c] + jnp.dot(vt, p, preferred_element_type=F32)
    return m_new


def _vt_with_ones(v_t):
    vt = v_t.astype(BF16)
    ones = jnp.ones((SUM_ROWS, v_t.shape[1]), BF16)
    parts = []
    for c in range(v_t.shape[0] // LANES):
        parts += [vt[c * LANES:(c + 1) * LANES], ones]
    return jnp.concatenate(parts, axis=0)


def _moba_kernel(q_ref, k_ref, vt_ref, km_ref, o_ref, qaug_scr, acc_scr, s_scr):
    qi = pl.program_id(1)
    n_pairs = D_A // LANES
    blk_id = lax.broadcasted_iota(jnp.int32, (TILE, LANES), 1)
    blk_f = blk_id.astype(F32)
    past = blk_id < qi
    for c in range(n_pairs):
        cs = slice(c * LANES, (c + 1) * LANES)
        qc = q_ref[0, :, cs]
        kmc = km_ref[0, :, cs]
        biases = []
        for hh in range(2):
            qm = jnp.where(_half_mask(hh), qc, 0.0)
            g = lax.dot_general(qm, kmc, NT, precision=lax.Precision.HIGHEST, preferred_element_type=F32)
            g = jnp.where(past, g, -jnp.inf)
            sel = jnp.zeros((TILE, LANES), jnp.bool_)
            for _ in range(MOBA_TOPK):
                m = jnp.max(g, axis=1, keepdims=True)
                first = jnp.min(jnp.where(g == m, blk_f, 1e9), axis=1, keepdims=True)
                pick = (blk_f == first) & past
                sel = sel | pick
                g = jnp.where(pick, -jnp.inf, g)
            biases.append(jnp.where(sel, 0.0, NEG).astype(BF16))
        qaug_scr[c] = jnp.concatenate([_head_pair_queries(qc), jnp.concatenate(biases, axis=0)], axis=1)

    own0 = pl.multiple_of(qi * TILE, TILE)
    ko = k_ref[0, pl.ds(own0, TILE), :]
    vto = vt_ref[0, qi]
    krow = lax.broadcasted_iota(jnp.int32, (TILE, 2 * TILE), 0)
    qcol = lax.broadcasted_iota(jnp.int32, (TILE, 2 * TILE), 1) % TILE
    ms = []
    for c in range(n_pairs):
        cs = slice(c * LANES, (c + 1) * LANES)
        s = lax.dot_general(ko[:, cs], qaug_scr[c, :, :LANES], NT, preferred_element_type=F32)
        s = jnp.where(krow <= qcol, s, NEG)
        m = jnp.max(s, axis=0, keepdims=True)
        p = jnp.exp2(s - m).astype(BF16)
        ms.append(m)
        acc_scr[c] = jnp.dot(vto[c * VT_ROWS:(c + 1) * VT_ROWS, :], p, preferred_element_type=F32)

    def logits(j, c):
        j0 = pl.multiple_of(j * TILE, TILE)
        onehot = jnp.where(blk_id == j, 1.0, 0.0).astype(BF16)
        k_aug = jnp.concatenate([k_ref[0, pl.ds(j0, TILE), c * LANES:(c + 1) * LANES], onehot], axis=1)
        return lax.dot_general(k_aug, qaug_scr[c], NT, preferred_element_type=F32)

    s_scr[0] = logits(0, 0)

    def body(j, ms):
        vtj = vt_ref[0, j]
        new_m = []
        for c in range(n_pairs):
            if c + 1 < n_pairs:
                s_scr[c + 1] = logits(j, c + 1)
            else:
                s_scr[0] = logits(jnp.minimum(j + 1, qi - 1), 0)
            new_m.append(_flash_step(s_scr[c], ms[c], acc_scr, c, vtj[c * VT_ROWS:(c + 1) * VT_ROWS, :]))
        return tuple(new_m)

    lax.fori_loop(0, qi, body, tuple(ms))
    o_ref[0] = jnp.concatenate([_pair_output(acc_scr[c]) for c in range(n_pairs)], axis=1)


def _moba_prompt(qa, ka_bf, va_t, kmean_pad):
    b, t, _ = qa.shape
    nt = t // TILE
    return pl.pallas_call(
        _moba_kernel,
        grid=(b, nt),
        in_specs=[pl.BlockSpec((1, TILE, D_A), lambda i, j: (i, j, 0)),
                  pl.BlockSpec((1, t, D_A), lambda i, j: (i, 0, 0)),
                  pl.BlockSpec((1, nt, va_t.shape[2], TILE), lambda i, j: (i, 0, 0, 0)),
                  pl.BlockSpec((1, LANES, D_A), lambda i, j: (i, 0, 0))],
        out_specs=pl.BlockSpec((1, TILE, D_A), lambda i, j: (i, j, 0)),
        out_shape=jax.ShapeDtypeStruct((b, t, D_A), F32),
        scratch_shapes=[pltpu.VMEM((D_A // LANES, 2 * TILE, 2 * LANES), BF16),
                        pltpu.VMEM((D_A // LANES, VT_ROWS, 2 * TILE), F32),
                        pltpu.VMEM((D_A // LANES, TILE, 2 * TILE), F32)],
        compiler_params=_cparams("parallel", "arbitrary"),
        name="moba_prompt",
    )(qa, ka_bf, va_t, kmean_pad)


def _dsa_kernel(q_ref, iq_ref, iw_ref, ik_ref, k_ref, vt_ref, o_ref,
                key_scr, hi_scr, lo_scr, rel_a, rel_b, acc_scr, s_scr, *, k_sel, idx_bits):
    qi = pl.program_id(1)
    n_tiles = qi + 1
    n_pairs = D_B // LANES
    kf = float(k_sel)
    krow = lax.broadcasted_iota(jnp.int32, (TILE, TILE), 0)
    q_pos = lax.broadcasted_iota(jnp.int32, (TILE, TILE), 1) + qi * TILE
    key_pos = lambda j: krow + j * TILE

    iq = iq_ref[0]
    iw_t = (iw_ref[0] * (IDX_HEADS ** -0.5)).T
    iqm = [(jnp.where(_half_mask(n % 2), iq[:, (n // 2) * LANES:(n // 2 + 1) * LANES], 0.0)
            * (IDX_DIM ** -0.5)).astype(BF16) for n in range(IDX_HEADS)]

    def relevances(j, out_ref):
        j0 = pl.multiple_of(j * TILE, TILE)
        ikj = ik_ref[0, pl.ds(j0, TILE), :]
        for n in range(IDX_HEADS):
            out_ref[n] = lax.dot_general(ikj, iqm[n], NT, preferred_element_type=F32)

    def store_keys(j, rel_ref):
        score = jnp.zeros((TILE, TILE), F32)
        for n in range(IDX_HEADS):
            score = score + iw_t[n:n + 1, :] * jnp.maximum(rel_ref[n], 0.0)
        key = jnp.where(key_pos(j) > q_pos, jnp.int32(INT_MIN), _sortable(score))
        key_scr[j] = key
        hi_scr[j] = lax.shift_right_arithmetic(key, 16).astype(jnp.int16)
        lo_scr[j] = ((key & 0xFFFF) + I16_MIN).astype(jnp.int16)

    last = n_tiles - 1
    relevances(0, rel_a)

    def score_body(i, carry):
        j = 2 * i
        relevances(jnp.minimum(j + 1, last), rel_b)
        store_keys(j, rel_a)
        relevances(jnp.minimum(j + 2, last), rel_a)
        store_keys(jnp.minimum(j + 1, last), rel_b)
        return carry

    n_pairs_kt = (n_tiles + 1) // 2
    lax.fori_loop(0, n_pairs_kt, score_body, 0)

    @pl.when(n_tiles % 2 == 1)
    def _():
        hi_scr[n_tiles] = jnp.full((TILE, TILE), I16_MIN, jnp.int16)
        lo_scr[n_tiles] = jnp.full((TILE, TILE), I16_MIN, jnp.int16)

    def count16(plane, pred):
        def body(i, acc):
            for jj in (2 * i, 2 * i + 1):
                w = jnp.where(pred(plane[jj]), jnp.int16(1), jnp.int16(0))
                for r in range(TILE // 64):
                    acc = acc + w[r * 64:(r + 1) * 64]
            return acc
        acc = lax.fori_loop(0, n_pairs_kt, body, jnp.zeros((64, TILE), jnp.int16))
        return jnp.sum(acc.astype(jnp.int32).astype(F32), axis=0, keepdims=True)

    def search16(plane, target):
        zero = jnp.zeros((1, TILE), jnp.int32)
        total = (n_tiles * TILE).astype(F32)
        c0 = count16(plane, lambda x: x >= zero.astype(jnp.int16))
        ok0 = c0 >= target
        init = (jnp.where(ok0, zero, jnp.int32(I16_MIN)), jnp.where(ok0, c0, total))

        def bit(i, carry):
            thr, cnt = carry
            cand = thr | lax.shift_left(jnp.int32(1), 14 - i)
            c = count16(plane, lambda x: x >= cand.astype(jnp.int16))
            ok = c >= target
            return jnp.where(ok, cand, thr), jnp.where(ok, c, cnt)

        return lax.fori_loop(0, 15, bit, init)

    thr_hi, cnt_hi = search16(hi_scr, kf)
    thr_hi16 = thr_hi.astype(jnp.int16)
    above = count16(hi_scr, lambda x: x > thr_hi16)

    def mask_lo(i, carry):
        for jj in (2 * i, 2 * i + 1):
            lo_scr[jj] = jnp.where(hi_scr[jj] == thr_hi16, lo_scr[jj], jnp.int16(I16_MIN))
        return carry

    lax.fori_loop(0, n_pairs_kt, mask_lo, 0)
    thr_lo, cnt_lo = search16(lo_scr, kf - above)
    thr = thr_hi * 65536 + (thr_lo - I16_MIN)
    cnt_ge = above + jnp.where(thr_lo > I16_MIN, cnt_lo, cnt_hi - above)
    has_ties = jnp.max(cnt_ge) > kf

    @pl.when(jnp.logical_not(has_ties))
    def _():
        floor = jnp.maximum(thr, jnp.int32(INT_MIN + 1))

        def to_bias(j, carry):
            key_scr[j] = lax.bitcast_convert_type(jnp.where(key_scr[j] >= floor, 0.0, NEG).astype(F32), jnp.int32)
            return carry

        lax.fori_loop(0, n_tiles, to_bias, 0)

    @pl.when(has_ties)
    def _():
        def tie_positions(j, carry):
            lo_scr[j] = jnp.where(key_scr[j] == thr, key_pos(j), jnp.int32(NO_POS)).astype(jnp.int16)
            return carry

        lax.fori_loop(0, n_tiles, tie_positions, 0)

        @pl.when(n_tiles % 2 == 1)
        def _():
            lo_scr[n_tiles] = jnp.full((TILE, TILE), NO_POS, jnp.int16)

        n_tied = count16(lo_scr, lambda x: x < jnp.int16(NO_POS))
        need = kf - (cnt_ge - n_tied)

        def cbit(i, cpos):
            cand = cpos | lax.shift_left(jnp.int32(1), idx_bits - 1 - i)
            cnt = count16(lo_scr, lambda x: x < cand.astype(jnp.int16))
            return jnp.where(cnt < need, cand, cpos)

        cstar = lax.fori_loop(0, idx_bits, cbit, jnp.zeros((1, TILE), jnp.int32))

        floor = jnp.maximum(thr, jnp.int32(INT_MIN + 1))

        def to_bias(j, carry):
            need_key = jnp.where(key_pos(j) <= cstar, floor, floor + 1)
            key_scr[j] = lax.bitcast_convert_type(jnp.where(key_scr[j] >= need_key, 0.0, NEG).astype(F32), jnp.int32)
            return carry

        lax.fori_loop(0, n_tiles, to_bias, 0)

    q_pairs = [_head_pair_queries(q_ref[0, :, c * LANES:(c + 1) * LANES]) for c in range(n_pairs)]
    for c in range(n_pairs):
        acc_scr[c] = jnp.zeros((VT_ROWS, 2 * TILE), F32)

    def logits(j, c):
        j0 = pl.multiple_of(j * TILE, TILE)
        return lax.dot_general(k_ref[0, pl.ds(j0, TILE), c * LANES:(c + 1) * LANES], q_pairs[c], NT,
                               preferred_element_type=F32)

    s_scr[0] = logits(0, 0)

    def body(j, ms):
        vtj = vt_ref[0, j]
        bias = lax.bitcast_convert_type(key_scr[j], F32)
        bias2 = jnp.concatenate([bias, bias], axis=1)
        new_m = []
        for c in range(n_pairs):
            if c + 1 < n_pairs:
                s_scr[c + 1] = logits(j, c + 1)
            else:
                s_scr[0] = logits(jnp.minimum(j + 1, n_tiles - 1), 0)
            new_m.append(_flash_step(s_scr[c] + bias2, ms[c], acc_scr, c, vtj[c * VT_ROWS:(c + 1) * VT_ROWS, :]))
        return tuple(new_m)

    lax.fori_loop(0, n_tiles, body, tuple(jnp.full((1, 2 * TILE), NEG, F32) for _ in range(n_pairs)))
    o_ref[0] = jnp.concatenate([_pair_output(acc_scr[c]) for c in range(n_pairs)], axis=1)


def _dsa_prompt(qb, iq, iw, ik_bf, kb_bf, vb_t):
    b, t, _ = qb.shape
    nt = t // TILE
    k_sel = min(IDX_TOPK, t // 4)
    assert t <= NO_POS and 2 * nt < 2 ** 15
    kern = functools.partial(_dsa_kernel, k_sel=k_sel, idx_bits=max(1, (t - 1).bit_length()))
    res = lambda w: pl.BlockSpec((1, t, w), lambda i, j: (i, 0, 0))
    til = lambda w: pl.BlockSpec((1, TILE, w), lambda i, j: (i, j, 0))
    return pl.pallas_call(
        kern,
        grid=(b, nt),
        in_specs=[til(D_B), til(IDX_HEADS * IDX_DIM), til(LANES), res(LANES), res(D_B),
                  pl.BlockSpec((1, nt, vb_t.shape[2], TILE), lambda i, j: (i, 0, 0, 0))],
        out_specs=til(D_B),
        out_shape=jax.ShapeDtypeStruct((b, t, D_B), F32),
        scratch_shapes=[pltpu.VMEM((nt, TILE, TILE), jnp.int32),
                        pltpu.VMEM((nt + nt % 2, TILE, TILE), jnp.int16),
                        pltpu.VMEM((nt + nt % 2, TILE, TILE), jnp.int16),
                        pltpu.VMEM((IDX_HEADS, TILE, TILE), F32),
                        pltpu.VMEM((IDX_HEADS, TILE, TILE), F32),
                        pltpu.VMEM((D_B // LANES, VT_ROWS, 2 * TILE), F32),
                        pltpu.VMEM((D_B // LANES, TILE, 2 * TILE), F32)],
        compiler_params=_cparams("parallel", "arbitrary"),
        name="dsa_prompt",
    )(qb, iq, iw, ik_bf, kb_bf, vb_t)


def _merge_kernel(x_ref, oa_ref, ga_ref, ob_ref, gb_ref, qm_ref, gm_ref, mk_ref, mv_ref, wo_ref, fg_ref, y_ref,
                  *, final):
    lane = lax.broadcasted_iota(jnp.int32, (1, LANES), 1)
    mk = mk_ref[0].astype(BF16)
    mv = mv_ref[0].astype(BF16)
    oms = []
    for c in range(D_M // LANES):
        cs = slice(c * LANES, (c + 1) * LANES)
        qc = qm_ref[0, :, cs]
        heads = []
        for hh in range(2):
            q_bf = (jnp.where(_half_mask(hh), qc, 0.0) * (HEAD_DIM ** -0.5)).astype(BF16)
            s = lax.dot_general(q_bf, mk[:, cs], NT, preferred_element_type=F32)
            m = jnp.max(s, axis=1, keepdims=True)
            p = jnp.exp(s - m)
            l = jnp.sum(p, axis=1, keepdims=True)
            heads.append(jnp.dot(p.astype(BF16), mv[:, cs], preferred_element_type=F32) / l)
        oms.append(jnp.where(lane < HEAD_DIM, heads[0], heads[1]))
    om = jnp.concatenate(oms, axis=1)
    silu = lambda g: g / (1.0 + jnp.exp(-g))
    ymix = jnp.concatenate([oa_ref[0] * silu(ga_ref[0]), ob_ref[0] * silu(gb_ref[0]), om * silu(gm_ref[0])],
                           axis=1).astype(BF16)
    z = x_ref[0] + jnp.dot(ymix, wo_ref[...], preferred_element_type=F32)
    if final:
        ms = jnp.mean(z * z, axis=-1, keepdims=True)
        z = z * lax.rsqrt(ms + EPS) * fg_ref[...]
    y_ref[0] = z


def _merge(x, oa, ga, ob, gb, qm, gm, mk, mv, wo_bf, fg, tm, final):
    b, t, d = x.shape
    til = lambda w: pl.BlockSpec((1, tm, w), lambda i, j: (i, j, 0))
    memspec = pl.BlockSpec((1, N_MEM, D_M), lambda i, j: (i, 0, 0))
    return pl.pallas_call(
        functools.partial(_merge_kernel, final=final),
        grid=(b, t // tm),
        in_specs=[til(d), til(D_A), til(D_A), til(D_B), til(D_B), til(D_M), til(D_M), memspec, memspec,
                  pl.BlockSpec((d, d), lambda i, j: (0, 0)),
                  pl.BlockSpec((1, d), lambda i, j: (0, 0))],
        out_specs=til(d),
        out_shape=jax.ShapeDtypeStruct((b, t, d), F32),
        compiler_params=_cparams("parallel", "parallel"),
        name="merge",
    )(x, oa, ga, ob, gb, qm, gm, mk, mv, wo_bf, fg)


def _expand_heads(q, n_heads):
    t, w = q.shape
    qt = jnp.concatenate([q] * n_heads, axis=0)
    r = lax.broadcasted_iota(jnp.int32, (n_heads * t, w), 0)
    l = lax.broadcasted_iota(jnp.int32, (n_heads * t, w), 1)
    keep = (l >= (r // t) * HEAD_DIM) & (l < (r // t + 1) * HEAD_DIM)
    return jnp.where(keep, qt, 0.0)


def _collapse_heads(o, n_heads, t):
    w = o.shape[1]
    l = lax.broadcasted_iota(jnp.int32, (t, w), 1)
    out = jnp.zeros((t, w), F32)
    for h in range(n_heads):
        keep = (l >= h * HEAD_DIM) & (l < (h + 1) * HEAD_DIM)
        out = out + jnp.where(keep, o[h * t:(h + 1) * t, :], 0.0)
    return out


def _pages_t(refs):
    pages = [r[0, 0].reshape(r.shape[2] * r.shape[3], r.shape[4]) for r in refs]
    return pages[0] if len(pages) == 1 else jnp.concatenate(pages, axis=1)


def _cache_view(cache):
    return jnp.transpose(cache, (0, 1, 3, 4, 2))


def _values_with_ones(vt):
    return jnp.concatenate([vt.astype(BF16), jnp.ones((SUM_ROWS, vt.shape[1]), BF16)], axis=0)


def _softmax_pv(s_scr, vt_scr, s_new, v_new):
    nsteps = s_scr.shape[0]
    d = v_new.shape[1]
    m = jnp.max(jnp.max(s_scr[...], axis=0), axis=1, keepdims=True)
    m = jnp.maximum(m, jnp.max(s_new, axis=1, keepdims=True))
    p_new = jnp.exp(s_new - m)
    l_new = jnp.sum(p_new, axis=1, keepdims=True)
    acc_new = jnp.dot(p_new.astype(BF16), v_new, preferred_element_type=F32)

    def body(i, acc):
        p = jnp.exp(s_scr[i] - m).astype(BF16)
        return acc + lax.dot_general(p, vt_scr[i], NT, preferred_element_type=F32)

    acc = lax.fori_loop(0, nsteps, body, jnp.zeros((s_new.shape[0], d + SUM_ROWS), F32),
                        unroll=nsteps if nsteps <= 8 else 8)
    return (acc[:, :d] + acc_new) / (acc[:, d:d + 1] + l_new)


def _paged_specs(layer, block_tail, per_step):
    zeros = (0,) * len(block_tail)
    return [pl.BlockSpec((1, 1) + block_tail, lambda s, j, pt, i=i: (layer, pt[s, j * per_step + i]) + zeros)
            for i in range(per_step)]


def _moba_dec_kernel(pt_ref, q_ref, *rest, ppb, bps):
    n_pg = ppb * bps
    k_refs, v_refs = rest[:n_pg], rest[n_pg:2 * n_pg]
    kn_ref, vn_ref, o_ref, s_scr, vt_scr, g_scr, qxt_scr = rest[2 * n_pg:]
    j = pl.program_id(1)
    nsteps = pl.num_programs(1)
    t = q_ref.shape[1]
    rows = H_A * t
    blk_w = s_scr.shape[2] // bps
    qx = _expand_heads(q_ref[0], H_A)
    qx_bf = (qx * (HEAD_DIM ** -0.5)).astype(BF16)

    @pl.when(j == 0)
    def _():
        qxt_scr[...] = jnp.concatenate([qx, jnp.zeros((LANES - rows, D_A), F32)], axis=0).T

    kt = _pages_t(k_refs)
    s_scr[j] = jnp.dot(qx_bf, kt.astype(BF16), preferred_element_type=F32)
    vt_scr[j] = _values_with_ones(_pages_t(v_refs))
    for b in range(bps):
        kmean = jnp.mean(kt[:, b * blk_w:(b + 1) * blk_w], axis=1, keepdims=True)
        g_scr[pl.ds(j * bps + b, 1), :] = jnp.sum(qxt_scr[...] * kmean, axis=0, keepdims=True)

    @pl.when(j == nsteps - 1)
    def _():
        npad = kn_ref.shape[1]
        r = lax.broadcasted_iota(jnp.int32, (rows, npad), 0)
        cidx = lax.broadcasted_iota(jnp.int32, (rows, npad), 1)
        s_new = lax.dot_general(qx_bf, kn_ref[0].astype(BF16), NT, preferred_element_type=F32)
        s_new = jnp.where(cidx <= (r % t), s_new, NEG)

        n_blocks = g_scr.shape[0]
        g = g_scr[...]
        bidx = lax.broadcasted_iota(jnp.int32, g.shape, 0).astype(F32)
        sel = jnp.zeros(g.shape, jnp.bool_)
        for _ in range(min(MOBA_TOPK, n_blocks)):
            mx = jnp.max(g, axis=0, keepdims=True)
            first = jnp.min(jnp.where(g == mx, bidx, 1e9), axis=0, keepdims=True)
            pick = bidx == first
            sel = sel | pick
            g = jnp.where(pick, -jnp.inf, g)
        bias_t = jnp.where(sel, 0.0, NEG).T
        for i in range(s_scr.shape[0]):
            cols = [jnp.broadcast_to(bias_t[:rows, i * bps + b:i * bps + b + 1], (rows, blk_w)) for b in range(bps)]
            s_scr[i] = s_scr[i] + (cols[0] if bps == 1 else jnp.concatenate(cols, axis=1))
        o_ref[0] = _collapse_heads(_softmax_pv(s_scr, vt_scr, s_new, vn_ref[0].astype(BF16)), H_A, t)


def _moba_decode(layer, page_table, qa, cache_kt, cache_vt, kn_pad, vn_pad):
    nseq, t, _ = qa.shape
    n_pages = page_table.shape[1]
    page_size = cache_kt.shape[4]
    ppb = MOBA_BLOCK // page_size
    nb = n_pages // ppb
    bps = max(1, KV_PAGES_PER_STEP // ppb)
    while nb % bps:
        bps //= 2
    rows = H_A * t
    seq = lambda w, r: pl.BlockSpec((1, r, w), lambda s, j, pt: (s, 0, 0))
    kspecs = _paged_specs(layer, cache_kt.shape[2:], ppb * bps)
    grid_spec = pltpu.PrefetchScalarGridSpec(
        num_scalar_prefetch=1,
        grid=(nseq, nb // bps),
        in_specs=[seq(D_A, t)] + kspecs + kspecs + [seq(D_A, kn_pad.shape[1])] * 2,
        out_specs=seq(D_A, t),
        scratch_shapes=[pltpu.VMEM((nb // bps, rows, bps * MOBA_BLOCK), F32),
                        pltpu.VMEM((nb // bps, D_A + SUM_ROWS, bps * MOBA_BLOCK), BF16),
                        pltpu.VMEM((nb, LANES), F32),
                        pltpu.VMEM((D_A, LANES), F32)],
    )
    assert rows <= LANES
    return pl.pallas_call(
        functools.partial(_moba_dec_kernel, ppb=ppb, bps=bps),
        grid_spec=grid_spec,
        out_shape=jax.ShapeDtypeStruct((nseq, t, D_A), F32),
        compiler_params=_cparams("parallel", "arbitrary"),
        name="moba_decode",
    )(page_table, qa, *([cache_kt] * (ppb * bps)), *([cache_vt] * (ppb * bps)), kn_pad, vn_pad)


def _idx_dec_kernel(pt_ref, iq_ref, iw_ref, *rest, per_step, k_sel, idx_bits, past, t):
    ik_refs = rest[:per_step]
    ikn_ref, bias_ref, key_scr = rest[per_step:]
    j = pl.program_id(1)
    nsteps = pl.num_programs(1)
    width = key_scr.shape[2]
    iq_bf = iq_ref[0].astype(BF16)
    iw = iw_ref[0][:, :1]

    def scores(rel):
        wrel = iw * jnp.maximum(rel * (IDX_DIM ** -0.5), 0.0)
        sc = wrel[0:t]
        for n in range(1, IDX_HEADS):
            sc = sc + wrel[n * t:(n + 1) * t]
        return sc

    ik_t = jnp.concatenate([r[0, 0] for r in ik_refs], axis=1)
    key_scr[j] = _sortable(scores(jnp.dot(iq_bf, ik_t.astype(BF16), preferred_element_type=F32)))

    @pl.when(j == nsteps - 1)
    def _():
        kf = float(k_sel)
        npad = ikn_ref.shape[1]
        r_new = lax.broadcasted_iota(jnp.int32, (t, npad), 0)
        c_new = lax.broadcasted_iota(jnp.int32, (t, npad), 1)
        rel_new = lax.dot_general(iq_bf, ikn_ref[0].astype(BF16), NT, preferred_element_type=F32)
        key_new = jnp.where(c_new <= r_new, _sortable(scores(rel_new)), jnp.int32(INT_MIN))
        tail = jnp.full((t, width - npad), jnp.int32(INT_MIN))
        key_scr[nsteps] = jnp.concatenate([key_new, tail], axis=1)

        keys = key_scr[...]
        pos = (lax.broadcasted_iota(jnp.int32, keys.shape, 0) * width
               + lax.broadcasted_iota(jnp.int32, keys.shape, 2))

        def count(pred):
            w = jnp.sum(jnp.where(pred, 1.0, 0.0), axis=0)
            acc = w[:, :LANES]
            for s in range(1, width // LANES):
                acc = acc + w[:, s * LANES:(s + 1) * LANES]
            return jnp.sum(acc, axis=1, keepdims=True)[None]

        zero = jnp.zeros((1, t, 1), jnp.int32)
        thr = jnp.where(count(keys >= zero) >= kf, zero, jnp.int32(INT_MIN))

        def bit(i, thr):
            cand = thr | lax.shift_left(jnp.int32(1), 30 - i)
            return jnp.where(count(keys >= cand) >= kf, cand, thr)

        thr = lax.fori_loop(0, 31, bit, thr)
        need = kf - count(keys > thr)

        def cbit(i, cpos):
            cand = cpos | lax.shift_left(jnp.int32(1), idx_bits - 1 - i)
            return jnp.where(count((keys == thr) & (pos < cand)) < need, cand, cpos)

        cstar = lax.fori_loop(0, idx_bits, cbit, zero)
        keep = (keys > thr) | ((keys == thr) & (pos <= cstar))
        bias_ref[0] = jnp.where(keep, 0.0, NEG).astype(F32)


def _idx_decode(layer, page_table, iq_rows, iw_rows, cache_ik, ikn_pad, t):
    nseq = iq_rows.shape[0]
    n_pages = page_table.shape[1]
    page_size = cache_ik.shape[3]
    per_step = IDX_PAGES_PER_STEP
    while n_pages % per_step:
        per_step //= 2
    nsteps = n_pages // per_step
    width = per_step * page_size
    past = n_pages * page_size
    k_sel = min(IDX_TOPK, (past + t) // 4)
    idx_bits = max(1, ((nsteps + 1) * width - 1).bit_length())
    seq = lambda r, w: pl.BlockSpec((1, r, w), lambda s, j, pt: (s, 0, 0))
    grid_spec = pltpu.PrefetchScalarGridSpec(
        num_scalar_prefetch=1,
        grid=(nseq, nsteps),
        in_specs=[seq(IDX_HEADS * t, IDX_DIM), seq(IDX_HEADS * t, LANES)]
        + _paged_specs(layer, cache_ik.shape[2:], per_step) + [seq(ikn_pad.shape[1], IDX_DIM)],
        out_specs=pl.BlockSpec((1, nsteps + 1, t, width), lambda s, j, pt: (s, 0, 0, 0)),
        scratch_shapes=[pltpu.VMEM((nsteps + 1, t, width), jnp.int32)],
    )
    kern = functools.partial(_idx_dec_kernel, per_step=per_step, k_sel=k_sel, idx_bits=idx_bits, past=past, t=t)
    return pl.pallas_call(
        kern,
        grid_spec=grid_spec,
        out_shape=jax.ShapeDtypeStruct((nseq, nsteps + 1, t, width), F32),
        compiler_params=_cparams("parallel", "arbitrary"),
        name="idx_decode",
    )(page_table, iq_rows, iw_rows, *([cache_ik] * per_step), ikn_pad)


def _dsa_dec_kernel(pt_ref, q_ref, b_ref, bn_ref, *rest, pps):
    k_refs, v_refs = rest[:pps], rest[pps:2 * pps]
    kn_ref, vn_ref, o_ref, s_scr, vt_scr = rest[2 * pps:]
    j = pl.program_id(1)
    nsteps = pl.num_programs(1)
    t = q_ref.shape[1]
    qx_bf = (_expand_heads(q_ref[0], H_B) * (HEAD_DIM ** -0.5)).astype(BF16)

    kt = _pages_t(k_refs).astype(BF16)
    s_scr[j] = jnp.dot(qx_bf, kt, preferred_element_type=F32) + jnp.concatenate([b_ref[0, 0]] * H_B, axis=0)
    vt_scr[j] = _values_with_ones(_pages_t(v_refs))

    @pl.when(j == nsteps - 1)
    def _():
        s_new = lax.dot_general(qx_bf, kn_ref[0].astype(BF16), NT, preferred_element_type=F32)
        s_new = s_new + jnp.concatenate([bn_ref[0, 0]] * H_B, axis=0)
        o_ref[0] = _collapse_heads(_softmax_pv(s_scr, vt_scr, s_new, vn_ref[0].astype(BF16)), H_B, t)


def _dsa_decode(layer, page_table, qb, bias, cache_kt, cache_vt, kn_pad, vn_pad):
    nseq, t, _ = qb.shape
    n_pages = page_table.shape[1]
    page_size = cache_kt.shape[4]
    pps = KV_PAGES_PER_STEP
    while n_pages % pps or bias.shape[3] % (pps * page_size):
        pps //= 2
    width = bias.shape[3]
    chunk = pps * page_size
    assert n_pages % pps == 0 and width % chunk == 0
    per_w = width // chunk
    rows = H_B * t
    npad = kn_pad.shape[1]
    seq = lambda w, r: pl.BlockSpec((1, r, w), lambda s, j, pt: (s, 0, 0))
    bspec = pl.BlockSpec((1, 1, t, chunk), lambda s, j, pt: (s, j // per_w, 0, j % per_w))
    bnew = pl.BlockSpec((1, 1, t, npad), lambda s, j, pt: (s, bias.shape[1] - 1, 0, 0))
    kspecs = _paged_specs(layer, cache_kt.shape[2:], pps)
    grid_spec = pltpu.PrefetchScalarGridSpec(
        num_scalar_prefetch=1,
        grid=(nseq, n_pages // pps),
        in_specs=[seq(D_B, t), bspec, bnew] + kspecs + kspecs + [seq(D_B, npad)] * 2,
        out_specs=seq(D_B, t),
        scratch_shapes=[pltpu.VMEM((n_pages // pps, rows, chunk), F32),
                        pltpu.VMEM((n_pages // pps, D_B + SUM_ROWS, chunk), BF16)],
    )
    return pl.pallas_call(
        functools.partial(_dsa_dec_kernel, pps=pps),
        grid_spec=grid_spec,
        out_shape=jax.ShapeDtypeStruct((nseq, t, D_B), F32),
        compiler_params=_cparams("parallel", "arbitrary"),
        name="dsa_decode",
    )(page_table, qb, bias, bias, *([cache_kt] * pps), *([cache_vt] * pps), kn_pad, vn_pad)


def _rope_tables(pos):
    half = ROT_DIM // 2
    inv = jnp.power(ROPE_THETA, -jnp.arange(half, dtype=F32) / half)
    ang = pos.astype(F32)[:, None] * inv[None, :]
    cos, sin = jnp.cos(ang), jnp.sin(ang)
    n = pos.shape[0]
    ones = jnp.ones((n, HEAD_DIM - ROT_DIM), F32)
    zeros = jnp.zeros((n, HEAD_DIM - ROT_DIM), F32)
    zh = jnp.zeros((n, half), F32)
    tc = jnp.concatenate([cos, cos, ones], axis=1)
    ts1 = jnp.concatenate([-sin, zh, zeros], axis=1)
    ts2 = jnp.concatenate([zh, sin, zeros], axis=1)
    dup = lambda a: jnp.concatenate([a, a], axis=1)
    return dup(tc), dup(ts1), dup(ts2)


def _pad_w_in(w):
    d = w.shape[0]
    main = w[:, :8 * _G]
    o = 8 * _G
    iq = w[:, o:o + 256]
    ik = w[:, o + 256:o + 320]
    iw = w[:, o + 320:o + 324]
    qm = w[:, o + 324:o + 324 + D_M]
    gm = w[:, o + 324 + D_M:o + 324 + 2 * D_M]
    iw_pad = jnp.concatenate([iw, jnp.zeros((d, LANES - IDX_HEADS), w.dtype)], axis=1)
    return jnp.concatenate([main, iq, ik, ik, iw_pad, qm, gm], axis=1).astype(BF16)


def kernel(x_prompt, x_sample, mem_prompt, cache_k_a, cache_v_a, cache_k_b, cache_v_b, cache_k_idx,
           cache_mem_k, cache_mem_v, page_table, norm_g, w_in, mem_norm_g, w_mem_kv, w_out, final_norm_g):
    depth = w_in.shape[0]
    b, t, d = x_prompt.shape
    nseq, ts, _ = x_sample.shape
    n_pages = page_table.shape[1]
    page_size = cache_k_a.shape[2]
    past = n_pages * page_size
    assert t % TILE == 0 and t // TILE <= LANES and page_size == LANES
    assert past % MOBA_BLOCK == 0 and ts <= LANES and ts % 8 == 0

    tabs_p = _rope_tables(jnp.arange(t, dtype=jnp.int32))
    pos_s = past + jnp.arange(ts, dtype=jnp.int32)
    tabs_s = tuple(jnp.tile(a, (nseq, 1)) for a in _rope_tables(pos_s))
    fg = final_norm_g.reshape(1, d)
    nbp = t // TILE

    xp, xs = x_prompt, x_sample
    outs = [[] for _ in range(12)]
    for l in range(depth):
        w_pad = _pad_w_in(w_in[l])
        g = norm_g[l].reshape(1, d)
        wo_bf = w_out[l].astype(BF16)
        last = l == depth - 1

        (qa, ka, va, ga, qb, kb, vb, gb, iq, ik, iw, qm, gm,
         ka_bf, va_t, kb_bf, vb_t, ik_bf, kmean) = _project(xp.reshape(b * t, d), g, w_pad, tabs_p, nbp, TILE, True)
        r3 = lambda a: a.reshape(b, t, a.shape[-1])
        r4 = lambda a: a.reshape(b, nbp, a.shape[-2], TILE)
        mk, mv = _mem_kv(mem_prompt, mem_norm_g[l].reshape(1, d), w_mem_kv[l].astype(BF16))
        kmean_pad = jnp.pad(kmean.reshape(b, nbp, D_A), ((0, 0), (0, LANES - nbp), (0, 0)))
        oa = _moba_prompt(r3(qa), r3(ka_bf), r4(va_t), kmean_pad)
        ob = _dsa_prompt(r3(qb), r3(iq).astype(BF16), r3(iw), r3(ik_bf), r3(kb_bf), r4(vb_t))
        xp = _merge(xp, oa, r3(ga), ob, r3(gb), r3(qm), r3(gm), mk, mv, wo_bf, fg, TILE, last)
        for lst, a, h in ((outs[0], ka, H_A), (outs[1], va, H_A), (outs[2], kb, H_B), (outs[3], vb, H_B)):
            lst.append(jnp.transpose(a.reshape(b, h, HEAD_DIM, t), (0, 3, 1, 2)))
        outs[4].append(jnp.transpose(ik, (0, 2, 1)))
        outs[5].append(mk.reshape(b, N_MEM, H_M, HEAD_DIM))
        outs[6].append(mv.reshape(b, N_MEM, H_M, HEAD_DIM))

        n_s = nseq * ts
        tm_s = n_s if n_s <= TILE else TILE
        (qa, ka, va, ga, qb, kb, vb, gb, iq, ik, iw, qm, gm,
         _, _, _, _, _, _) = _project(xs.reshape(n_s, d), g, w_pad, tabs_s, n_s // tm_s, tm_s, False)
        s3 = lambda a: a.reshape(nseq, ts, a.shape[-1])
        padn = lambda a: jnp.pad(s3(a), ((0, 0), (0, LANES - ts), (0, 0)))
        oa = _moba_decode(l, page_table, s3(qa), _cache_view(cache_k_a), _cache_view(cache_v_a), padn(ka), padn(va))
        iq_rows = s3(iq).reshape(nseq, ts, IDX_HEADS, IDX_DIM).transpose(0, 2, 1, 3).reshape(
            nseq, IDX_HEADS * ts, IDX_DIM)
        iw_rows = (s3(iw)[:, :, :IDX_HEADS] * (IDX_HEADS ** -0.5)).transpose(0, 2, 1).reshape(
            nseq, IDX_HEADS * ts, 1)
        iw_rows = jnp.broadcast_to(iw_rows, (nseq, IDX_HEADS * ts, LANES))
        bias = _idx_decode(l, page_table, iq_rows, iw_rows, jnp.transpose(cache_k_idx, (0, 1, 3, 2)), padn(ik), ts)
        ob = _dsa_decode(l, page_table, s3(qb), bias, _cache_view(cache_k_b), _cache_view(cache_v_b),
                         padn(kb), padn(vb))
        mks = cache_mem_k[l].reshape(nseq, N_MEM, D_M)
        mvs = cache_mem_v[l].reshape(nseq, N_MEM, D_M)
        xs = _merge(xs, oa, s3(ga), ob, s3(gb), s3(qm), s3(gm), mks, mvs, wo_bf, fg, ts, last)
        for lst, a, shp in ((outs[7], ka, (nseq, ts, H_A, HEAD_DIM)), (outs[8], va, (nseq, ts, H_A, HEAD_DIM)),
                            (outs[9], kb, (nseq, ts, H_B, HEAD_DIM)), (outs[10], vb, (nseq, ts, H_B, HEAD_DIM)),
                            (outs[11], ik, (nseq, ts, IDX_DIM))):
            lst.append(a.reshape(shp))

    stk = [jnp.stack(o) for o in outs]
    return (xp, xs, stk[0], stk[1], stk[2], stk[3], stk[4], stk[5], stk[6],
            stk[7], stk[8], stk[9], stk[10], stk[11])
```

```python
import functools

import jax
import jax.numpy as jnp
from jax import lax
from jax.experimental import pallas as pl
from jax.experimental.pallas import tpu as pltpu

HEAD_DIM = 64
H_A = 6
H_B = 6
H_M = 4
D_A = H_A * HEAD_DIM
D_B = H_B * HEAD_DIM
D_M = H_M * HEAD_DIM
N_MEM = 256
MOBA_BLOCK = 256
MOBA_TOPK = 3
IDX_HEADS = 4
IDX_DIM = 64
IDX_TOPK = 256
ROPE_THETA = 500000.0
ROT_DIM = HEAD_DIM // 4
EPS = 1e-6

LANES = 128
TILE = 256
NEG = -1e30
LOG2E = 1.4426950408889634
SUM_ROWS = 16
VT_ROWS = LANES + SUM_ROWS
INT_MIN = -(2 ** 31)
I16_MIN = -(2 ** 15)
NO_POS = 2 ** 15 - 1
VMEM_LIMIT = 60 * 1024 * 1024
IDX_PAGES_PER_STEP = 32
KV_PAGES_PER_STEP = 32

_G = D_A
OFF_QA, OFF_KA, OFF_VA, OFF_GA = 0, _G, 2 * _G, 3 * _G
OFF_QB, OFF_KB, OFF_VB, OFF_GB = 4 * _G, 5 * _G, 6 * _G, 7 * _G
OFF_IQ = 8 * _G
OFF_IK = OFF_IQ + 256
OFF_IW = OFF_IK + LANES
OFF_QM = OFF_IW + LANES
OFF_GM = OFF_QM + D_M
W_PAD = OFF_GM + D_M

F32 = jnp.float32
BF16 = jnp.bfloat16
NT = (((1,), (1,)), ((), ()))


def _cparams(*sem):
    return pltpu.CompilerParams(dimension_semantics=sem, vmem_limit_bytes=VMEM_LIMIT)


def _half_mask(hh):
    lane = lax.broadcasted_iota(jnp.int32, (1, LANES), 1)
    return (lane >= HEAD_DIM * hh) & (lane < HEAD_DIM * (hh + 1))


def _sortable(score):
    bits = lax.bitcast_convert_type(score, jnp.int32)
    key = bits ^ (lax.shift_right_arithmetic(bits, 31) & jnp.int32(0x7FFFFFFF))
    return jnp.where(score == 0.0, jnp.int32(0), key)


def _head_pair_queries(qc):
    scaled = qc * (HEAD_DIM ** -0.5 * LOG2E)
    return jnp.concatenate([jnp.where(_half_mask(0), scaled, 0.0), jnp.where(_half_mask(1), scaled, 0.0)],
                           axis=0).astype(BF16)


def _pair_output(acc):
    t = acc.shape[1] // 2
    o = acc[:LANES] / acc[LANES:LANES + 1]
    d = lax.broadcasted_iota(jnp.int32, (LANES, t), 0)
    return jnp.where(d < HEAD_DIM, o[:, :t], o[:, t:]).T


def _proj_kernel(x_ref, g_ref, w_ref, tc_ref, ts1_ref, ts2_ref,
                 qa_ref, ka_ref, va_ref, ga_ref, qb_ref, kb_ref, vb_ref, gb_ref,
                 iq_ref, ik_ref, iw_ref, qm_ref, gm_ref,
                 kabf_ref, vat_ref, kbbf_ref, vbt_ref, ikbf_ref, kmean_ref, *, kv_transposed):
    x = x_ref[...]
    ms = jnp.mean(x * x, axis=-1, keepdims=True)
    h = (x * lax.rsqrt(ms + EPS) * g_ref[...]).astype(BF16)
    tc, ts1, ts2 = tc_ref[...], ts1_ref[...], ts2_ref[...]

    def mm(off, width):
        return jnp.dot(h, w_ref[:, off:off + width], preferred_element_type=F32)

    def rope(u):
        outs = []
        for c in range(u.shape[1] // LANES):
            uc = u[:, c * LANES:(c + 1) * LANES]
            outs.append(uc * tc + pltpu.roll(uc, LANES - ROT_DIM // 2, 1) * ts1
                        + pltpu.roll(uc, ROT_DIM // 2, 1) * ts2)
        return outs[0] if len(outs) == 1 else jnp.concatenate(outs, axis=1)

    def put(ref, val, val_t=None):
        if kv_transposed:
            ref[0] = val.T if val_t is None else val_t
        else:
            ref[...] = val

    qa_ref[...] = rope(mm(OFF_QA, D_A))
    ka = rope(mm(OFF_KA, D_A))
    put(ka_ref, ka)
    kabf_ref[...] = ka.astype(BF16)
    kmean_ref[0] = jnp.mean(ka, axis=0, keepdims=True)
    va = mm(OFF_VA, D_A)
    va_t = va.T
    put(va_ref, va, va_t)
    vat_ref[0] = _vt_with_ones(va_t)
    ga_ref[...] = mm(OFF_GA, D_A)
    qb_ref[...] = rope(mm(OFF_QB, D_B))
    kb = rope(mm(OFF_KB, D_B))
    put(kb_ref, kb)
    kbbf_ref[...] = kb.astype(BF16)
    vb = mm(OFF_VB, D_B)
    vb_t = vb.T
    put(vb_ref, vb, vb_t)
    vbt_ref[0] = _vt_with_ones(vb_t)
    gb_ref[...] = mm(OFF_GB, D_B)
    iq_ref[...] = rope(mm(OFF_IQ, IDX_HEADS * IDX_DIM))
    ik2 = rope(mm(OFF_IK, LANES))
    put(ik_ref, ik2[:, :IDX_DIM], ik2.T[:IDX_DIM] if kv_transposed else None)
    ikbf_ref[...] = ik2.astype(BF16)
    iw_ref[...] = mm(OFF_IW, LANES)
    qm_ref[...] = mm(OFF_QM, D_M)
    gm_ref[...] = mm(OFF_GM, D_M)


def _project(x2d, g, w_pad, tabs, n_tab_tiles, tm, kv_transposed):
    n, d = x2d.shape
    nt = n // tm
    tok = lambda w: pl.BlockSpec((tm, w), lambda i: (i, 0))
    tab = pl.BlockSpec((tm, LANES), lambda i: (i % n_tab_tiles, 0))
    vt = lambda w: pl.BlockSpec((1, w // LANES * VT_ROWS, tm), lambda i: (i, 0, 0))
    vt_shape = lambda w: jax.ShapeDtypeStruct((nt, w // LANES * VT_ROWS, tm), BF16)
    widths = [D_A] * 4 + [D_B] * 4 + [IDX_HEADS * IDX_DIM, IDX_DIM, LANES, D_M, D_M]
    out_shape = [jax.ShapeDtypeStruct((n, w), F32) for w in widths]
    out_specs = [tok(w) for w in widths]
    if kv_transposed:
        for idx in (1, 2, 5, 6, 9):
            w = widths[idx]
            out_shape[idx] = jax.ShapeDtypeStruct((nt // n_tab_tiles, w, n_tab_tiles * tm), F32)
            out_specs[idx] = pl.BlockSpec((1, w, tm), lambda i: (i // n_tab_tiles, 0, i % n_tab_tiles))
    out_shape += [jax.ShapeDtypeStruct((n, D_A), BF16), vt_shape(D_A),
                  jax.ShapeDtypeStruct((n, D_B), BF16), vt_shape(D_B),
                  jax.ShapeDtypeStruct((n, LANES), BF16), jax.ShapeDtypeStruct((nt, 1, D_A), F32)]
    out_specs += [tok(D_A), vt(D_A), tok(D_B), vt(D_B), tok(LANES), pl.BlockSpec((1, 1, D_A), lambda i: (i, 0, 0))]
    return pl.pallas_call(
        functools.partial(_proj_kernel, kv_transposed=kv_transposed),
        grid=(nt,),
        in_specs=[tok(d),
                  pl.BlockSpec((1, d), lambda i: (0, 0)),
                  pl.BlockSpec((d, W_PAD), lambda i: (0, 0)),
                  tab, tab, tab],
        out_specs=out_specs,
        out_shape=out_shape,
        compiler_params=_cparams("parallel"),
        name="proj",
    )(x2d, g, w_pad, *tabs)


def _memkv_kernel(m_ref, g_ref, w_ref, mk_ref, mv_ref):
    x = m_ref[0]
    ms = jnp.mean(x * x, axis=-1, keepdims=True)
    h = (x * lax.rsqrt(ms + EPS) * g_ref[...]).astype(BF16)
    u = jnp.dot(h, w_ref[...], preferred_element_type=F32)
    mk_ref[0] = u[:, :D_M]
    mv_ref[0] = u[:, D_M:]


def _mem_kv(mem, g, w_bf):
    b, nm, d = mem.shape
    return pl.pallas_call(
        _memkv_kernel,
        grid=(b,),
        in_specs=[pl.BlockSpec((1, nm, d), lambda i: (i, 0, 0)),
                  pl.BlockSpec((1, d), lambda i: (0, 0)),
                  pl.BlockSpec((d, 2 * D_M), lambda i: (0, 0))],
        out_specs=[pl.BlockSpec((1, nm, D_M), lambda i: (i, 0, 0))] * 2,
        out_shape=[jax.ShapeDtypeStruct((b, nm, D_M), F32)] * 2,
        compiler_params=_cparams("parallel"),
        name="mem_kv",
    )(mem, g, w_bf)


def _attention_loop(n_steps, logits, bias, vt_ref, ms, acc_scr, s_scr, p_scr):
    n_pairs = acc_scr.shape[0]
    last = jnp.maximum(n_steps - 1, 0)
    vt_rows = lambda j, c: vt_ref[0, j, c * VT_ROWS:(c + 1) * VT_ROWS, :]
    defer_pv = p_scr is not None
    s_scr[0] = logits(0, 0)
    if defer_pv:
        p_scr[n_pairs - 1] = jnp.zeros(p_scr.shape[1:], BF16)

    def accumulate(c, alpha, j, p):
        acc_scr[c] = alpha * acc_scr[c] + jnp.dot(vt_rows(j, c), p, preferred_element_type=F32)

    def body(j, carry):
        ms, alphas = list(carry[0]), list(carry[1])
        b = bias(j)
        for c in range(n_pairs):
            if c + 1 < n_pairs:
                s_scr[c + 1] = logits(j, c + 1)
            else:
                s_scr[0] = logits(jnp.minimum(j + 1, last), 0)
            if defer_pv:
                cp = (c - 1) % n_pairs
                accumulate(cp, alphas[cp], j if c > 0 else jnp.maximum(j - 1, 0), p_scr[cp])
            s = s_scr[c] if b is None else s_scr[c] + b
            m_new = jnp.maximum(ms[c], jnp.max(s, axis=0, keepdims=True))
            alphas[c] = jnp.exp2(ms[c] - m_new)
            p = jnp.exp2(s - m_new).astype(BF16)
            ms[c] = m_new
            if defer_pv:
                p_scr[c] = p
            else:
                accumulate(c, alphas[c], j, p)
        return tuple(ms), tuple(alphas)

    ones = tuple(jnp.ones_like(m) for m in ms)
    ms, alphas = lax.fori_loop(0, n_steps, body, (tuple(ms), ones))
    if defer_pv:
        accumulate(n_pairs - 1, alphas[n_pairs - 1], last, p_scr[n_pairs - 1])


def _vt_with_ones(v_t):
    vt = v_t.astype(BF16)
    ones = jnp.ones((SUM_ROWS, v_t.shape[1]), BF16)
    parts = []
    for c in range(v_t.shape[0] // LANES):
        parts += [vt[c * LANES:(c + 1) * LANES], ones]
    return jnp.concatenate(parts, axis=0)


def _moba_kernel(q_ref, k_ref, vt_ref, km_ref, o_ref, qaug_scr, acc_scr, s_scr):
    qi = pl.program_id(1)
    n_pairs = D_A // LANES
    blk_id = lax.broadcasted_iota(jnp.int32, (TILE, LANES), 1)
    blk_f = blk_id.astype(F32)
    past = blk_id < qi
    for c in range(n_pairs):
        cs = slice(c * LANES, (c + 1) * LANES)
        qc = q_ref[0, :, cs]
        kmc = km_ref[0, :, cs]
        biases = []
        for hh in range(2):
            qm = jnp.where(_half_mask(hh), qc, 0.0)
            g = lax.dot_general(qm, kmc, NT, precision=lax.Precision.HIGHEST, preferred_element_type=F32)
            g = jnp.where(past, g, -jnp.inf)
            sel = jnp.zeros((TILE, LANES), jnp.bool_)
            for _ in range(MOBA_TOPK):
                m = jnp.max(g, axis=1, keepdims=True)
                first = jnp.min(jnp.where(g == m, blk_f, 1e9), axis=1, keepdims=True)
                pick = (blk_f == first) & past
                sel = sel | pick
                g = jnp.where(pick, -jnp.inf, g)
            biases.append(jnp.where(sel, 0.0, NEG).astype(BF16))
        qaug_scr[c] = jnp.concatenate([_head_pair_queries(qc), jnp.concatenate(biases, axis=0)], axis=1)

    own0 = pl.multiple_of(qi * TILE, TILE)
    ko = k_ref[0, pl.ds(own0, TILE), :]
    vto = vt_ref[0, qi]
    krow = lax.broadcasted_iota(jnp.int32, (TILE, 2 * TILE), 0)
    qcol = lax.broadcasted_iota(jnp.int32, (TILE, 2 * TILE), 1) % TILE
    ms = []
    for c in range(n_pairs):
        cs = slice(c * LANES, (c + 1) * LANES)
        s = lax.dot_general(ko[:, cs], qaug_scr[c, :, :LANES], NT, preferred_element_type=F32)
        s = jnp.where(krow <= qcol, s, NEG)
        m = jnp.max(s, axis=0, keepdims=True)
        p = jnp.exp2(s - m).astype(BF16)
        ms.append(m)
        acc_scr[c] = jnp.dot(vto[c * VT_ROWS:(c + 1) * VT_ROWS, :], p, preferred_element_type=F32)

    def logits(j, c):
        j0 = pl.multiple_of(j * TILE, TILE)
        onehot = jnp.where(blk_id == j, 1.0, 0.0).astype(BF16)
        k_aug = jnp.concatenate([k_ref[0, pl.ds(j0, TILE), c * LANES:(c + 1) * LANES], onehot], axis=1)
        return lax.dot_general(k_aug, qaug_scr[c], NT, preferred_element_type=F32)

    _attention_loop(qi, logits, lambda j: None, vt_ref, ms, acc_scr, s_scr, None)
    o_ref[0] = jnp.concatenate([_pair_output(acc_scr[c]) for c in range(n_pairs)], axis=1)


def _moba_prompt(qa, ka_bf, va_t, kmean_pad):
    b, t, _ = qa.shape
    nt = t // TILE
    return pl.pallas_call(
        _moba_kernel,
        grid=(b, nt),
        in_specs=[pl.BlockSpec((1, TILE, D_A), lambda i, j: (i, j, 0)),
                  pl.BlockSpec((1, t, D_A), lambda i, j: (i, 0, 0)),
                  pl.BlockSpec((1, nt, va_t.shape[2], TILE), lambda i, j: (i, 0, 0, 0)),
                  pl.BlockSpec((1, LANES, D_A), lambda i, j: (i, 0, 0))],
        out_specs=pl.BlockSpec((1, TILE, D_A), lambda i, j: (i, j, 0)),
        out_shape=jax.ShapeDtypeStruct((b, t, D_A), F32),
        scratch_shapes=[pltpu.VMEM((D_A // LANES, 2 * TILE, 2 * LANES), BF16),
                        pltpu.VMEM((D_A // LANES, VT_ROWS, 2 * TILE), F32),
                        pltpu.VMEM((D_A // LANES, TILE, 2 * TILE), F32)],
        compiler_params=_cparams("parallel", "arbitrary"),
        name="moba_prompt",
    )(qa, ka_bf, va_t, kmean_pad)


def _dsa_kernel(q_ref, iq_ref, iw_ref, ik_ref, k_ref, vt_ref, o_ref,
                key_scr, hi_scr, lo_scr, rel_a, rel_b, acc_scr, s_scr, p_scr, *, k_sel, idx_bits):
    qi = pl.program_id(1)
    n_tiles = qi + 1
    n_pairs = D_B // LANES
    kf = float(k_sel)
    krow = lax.broadcasted_iota(jnp.int32, (TILE, TILE), 0)
    q_pos = lax.broadcasted_iota(jnp.int32, (TILE, TILE), 1) + qi * TILE
    key_pos = lambda j: krow + j * TILE

    iq = iq_ref[0]
    iw_t = (iw_ref[0] * (IDX_HEADS ** -0.5)).T
    iqm = [(jnp.where(_half_mask(n % 2), iq[:, (n // 2) * LANES:(n // 2 + 1) * LANES], 0.0)
            * (IDX_DIM ** -0.5)).astype(BF16) for n in range(IDX_HEADS)]

    def relevances(j, out_ref):
        j0 = pl.multiple_of(j * TILE, TILE)
        ikj = ik_ref[0, pl.ds(j0, TILE), :]
        for n in range(IDX_HEADS):
            out_ref[n] = lax.dot_general(ikj, iqm[n], NT, preferred_element_type=F32)

    def store_keys(j, rel_ref):
        score = jnp.zeros((TILE, TILE), F32)
        for n in range(IDX_HEADS):
            score = score + iw_t[n:n + 1, :] * jnp.maximum(rel_ref[n], 0.0)
        key = jnp.where(key_pos(j) > q_pos, jnp.int32(INT_MIN), _sortable(score))
        key_scr[j] = key
        hi_scr[j] = lax.shift_right_arithmetic(key, 16).astype(jnp.int16)
        lo_scr[j] = ((key & 0xFFFF) + I16_MIN).astype(jnp.int16)

    last = n_tiles - 1
    relevances(0, rel_a)

    def score_body(i, carry):
        j = 2 * i
        relevances(jnp.minimum(j + 1, last), rel_b)
        store_keys(j, rel_a)
        relevances(jnp.minimum(j + 2, last), rel_a)
        store_keys(jnp.minimum(j + 1, last), rel_b)
        return carry

    n_pairs_kt = (n_tiles + 1) // 2
    lax.fori_loop(0, n_pairs_kt, score_body, 0)

    @pl.when(n_tiles % 2 == 1)
    def _():
        hi_scr[n_tiles] = jnp.full((TILE, TILE), I16_MIN, jnp.int16)
        lo_scr[n_tiles] = jnp.full((TILE, TILE), I16_MIN, jnp.int16)

    def count16(plane, pred):
        def body(i, acc):
            for jj in (2 * i, 2 * i + 1):
                w = jnp.where(pred(plane[jj]), jnp.int16(1), jnp.int16(0))
                for r in range(TILE // 64):
                    acc = acc + w[r * 64:(r + 1) * 64]
            return acc
        acc = lax.fori_loop(0, n_pairs_kt, body, jnp.zeros((64, TILE), jnp.int16))
        return jnp.sum(acc.astype(jnp.int32).astype(F32), axis=0, keepdims=True)

    def search16(plane, target):
        zero = jnp.zeros((1, TILE), jnp.int32)
        total = (n_tiles * TILE).astype(F32)
        c0 = count16(plane, lambda x: x >= zero.astype(jnp.int16))
        ok0 = c0 >= target
        init = (jnp.where(ok0, zero, jnp.int32(I16_MIN)), jnp.where(ok0, c0, total))

        def bit(i, carry):
            thr, cnt = carry
            cand = thr | lax.shift_left(jnp.int32(1), 14 - i)
            c = count16(plane, lambda x: x >= cand.astype(jnp.int16))
            ok = c >= target
            return jnp.where(ok, cand, thr), jnp.where(ok, c, cnt)

        return lax.fori_loop(0, 15, bit, init)

    thr_hi, cnt_hi = search16(hi_scr, kf)
    thr_hi16 = thr_hi.astype(jnp.int16)
    above = count16(hi_scr, lambda x: x > thr_hi16)

    def mask_lo(i, carry):
        for jj in (2 * i, 2 * i + 1):
            lo_scr[jj] = jnp.where(hi_scr[jj] == thr_hi16, lo_scr[jj], jnp.int16(I16_MIN))
        return carry

    lax.fori_loop(0, n_pairs_kt, mask_lo, 0)
    thr_lo, cnt_lo = search16(lo_scr, kf - above)
    thr = thr_hi * 65536 + (thr_lo - I16_MIN)
    cnt_ge = above + jnp.where(thr_lo > I16_MIN, cnt_lo, cnt_hi - above)
    has_ties = jnp.max(cnt_ge) > kf

    @pl.when(jnp.logical_not(has_ties))
    def _():
        floor = jnp.maximum(thr, jnp.int32(INT_MIN + 1))

        def to_bias(j, carry):
            key_scr[j] = lax.bitcast_convert_type(jnp.where(key_scr[j] >= floor, 0.0, NEG).astype(F32), jnp.int32)
            return carry

        lax.fori_loop(0, n_tiles, to_bias, 0)

    @pl.when(has_ties)
    def _():
        def tie_positions(j, carry):
            lo_scr[j] = jnp.where(key_scr[j] == thr, key_pos(j), jnp.int32(NO_POS)).astype(jnp.int16)
            return carry

        lax.fori_loop(0, n_tiles, tie_positions, 0)

        @pl.when(n_tiles % 2 == 1)
        def _():
            lo_scr[n_tiles] = jnp.full((TILE, TILE), NO_POS, jnp.int16)

        n_tied = count16(lo_scr, lambda x: x < jnp.int16(NO_POS))
        need = kf - (cnt_ge - n_tied)

        def cbit(i, cpos):
            cand = cpos | lax.shift_left(jnp.int32(1), idx_bits - 1 - i)
            cnt = count16(lo_scr, lambda x: x < cand.astype(jnp.int16))
            return jnp.where(cnt < need, cand, cpos)

        cstar = lax.fori_loop(0, idx_bits, cbit, jnp.zeros((1, TILE), jnp.int32))

        floor = jnp.maximum(thr, jnp.int32(INT_MIN + 1))

        def to_bias(j, carry):
            need_key = jnp.where(key_pos(j) <= cstar, floor, floor + 1)
            key_scr[j] = lax.bitcast_convert_type(jnp.where(key_scr[j] >= need_key, 0.0, NEG).astype(F32), jnp.int32)
            return carry

        lax.fori_loop(0, n_tiles, to_bias, 0)

    q_pairs = [_head_pair_queries(q_ref[0, :, c * LANES:(c + 1) * LANES]) for c in range(n_pairs)]
    for c in range(n_pairs):
        acc_scr[c] = jnp.zeros((VT_ROWS, 2 * TILE), F32)

    def logits(j, c):
        j0 = pl.multiple_of(j * TILE, TILE)
        return lax.dot_general(k_ref[0, pl.ds(j0, TILE), c * LANES:(c + 1) * LANES], q_pairs[c], NT,
                               preferred_element_type=F32)

    def bias(j):
        b = lax.bitcast_convert_type(key_scr[j], F32)
        return jnp.concatenate([b, b], axis=1)

    ms = [jnp.full((1, 2 * TILE), NEG, F32) for _ in range(n_pairs)]
    _attention_loop(n_tiles, logits, bias, vt_ref, ms, acc_scr, s_scr, p_scr)
    o_ref[0] = jnp.concatenate([_pair_output(acc_scr[c]) for c in range(n_pairs)], axis=1)


def _dsa_prompt(qb, iq, iw, ik_bf, kb_bf, vb_t):
    b, t, _ = qb.shape
    nt = t // TILE
    k_sel = min(IDX_TOPK, t // 4)
    assert t <= NO_POS and 2 * nt < 2 ** 15
    kern = functools.partial(_dsa_kernel, k_sel=k_sel, idx_bits=max(1, (t - 1).bit_length()))
    res = lambda w: pl.BlockSpec((1, t, w), lambda i, j: (i, 0, 0))
    til = lambda w: pl.BlockSpec((1, TILE, w), lambda i, j: (i, j, 0))
    return pl.pallas_call(
        kern,
        grid=(b, nt),
        in_specs=[til(D_B), til(IDX_HEADS * IDX_DIM), til(LANES), res(LANES), res(D_B),
                  pl.BlockSpec((1, nt, vb_t.shape[2], TILE), lambda i, j: (i, 0, 0, 0))],
        out_specs=til(D_B),
        out_shape=jax.ShapeDtypeStruct((b, t, D_B), F32),
        scratch_shapes=[pltpu.VMEM((nt, TILE, TILE), jnp.int32),
                        pltpu.VMEM((nt + nt % 2, TILE, TILE), jnp.int16),
                        pltpu.VMEM((nt + nt % 2, TILE, TILE), jnp.int16),
                        pltpu.VMEM((IDX_HEADS, TILE, TILE), F32),
                        pltpu.VMEM((IDX_HEADS, TILE, TILE), F32),
                        pltpu.VMEM((D_B // LANES, VT_ROWS, 2 * TILE), F32),
                        pltpu.VMEM((D_B // LANES, TILE, 2 * TILE), F32),
                        pltpu.VMEM((D_B // LANES, TILE, 2 * TILE), BF16)],
        compiler_params=_cparams("parallel", "arbitrary"),
        name="dsa_prompt",
    )(qb, iq, iw, ik_bf, kb_bf, vb_t)


def _merge_kernel(x_ref, oa_ref, ga_ref, ob_ref, gb_ref, qm_ref, gm_ref, mk_ref, mv_ref, wo_ref, fg_ref, y_ref,
                  *, final):
    lane = lax.broadcasted_iota(jnp.int32, (1, LANES), 1)
    mk = mk_ref[0].astype(BF16)
    mv = mv_ref[0].astype(BF16)
    oms = []
    for c in range(D_M // LANES):
        cs = slice(c * LANES, (c + 1) * LANES)
        qc = qm_ref[0, :, cs]
        heads = []
        for hh in range(2):
            q_bf = (jnp.where(_half_mask(hh), qc, 0.0) * (HEAD_DIM ** -0.5)).astype(BF16)
            s = lax.dot_general(q_bf, mk[:, cs], NT, preferred_element_type=F32)
            m = jnp.max(s, axis=1, keepdims=True)
            p = jnp.exp(s - m)
            l = jnp.sum(p, axis=1, keepdims=True)
            heads.append(jnp.dot(p.astype(BF16), mv[:, cs], preferred_element_type=F32) / l)
        oms.append(jnp.where(lane < HEAD_DIM, heads[0], heads[1]))
    om = jnp.concatenate(oms, axis=1)
    silu = lambda g: g / (1.0 + jnp.exp(-g))
    ymix = jnp.concatenate([oa_ref[0] * silu(ga_ref[0]), ob_ref[0] * silu(gb_ref[0]), om * silu(gm_ref[0])],
                           axis=1).astype(BF16)
    z = x_ref[0] + jnp.dot(ymix, wo_ref[...], preferred_element_type=F32)
    if final:
        ms = jnp.mean(z * z, axis=-1, keepdims=True)
        z = z * lax.rsqrt(ms + EPS) * fg_ref[...]
    y_ref[0] = z


def _merge(x, oa, ga, ob, gb, qm, gm, mk, mv, wo_bf, fg, tm, final):
    b, t, d = x.shape
    til = lambda w: pl.BlockSpec((1, tm, w), lambda i, j: (i, j, 0))
    memspec = pl.BlockSpec((1, N_MEM, D_M), lambda i, j: (i, 0, 0))
    return pl.pallas_call(
        functools.partial(_merge_kernel, final=final),
        grid=(b, t // tm),
        in_specs=[til(d), til(D_A), til(D_A), til(D_B), til(D_B), til(D_M), til(D_M), memspec, memspec,
                  pl.BlockSpec((d, d), lambda i, j: (0, 0)),
                  pl.BlockSpec((1, d), lambda i, j: (0, 0))],
        out_specs=til(d),
        out_shape=jax.ShapeDtypeStruct((b, t, d), F32),
        compiler_params=_cparams("parallel", "parallel"),
        name="merge",
    )(x, oa, ga, ob, gb, qm, gm, mk, mv, wo_bf, fg)


def _expand_heads(q, n_heads):
    t, w = q.shape
    qt = jnp.concatenate([q] * n_heads, axis=0)
    r = lax.broadcasted_iota(jnp.int32, (n_heads * t, w), 0)
    l = lax.broadcasted_iota(jnp.int32, (n_heads * t, w), 1)
    keep = (l >= (r // t) * HEAD_DIM) & (l < (r // t + 1) * HEAD_DIM)
    return jnp.where(keep, qt, 0.0)


def _collapse_heads(o, n_heads, t):
    w = o.shape[1]
    l = lax.broadcasted_iota(jnp.int32, (t, w), 1)
    out = jnp.zeros((t, w), F32)
    for h in range(n_heads):
        keep = (l >= h * HEAD_DIM) & (l < (h + 1) * HEAD_DIM)
        out = out + jnp.where(keep, o[h * t:(h + 1) * t, :], 0.0)
    return out


def _pages_t(refs):
    pages = [r[0, 0].reshape(r.shape[2] * r.shape[3], r.shape[4]) for r in refs]
    return pages[0] if len(pages) == 1 else jnp.concatenate(pages, axis=1)


def _cache_view(cache):
    return jnp.transpose(cache, (0, 1, 3, 4, 2))


def _values_with_ones(vt):
    return jnp.concatenate([vt.astype(BF16), jnp.ones((SUM_ROWS, vt.shape[1]), BF16)], axis=0)


def _softmax_pv(s_scr, vt_scr, s_new, v_new):
    nsteps = s_scr.shape[0]
    d = v_new.shape[1]
    m = jnp.max(jnp.max(s_scr[...], axis=0), axis=1, keepdims=True)
    m = jnp.maximum(m, jnp.max(s_new, axis=1, keepdims=True))
    p_new = jnp.exp(s_new - m)
    l_new = jnp.sum(p_new, axis=1, keepdims=True)
    acc_new = jnp.dot(p_new.astype(BF16), v_new, preferred_element_type=F32)

    def body(i, acc):
        p = jnp.exp(s_scr[i] - m).astype(BF16)
        return acc + lax.dot_general(p, vt_scr[i], NT, preferred_element_type=F32)

    acc = lax.fori_loop(0, nsteps, body, jnp.zeros((s_new.shape[0], d + SUM_ROWS), F32),
                        unroll=nsteps if nsteps <= 8 else 8)
    return (acc[:, :d] + acc_new) / (acc[:, d:d + 1] + l_new)


def _paged_specs(layer, block_tail, per_step):
    zeros = (0,) * len(block_tail)
    return [pl.BlockSpec((1, 1) + block_tail, lambda s, j, pt, i=i: (layer, pt[s, j * per_step + i]) + zeros)
            for i in range(per_step)]


def _moba_dec_kernel(pt_ref, q_ref, *rest, ppb, bps):
    n_pg = ppb * bps
    k_refs, v_refs = rest[:n_pg], rest[n_pg:2 * n_pg]
    kn_ref, vn_ref, o_ref, s_scr, vt_scr, g_scr, qxt_scr = rest[2 * n_pg:]
    j = pl.program_id(1)
    nsteps = pl.num_programs(1)
    t = q_ref.shape[1]
    rows = H_A * t
    blk_w = s_scr.shape[2] // bps
    qx = _expand_heads(q_ref[0], H_A)
    qx_bf = (qx * (HEAD_DIM ** -0.5)).astype(BF16)

    @pl.when(j == 0)
    def _():
        qxt_scr[...] = jnp.concatenate([qx, jnp.zeros((LANES - rows, D_A), F32)], axis=0).T

    kt = _pages_t(k_refs)
    s_scr[j] = jnp.dot(qx_bf, kt.astype(BF16), preferred_element_type=F32)
    vt_scr[j] = _values_with_ones(_pages_t(v_refs))
    for b in range(bps):
        kmean = jnp.mean(kt[:, b * blk_w:(b + 1) * blk_w], axis=1, keepdims=True)
        g_scr[pl.ds(j * bps + b, 1), :] = jnp.sum(qxt_scr[...] * kmean, axis=0, keepdims=True)

    @pl.when(j == nsteps - 1)
    def _():
        npad = kn_ref.shape[1]
        r = lax.broadcasted_iota(jnp.int32, (rows, npad), 0)
        cidx = lax.broadcasted_iota(jnp.int32, (rows, npad), 1)
        s_new = lax.dot_general(qx_bf, kn_ref[0].astype(BF16), NT, preferred_element_type=F32)
        s_new = jnp.where(cidx <= (r % t), s_new, NEG)

        n_blocks = g_scr.shape[0]
        g = g_scr[...]
        bidx = lax.broadcasted_iota(jnp.int32, g.shape, 0).astype(F32)
        sel = jnp.zeros(g.shape, jnp.bool_)
        for _ in range(min(MOBA_TOPK, n_blocks)):
            mx = jnp.max(g, axis=0, keepdims=True)
            first = jnp.min(jnp.where(g == mx, bidx, 1e9), axis=0, keepdims=True)
            pick = bidx == first
            sel = sel | pick
            g = jnp.where(pick, -jnp.inf, g)
        bias_t = jnp.where(sel, 0.0, NEG).T
        for i in range(s_scr.shape[0]):
            cols = [jnp.broadcast_to(bias_t[:rows, i * bps + b:i * bps + b + 1], (rows, blk_w)) for b in range(bps)]
            s_scr[i] = s_scr[i] + (cols[0] if bps == 1 else jnp.concatenate(cols, axis=1))
        o_ref[0] = _collapse_heads(_softmax_pv(s_scr, vt_scr, s_new, vn_ref[0].astype(BF16)), H_A, t)


def _moba_decode(layer, page_table, qa, cache_kt, cache_vt, kn_pad, vn_pad):
    nseq, t, _ = qa.shape
    n_pages = page_table.shape[1]
    page_size = cache_kt.shape[4]
    ppb = MOBA_BLOCK // page_size
    nb = n_pages // ppb
    bps = max(1, KV_PAGES_PER_STEP // ppb)
    while nb % bps:
        bps //= 2
    rows = H_A * t
    seq = lambda w, r: pl.BlockSpec((1, r, w), lambda s, j, pt: (s, 0, 0))
    kspecs = _paged_specs(layer, cache_kt.shape[2:], ppb * bps)
    grid_spec = pltpu.PrefetchScalarGridSpec(
        num_scalar_prefetch=1,
        grid=(nseq, nb // bps),
        in_specs=[seq(D_A, t)] + kspecs + kspecs + [seq(D_A, kn_pad.shape[1])] * 2,
        out_specs=seq(D_A, t),
        scratch_shapes=[pltpu.VMEM((nb // bps, rows, bps * MOBA_BLOCK), F32),
                        pltpu.VMEM((nb // bps, D_A + SUM_ROWS, bps * MOBA_BLOCK), BF16),
                        pltpu.VMEM((nb, LANES), F32),
                        pltpu.VMEM((D_A, LANES), F32)],
    )
    assert rows <= LANES
    return pl.pallas_call(
        functools.partial(_moba_dec_kernel, ppb=ppb, bps=bps),
        grid_spec=grid_spec,
        out_shape=jax.ShapeDtypeStruct((nseq, t, D_A), F32),
        compiler_params=_cparams("parallel", "arbitrary"),
        name="moba_decode",
    )(page_table, qa, *([cache_kt] * (ppb * bps)), *([cache_vt] * (ppb * bps)), kn_pad, vn_pad)


def _idx_dec_kernel(pt_ref, iq_ref, iw_ref, *rest, per_step, k_sel, idx_bits, past, t):
    ik_refs = rest[:per_step]
    ikn_ref, bias_ref, key_scr = rest[per_step:]
    j = pl.program_id(1)
    nsteps = pl.num_programs(1)
    width = key_scr.shape[2]
    iq_bf = iq_ref[0].astype(BF16)
    iw = iw_ref[0][:, :1]

    def scores(rel):
        wrel = iw * jnp.maximum(rel * (IDX_DIM ** -0.5), 0.0)
        sc = wrel[0:t]
        for n in range(1, IDX_HEADS):
            sc = sc + wrel[n * t:(n + 1) * t]
        return sc

    ik_t = jnp.concatenate([r[0, 0] for r in ik_refs], axis=1)
    key_scr[j] = _sortable(scores(jnp.dot(iq_bf, ik_t.astype(BF16), preferred_element_type=F32)))

    @pl.when(j == nsteps - 1)
    def _():
        kf = float(k_sel)
        npad = ikn_ref.shape[1]
        r_new = lax.broadcasted_iota(jnp.int32, (t, npad), 0)
        c_new = lax.broadcasted_iota(jnp.int32, (t, npad), 1)
        rel_new = lax.dot_general(iq_bf, ikn_ref[0].astype(BF16), NT, preferred_element_type=F32)
        key_new = jnp.where(c_new <= r_new, _sortable(scores(rel_new)), jnp.int32(INT_MIN))
        tail = jnp.full((t, width - npad), jnp.int32(INT_MIN))
        key_scr[nsteps] = jnp.concatenate([key_new, tail], axis=1)

        keys = key_scr[...]
        pos = (lax.broadcasted_iota(jnp.int32, keys.shape, 0) * width
               + lax.broadcasted_iota(jnp.int32, keys.shape, 2))

        def count(pred):
            w = jnp.sum(jnp.where(pred, 1.0, 0.0), axis=0)
            acc = w[:, :LANES]
            for s in range(1, width // LANES):
                acc = acc + w[:, s * LANES:(s + 1) * LANES]
            return jnp.sum(acc, axis=1, keepdims=True)[None]

        zero = jnp.zeros((1, t, 1), jnp.int32)
        thr = jnp.where(count(keys >= zero) >= kf, zero, jnp.int32(INT_MIN))

        def bit(i, thr):
            cand = thr | lax.shift_left(jnp.int32(1), 30 - i)
            return jnp.where(count(keys >= cand) >= kf, cand, thr)

        thr = lax.fori_loop(0, 31, bit, thr)
        need = kf - count(keys > thr)

        def cbit(i, cpos):
            cand = cpos | lax.shift_left(jnp.int32(1), idx_bits - 1 - i)
            return jnp.where(count((keys == thr) & (pos < cand)) < need, cand, cpos)

        cstar = lax.fori_loop(0, idx_bits, cbit, zero)
        keep = (keys > thr) | ((keys == thr) & (pos <= cstar))
        bias_ref[0] = jnp.where(keep, 0.0, NEG).astype(F32)


def _idx_decode(layer, page_table, iq_rows, iw_rows, cache_ik, ikn_pad, t):
    nseq = iq_rows.shape[0]
    n_pages = page_table.shape[1]
    page_size = cache_ik.shape[3]
    per_step = IDX_PAGES_PER_STEP
    while n_pages % per_step:
        per_step //= 2
    nsteps = n_pages // per_step
    width = per_step * page_size
    past = n_pages * page_size
    k_sel = min(IDX_TOPK, (past + t) // 4)
    idx_bits = max(1, ((nsteps + 1) * width - 1).bit_length())
    seq = lambda r, w: pl.BlockSpec((1, r, w), lambda s, j, pt: (s, 0, 0))
    grid_spec = pltpu.PrefetchScalarGridSpec(
        num_scalar_prefetch=1,
        grid=(nseq, nsteps),
        in_specs=[seq(IDX_HEADS * t, IDX_DIM), seq(IDX_HEADS * t, LANES)]
        + _paged_specs(layer, cache_ik.shape[2:], per_step) + [seq(ikn_pad.shape[1], IDX_DIM)],
        out_specs=pl.BlockSpec((1, nsteps + 1, t, width), lambda s, j, pt: (s, 0, 0, 0)),
        scratch_shapes=[pltpu.VMEM((nsteps + 1, t, width), jnp.int32)],
    )
    kern = functools.partial(_idx_dec_kernel, per_step=per_step, k_sel=k_sel, idx_bits=idx_bits, past=past, t=t)
    return pl.pallas_call(
        kern,
        grid_spec=grid_spec,
        out_shape=jax.ShapeDtypeStruct((nseq, nsteps + 1, t, width), F32),
        compiler_params=_cparams("parallel", "arbitrary"),
        name="idx_decode",
    )(page_table, iq_rows, iw_rows, *([cache_ik] * per_step), ikn_pad)


def _dsa_dec_kernel(pt_ref, q_ref, b_ref, bn_ref, *rest, pps):
    k_refs, v_refs = rest[:pps], rest[pps:2 * pps]
    kn_ref, vn_ref, o_ref, s_scr, vt_scr = rest[2 * pps:]
    j = pl.program_id(1)
    nsteps = pl.num_programs(1)
    t = q_ref.shape[1]
    qx_bf = (_expand_heads(q_ref[0], H_B) * (HEAD_DIM ** -0.5)).astype(BF16)

    kt = _pages_t(k_refs).astype(BF16)
    s_scr[j] = jnp.dot(qx_bf, kt, preferred_element_type=F32) + jnp.concatenate([b_ref[0, 0]] * H_B, axis=0)
    vt_scr[j] = _values_with_ones(_pages_t(v_refs))

    @pl.when(j == nsteps - 1)
    def _():
        s_new = lax.dot_general(qx_bf, kn_ref[0].astype(BF16), NT, preferred_element_type=F32)
        s_new = s_new + jnp.concatenate([bn_ref[0, 0]] * H_B, axis=0)
        o_ref[0] = _collapse_heads(_softmax_pv(s_scr, vt_scr, s_new, vn_ref[0].astype(BF16)), H_B, t)


def _dsa_decode(layer, page_table, qb, bias, cache_kt, cache_vt, kn_pad, vn_pad):
    nseq, t, _ = qb.shape
    n_pages = page_table.shape[1]
    page_size = cache_kt.shape[4]
    pps = KV_PAGES_PER_STEP
    while n_pages % pps or bias.shape[3] % (pps * page_size):
        pps //= 2
    width = bias.shape[3]
    chunk = pps * page_size
    assert n_pages % pps == 0 and width % chunk == 0
    per_w = width // chunk
    rows = H_B * t
    npad = kn_pad.shape[1]
    seq = lambda w, r: pl.BlockSpec((1, r, w), lambda s, j, pt: (s, 0, 0))
    bspec = pl.BlockSpec((1, 1, t, chunk), lambda s, j, pt: (s, j // per_w, 0, j % per_w))
    bnew = pl.BlockSpec((1, 1, t, npad), lambda s, j, pt: (s, bias.shape[1] - 1, 0, 0))
    kspecs = _paged_specs(layer, cache_kt.shape[2:], pps)
    grid_spec = pltpu.PrefetchScalarGridSpec(
        num_scalar_prefetch=1,
        grid=(nseq, n_pages // pps),
        in_specs=[seq(D_B, t), bspec, bnew] + kspecs + kspecs + [seq(D_B, npad)] * 2,
        out_specs=seq(D_B, t),
        scratch_shapes=[pltpu.VMEM((n_pages // pps, rows, chunk), F32),
                        pltpu.VMEM((n_pages // pps, D_B + SUM_ROWS, chunk), BF16)],
    )
    return pl.pallas_call(
        functools.partial(_dsa_dec_kernel, pps=pps),
        grid_spec=grid_spec,
        out_shape=jax.ShapeDtypeStruct((nseq, t, D_B), F32),
        compiler_params=_cparams("parallel", "arbitrary"),
        name="dsa_decode",
    )(page_table, qb, bias, bias, *([cache_kt] * pps), *([cache_vt] * pps), kn_pad, vn_pad)


def _rope_tables(pos):
    half = ROT_DIM // 2
    inv = jnp.power(ROPE_THETA, -jnp.arange(half, dtype=F32) / half)
    ang = pos.astype(F32)[:, None] * inv[None, :]
    cos, sin = jnp.cos(ang), jnp.sin(ang)
    n = pos.shape[0]
    ones = jnp.ones((n, HEAD_DIM - ROT_DIM), F32)
    zeros = jnp.zeros((n, HEAD_DIM - ROT_DIM), F32)
    zh = jnp.zeros((n, half), F32)
    tc = jnp.concatenate([cos, cos, ones], axis=1)
    ts1 = jnp.concatenate([-sin, zh, zeros], axis=1)
    ts2 = jnp.concatenate([zh, sin, zeros], axis=1)
    dup = lambda a: jnp.concatenate([a, a], axis=1)
    return dup(tc), dup(ts1), dup(ts2)


def _pad_w_in(w):
    d = w.shape[0]
    main = w[:, :8 * _G]
    o = 8 * _G
    iq = w[:, o:o + 256]
    ik = w[:, o + 256:o + 320]
    iw = w[:, o + 320:o + 324]
    qm = w[:, o + 324:o + 324 + D_M]
    gm = w[:, o + 324 + D_M:o + 324 + 2 * D_M]
    iw_pad = jnp.concatenate([iw, jnp.zeros((d, LANES - IDX_HEADS), w.dtype)], axis=1)
    return jnp.concatenate([main, iq, ik, ik, iw_pad, qm, gm], axis=1).astype(BF16)


def kernel(x_prompt, x_sample, mem_prompt, cache_k_a, cache_v_a, cache_k_b, cache_v_b, cache_k_idx,
           cache_mem_k, cache_mem_v, page_table, norm_g, w_in, mem_norm_g, w_mem_kv, w_out, final_norm_g):
    depth = w_in.shape[0]
    b, t, d = x_prompt.shape
    nseq, ts, _ = x_sample.shape
    n_pages = page_table.shape[1]
    page_size = cache_k_a.shape[2]
    past = n_pages * page_size
    assert t % TILE == 0 and t // TILE <= LANES and page_size == LANES
    assert past % MOBA_BLOCK == 0 and ts <= LANES and ts % 8 == 0

    tabs_p = _rope_tables(jnp.arange(t, dtype=jnp.int32))
    pos_s = past + jnp.arange(ts, dtype=jnp.int32)
    tabs_s = tuple(jnp.tile(a, (nseq, 1)) for a in _rope_tables(pos_s))
    fg = final_norm_g.reshape(1, d)
    nbp = t // TILE

    xp, xs = x_prompt, x_sample
    outs = [[] for _ in range(12)]
    for l in range(depth):
        w_pad = _pad_w_in(w_in[l])
        g = norm_g[l].reshape(1, d)
        wo_bf = w_out[l].astype(BF16)
        last = l == depth - 1

        (qa, ka, va, ga, qb, kb, vb, gb, iq, ik, iw, qm, gm,
         ka_bf, va_t, kb_bf, vb_t, ik_bf, kmean) = _project(xp.reshape(b * t, d), g, w_pad, tabs_p, nbp, TILE, True)
        r3 = lambda a: a.reshape(b, t, a.shape[-1])
        r4 = lambda a: a.reshape(b, nbp, a.shape[-2], TILE)
        mk, mv = _mem_kv(mem_prompt, mem_norm_g[l].reshape(1, d), w_mem_kv[l].astype(BF16))
        kmean_pad = jnp.pad(kmean.reshape(b, nbp, D_A), ((0, 0), (0, LANES - nbp), (0, 0)))
        oa = _moba_prompt(r3(qa), r3(ka_bf), r4(va_t), kmean_pad)
        ob = _dsa_prompt(r3(qb), r3(iq).astype(BF16), r3(iw), r3(ik_bf), r3(kb_bf), r4(vb_t))
        xp = _merge(xp, oa, r3(ga), ob, r3(gb), r3(qm), r3(gm), mk, mv, wo_bf, fg, TILE, last)
        for lst, a, h in ((outs[0], ka, H_A), (outs[1], va, H_A), (outs[2], kb, H_B), (outs[3], vb, H_B)):
            lst.append(jnp.transpose(a.reshape(b, h, HEAD_DIM, t), (0, 3, 1, 2)))
        outs[4].append(jnp.transpose(ik, (0, 2, 1)))
        outs[5].append(mk.reshape(b, N_MEM, H_M, HEAD_DIM))
        outs[6].append(mv.reshape(b, N_MEM, H_M, HEAD_DIM))

        n_s = nseq * ts
        tm_s = n_s if n_s <= TILE else TILE
        (qa, ka, va, ga, qb, kb, vb, gb, iq, ik, iw, qm, gm,
         _, _, _, _, _, _) = _project(xs.reshape(n_s, d), g, w_pad, tabs_s, n_s // tm_s, tm_s, False)
        s3 = lambda a: a.reshape(nseq, ts, a.shape[-1])
        padn = lambda a: jnp.pad(s3(a), ((0, 0), (0, LANES - ts), (0, 0)))
        oa = _moba_decode(l, page_table, s3(qa), _cache_view(cache_k_a), _cache_view(cache_v_a), padn(ka), padn(va))
        iq_rows = s3(iq).reshape(nseq, ts, IDX_HEADS, IDX_DIM).transpose(0, 2, 1, 3).reshape(
            nseq, IDX_HEADS * ts, IDX_DIM)
        iw_rows = (s3(iw)[:, :, :IDX_HEADS] * (IDX_HEADS ** -0.5)).transpose(0, 2, 1).reshape(
            nseq, IDX_HEADS * ts, 1)
        iw_rows = jnp.broadcast_to(iw_rows, (nseq, IDX_HEADS * ts, LANES))
        bias = _idx_decode(l, page_table, iq_rows, iw_rows, jnp.transpose(cache_k_idx, (0, 1, 3, 2)), padn(ik), ts)
        ob = _dsa_decode(l, page_table, s3(qb), bias, _cache_view(cache_k_b), _cache_view(cache_v_b),
                         padn(kb), padn(vb))
        mks = cache_mem_k[l].reshape(nseq, N_MEM, D_M)
        mvs = cache_mem_v[l].reshape(nseq, N_MEM, D_M)
        xs = _merge(xs, oa, s3(ga), ob, s3(gb), s3(qm), s3(gm), mks, mvs, wo_bf, fg, ts, last)
        for lst, a, shp in ((outs[7], ka, (nseq, ts, H_A, HEAD_DIM)), (outs[8], va, (nseq, ts, H_A, HEAD_DIM)),
                            (outs[9], kb, (nseq, ts, H_B, HEAD_DIM)), (outs[10], vb, (nseq, ts, H_B, HEAD_DIM)),
                            (outs[11], ik, (nseq, ts, IDX_DIM))):
            lst.append(a.reshape(shp))

    stk = [jnp.stack(o) for o in outs]
    return (xp, xs, stk[0], stk[1], stk[2], stk[3], stk[4], stk[5], stk[6],
            stk[7], stk[8], stk[9], stk[10], stk[11])
```
